```python
import math
import jax, jax.numpy as jnp
from jax import lax
import numpy as np


D_MODEL = 1024
BATCH = 16
SEQ = 256
DEPTH = 2
DEC_BATCH = 2
DEC_SEQ = 2048
PAST_LEN = 256

GRID_W = 64
ROPE_THETA = 10000.0
Q_BLOCK = 128
EPS = 1e-6
N_MOD = 9
D_FF = ((8 * D_MODEL // 3 + 127) // 128) * 128

W_SCONV = D_MODEL // 4
SCONV_K = 3
N_HEADS_DIFF = 4
DV_DIFF = D_MODEL // 4 // N_HEADS_DIFF
DK_DIFF = DV_DIFF // 2
W_CCM = D_MODEL // 4
CCM_K = 31
N_HEADS_GQA = 4
N_KV_GQA = 2
HD_GQA = D_MODEL // 4 // N_HEADS_GQA
G_GQA = N_HEADS_GQA // N_KV_GQA

SPLIT_SIZES = (W_SCONV, W_SCONV, W_SCONV,
               N_HEADS_DIFF * 2 * DK_DIFF, N_HEADS_DIFF * 2 * DK_DIFF, N_HEADS_DIFF * DV_DIFF,
               W_CCM, W_CCM,
               N_HEADS_GQA * HD_GQA, N_KV_GQA * HD_GQA, N_KV_GQA * HD_GQA)
MIX_IN = 3 * W_SCONV + 2 * N_HEADS_DIFF * 2 * DK_DIFF + N_HEADS_DIFF * DV_DIFF + 2 * W_CCM + N_HEADS_GQA * HD_GQA + 2 * N_KV_GQA * HD_GQA
MIX_WIDTH = W_SCONV + N_HEADS_DIFF * DV_DIFF + W_CCM + N_HEADS_GQA * HD_GQA

kernel_name = "hybrid_diffusion_parallel_groups_step"


def rms_norm(x, g):
    xf = x.astype(jnp.float32)
    y = xf * lax.rsqrt(jnp.mean(xf * xf, axis=-1, keepdims=True) + EPS)
    return (y * g.astype(jnp.float32)).astype(x.dtype)


def layer_norm(x, g, b):
    xf = x.astype(jnp.float32)
    mu = jnp.mean(xf, axis=-1, keepdims=True)
    var = jnp.mean(jnp.square(xf - mu), axis=-1, keepdims=True)
    y = (xf - mu) * lax.rsqrt(var + EPS)
    return (y * g.astype(jnp.float32) + b.astype(jnp.float32)).astype(x.dtype)


def depthwise_conv(x, w):
    k = w.shape[0]
    return lax.conv_general_dilated(
        x, w[:, None, :].astype(x.dtype), window_strides=(1,), padding=[(k // 2, k // 2)],
        dimension_numbers=("NWC", "WIO", "NWC"), feature_group_count=x.shape[-1])


def swiglu(h, wg, wu, wd):
    return (jax.nn.silu(h @ wg) * (h @ wu)) @ wd


def axial_angles(n_tokens, rot_dim):
    n_rows = n_tokens // GRID_W
    row = jnp.repeat(jnp.arange(n_rows, dtype=jnp.float32), GRID_W)
    col = jnp.tile(jnp.arange(GRID_W, dtype=jnp.float32), n_rows)
    n_freq = rot_dim // 4
    inv = ROPE_THETA ** (-jnp.arange(n_freq, dtype=jnp.float32) / n_freq)
    return row[:, None] * inv, col[:, None] * inv


def _rotate(x, ang):
    m = x.shape[-1] // 2
    shape = (ang.shape[0],) + (1,) * (x.ndim - 3) + (m,)
    cos = jnp.cos(ang).reshape(shape).astype(x.dtype)
    sin = jnp.sin(ang).reshape(shape).astype(x.dtype)
    x1, x2 = x[..., :m], x[..., m:]
    return jnp.concatenate([x1 * cos - x2 * sin, x1 * sin + x2 * cos], axis=-1)


def axial_rope(x, angles):
    ang_row, ang_col = angles
    half = x.shape[-1] // 2
    return jnp.concatenate([_rotate(x[..., :half], ang_row), _rotate(x[..., half:], ang_col)], axis=-1)


def _to_blocks(q):
    b, t = q.shape[:2]
    return q.reshape((b, t // Q_BLOCK, Q_BLOCK) + q.shape[2:]).swapaxes(0, 1)


def _from_blocks(o):
    nb, b = o.shape[:2]
    return o.swapaxes(0, 1).reshape((b, nb * Q_BLOCK) + o.shape[3:])


def diff_attention(q, k, v, lam):
    scale = q.shape[-1] ** -0.5

    def block(qb):
        s = jnp.einsum("bqhmd,bshmd->bhmqs", qb, k).astype(jnp.float32) * scale
        p = jax.nn.softmax(s, axis=-1)
        w = p[:, :, 0] - lam * p[:, :, 1]
        return jnp.einsum("bhqs,bshd->bqhd", w.astype(v.dtype), v)

    return _from_blocks(lax.map(block, _to_blocks(q)))


def gqa_attention(q, k, v):
    scale = q.shape[-1] ** -0.5

    def block(qb):
        s = jnp.einsum("bqkgd,bskd->bkgqs", qb, k).astype(jnp.float32) * scale
        p = jax.nn.softmax(s, axis=-1)
        return jnp.einsum("bkgqs,bskd->bqkgd", p.astype(v.dtype), v)

    return _from_blocks(lax.map(block, _to_blocks(q)))


def token_mix(h, l, p, angles, ctx_kv):
    bsz, t, _ = h.shape
    points = []
    acc = 0
    for s in SPLIT_SIZES[:-1]:
        acc += s
        points.append(acc)
    proj = h @ p["w_mix_in"][l]
    a_b, a_c, a_h, b_q, b_k, b_v, c_a, c_g, d_q, d_k, d_v = jnp.split(proj, points, axis=-1)

    a_out = a_b * depthwise_conv(a_c * a_h, p["sconv_w"][l])

    bq = b_q.reshape(bsz, t, N_HEADS_DIFF, 2, DK_DIFF)
    bk = b_k.reshape(bsz, t, N_HEADS_DIFF, 2, DK_DIFF)
    bv = b_v.reshape(bsz, t, N_HEADS_DIFF, DV_DIFF)
    dq = rms_norm(d_q.reshape(bsz, t, N_HEADS_GQA, HD_GQA), p["gqa_qnorm"][l])
    dk = rms_norm(d_k.reshape(bsz, t, N_KV_GQA, HD_GQA), p["gqa_knorm"][l])
    dv = d_v.reshape(bsz, t, N_KV_GQA, HD_GQA)
    own = (bk, bv, dk, dv)

    if angles is not None:
        ang_diff, ang_gqa = angles
        bq = axial_rope(bq, ang_diff)
        bk = axial_rope(bk, ang_diff)
        dq = axial_rope(dq, ang_gqa)
        dk = axial_rope(dk, ang_gqa)
    if ctx_kv is not None:
        cbk, cbv, cdk, cdv = ctx_kv
        bk = jnp.concatenate([cbk, bk], axis=1)
        bv = jnp.concatenate([cbv, bv], axis=1)
        dk = jnp.concatenate([cdk, dk], axis=1)
        dv = jnp.concatenate([cdv, dv], axis=1)

    lam_init = 0.8 - 0.6 * math.exp(-0.3 * l)
    f32 = jnp.float32
    lam = (jnp.exp(jnp.sum(p["diff_lq1"][l].astype(f32) * p["diff_lk1"][l].astype(f32)))
           - jnp.exp(jnp.sum(p["diff_lq2"][l].astype(f32) * p["diff_lk2"][l].astype(f32)))
           + lam_init)
    b_out = diff_attention(bq, bk, bv, lam)
    b_out = (rms_norm(b_out, p["diff_subln"][l]) * (1.0 - lam_init)).reshape(bsz, t, -1)

    u = c_a * jax.nn.sigmoid(c_g)
    u = depthwise_conv(u, p["ccm_dw_w"][l]) + p["ccm_dw_b"][l]
    u = jax.nn.silu(layer_norm(u, p["ccm_ln_g"][l], p["ccm_ln_b"][l]))
    c_out = u @ p["ccm_pw"][l]

    d_out = gqa_attention(dq.reshape(bsz, t, N_KV_GQA, G_GQA, HD_GQA), dk, dv).reshape(bsz, t, -1)

    out = jnp.concatenate([a_out, b_out, c_out, d_out], axis=-1) @ p["w_mix_out"][l]
    return out, own


def trunk_layer(x, cond, l, p, angles, ctx_kv):
    m = jax.nn.silu(cond) @ p["w_ada"][l] + p["b_ada"][l]
    m = m.reshape(m.shape[0], 1, N_MOD, D_MODEL)
    shift, scale, gate = m[:, :, 0::3], m[:, :, 1::3], m[:, :, 2::3]

    def pre(y, i):
        return rms_norm(y, p["norm_pre"][l, i]) * (1.0 + scale[:, :, i]) + shift[:, :, i]

    def post(y, i):
        return gate[:, :, i] * rms_norm(y, p["norm_post"][l, i])

    x = x + 0.5 * post(swiglu(pre(x, 0), p["ffn1_gate"][l], p["ffn1_up"][l], p["ffn1_down"][l]), 0)
    y, own = token_mix(pre(x, 1), l, p, angles, ctx_kv)
    x = x + post(y, 1)
    x = x + 0.5 * post(swiglu(pre(x, 2), p["ffn2_gate"][l], p["ffn2_up"][l], p["ffn2_down"][l]), 2)
    return x, own


def setup_inputs(seed: int = 0) -> dict:
    key = jax.random.key(seed)
    ks = iter(jax.random.split(key, 48))
    D = D_MODEL

    def nrm(shape, s):
        return jax.random.normal(next(ks), shape, jnp.float32) * s

    def gain(shape):
        return 1.0 + nrm(shape, 0.02)

    return {
        "x_prompt": nrm((BATCH, SEQ, D), 1.0),
        "x_sample": nrm((DEC_BATCH, DEC_SEQ, D), 1.0),
        "cache_diff_k": nrm((DEC_BATCH, DEPTH, PAST_LEN, N_HEADS_DIFF, 2, DK_DIFF), 1.0),
        "cache_diff_v": nrm((DEC_BATCH, DEPTH, PAST_LEN, N_HEADS_DIFF, DV_DIFF), 1.0),
        "cache_gqa_k": nrm((DEC_BATCH, DEPTH, PAST_LEN, N_KV_GQA, HD_GQA), 1.0),
        "cache_gqa_v": nrm((DEC_BATCH, DEPTH, PAST_LEN, N_KV_GQA, HD_GQA), 1.0),
        "c": nrm((DEC_BATCH, D), 1.0),
        "c_ctx": nrm((D,), 1.0),
        "w_ada": nrm((DEPTH, D, N_MOD * D), 0.5 * D ** -0.5),
        "b_ada": nrm((DEPTH, N_MOD * D), 0.02),
        "norm_pre": gain((DEPTH, 3, D)),
        "norm_post": gain((DEPTH, 3, D)),
        "ffn1_gate": nrm((DEPTH, D, D_FF), D ** -0.5),
        "ffn1_up": nrm((DEPTH, D, D_FF), D ** -0.5),
        "ffn1_down": nrm((DEPTH, D_FF, D), D_FF ** -0.5),
        "ffn2_gate": nrm((DEPTH, D, D_FF), D ** -0.5),
        "ffn2_up": nrm((DEPTH, D, D_FF), D ** -0.5),
        "ffn2_down": nrm((DEPTH, D_FF, D), D_FF ** -0.5),
        "w_mix_in": nrm((DEPTH, D, MIX_IN), D ** -0.5),
        "w_mix_out": nrm((DEPTH, MIX_WIDTH, D), MIX_WIDTH ** -0.5),
        "sconv_w": nrm((DEPTH, SCONV_K, W_SCONV), SCONV_K ** -0.5),
        "diff_lq1": nrm((DEPTH, DK_DIFF), 0.1),
        "diff_lk1": nrm((DEPTH, DK_DIFF), 0.1),
        "diff_lq2": nrm((DEPTH, DK_DIFF), 0.1),
        "diff_lk2": nrm((DEPTH, DK_DIFF), 0.1),
        "diff_subln": gain((DEPTH, DV_DIFF)),
        "ccm_dw_w": nrm((DEPTH, CCM_K, W_CCM), CCM_K ** -0.5),
        "ccm_dw_b": nrm((DEPTH, W_CCM), 0.02),
        "ccm_ln_g": gain((DEPTH, W_CCM)),
        "ccm_ln_b": nrm((DEPTH, W_CCM), 0.02),
        "ccm_pw": nrm((DEPTH, W_CCM, W_CCM), W_CCM ** -0.5),
        "gqa_qnorm": gain((DEPTH, HD_GQA)),
        "gqa_knorm": gain((DEPTH, HD_GQA)),
    }


def reference(x_prompt, x_sample, cache_diff_k, cache_diff_v, cache_gqa_k, cache_gqa_v, c, c_ctx,
              w_ada, b_ada, norm_pre, norm_post,
              ffn1_gate, ffn1_up, ffn1_down, ffn2_gate, ffn2_up, ffn2_down,
              w_mix_in, w_mix_out, sconv_w,
              diff_lq1, diff_lk1, diff_lq2, diff_lk2, diff_subln,
              ccm_dw_w, ccm_dw_b, ccm_ln_g, ccm_ln_b, ccm_pw,
              gqa_qnorm, gqa_knorm):
    p = dict(w_ada=w_ada, b_ada=b_ada, norm_pre=norm_pre, norm_post=norm_post,
             ffn1_gate=ffn1_gate, ffn1_up=ffn1_up, ffn1_down=ffn1_down,
             ffn2_gate=ffn2_gate, ffn2_up=ffn2_up, ffn2_down=ffn2_down,
             w_mix_in=w_mix_in, w_mix_out=w_mix_out, sconv_w=sconv_w,
             diff_lq1=diff_lq1, diff_lk1=diff_lk1, diff_lq2=diff_lq2, diff_lk2=diff_lk2,
             diff_subln=diff_subln, ccm_dw_w=ccm_dw_w, ccm_dw_b=ccm_dw_b,
             ccm_ln_g=ccm_ln_g, ccm_ln_b=ccm_ln_b, ccm_pw=ccm_pw,
             gqa_qnorm=gqa_qnorm, gqa_knorm=gqa_knorm)

    x = x_prompt
    dks, dvs, gks, gvs = [], [], [], []
    for l in range(DEPTH):
        x, own = trunk_layer(x, c_ctx[None, :], l, p, None, None)
        dks.append(own[0])
        dvs.append(own[1])
        gks.append(own[2])
        gvs.append(own[3])
    y_prompt = x
    new_diff_k = jnp.stack(dks, axis=1)
    new_diff_v = jnp.stack(dvs, axis=1)
    new_gqa_k = jnp.stack(gks, axis=1)
    new_gqa_v = jnp.stack(gvs, axis=1)

    t_lat = x_sample.shape[1]
    angles = (axial_angles(t_lat, DK_DIFF), axial_angles(t_lat, HD_GQA))
    x = x_sample
    for l in range(DEPTH):
        ctx_kv = (cache_diff_k[:, l], cache_diff_v[:, l], cache_gqa_k[:, l], cache_gqa_v[:, l])
        x, _ = trunk_layer(x, c, l, p, angles, ctx_kv)
    y_sample = x

    return (y_prompt, y_sample, new_diff_k, new_diff_v, new_gqa_k, new_gqa_v)
```

```python
import functools
import math

import jax
import jax.numpy as jnp
from jax import lax
from jax.experimental import pallas as pl
from jax.experimental.pallas import tpu as pltpu

F32 = jnp.float32
BF16 = jnp.bfloat16

D_MODEL = 1024
BATCH = 16
SEQ = 256
DEPTH = 2
DEC_BATCH = 2
DEC_SEQ = 2048
PAST_LEN = 256
GRID_W = 64
ROPE_THETA = 10000.0
EPS = 1e-6
N_MOD = 9
D_FF = 2816
W_GRP = 256
DK_DIFF = 32
DV_DIFF = 64
HD_GQA = 64
CCM_K = 31
SCONV_K = 3
MIX_IN = 2560

N_CTX = BATCH * SEQ
N_LAT = DEC_BATCH * DEC_SEQ
N_TOK = N_CTX + N_LAT
N_COND = 8

VMEM_LIMIT = 56 * 1024 * 1024

TM_FFN = 1024
TF_FFN = 256
TM_MIX = 512
TQ_LAT = 256
T_CONV = 2048
R_CONV = 64
CONV_HALO = 16
TM_OUT = 1024
TN_ADA = 1536


def _cond_of_tile(i, tm):
    n_ctx_tiles = N_CTX // tm
    per_b = DEC_SEQ // tm
    return jnp.where(i < n_ctx_tiles, 0, 1 + (i - n_ctx_tiles) // per_b)


def _dot(a, b):
    return jnp.dot(a, b, preferred_element_type=F32)


def _dot_nt(a, b):
    return lax.dot_general(a, b, (((1,), (1,)), ((), ())), preferred_element_type=F32)


def _rms(x, g):
    ms = jnp.mean(x * x, axis=-1, keepdims=True)
    return x * lax.rsqrt(ms + EPS) * g


def _group_mean(xx, width, group):
    r = lax.broadcasted_iota(jnp.int32, (width, width), 0) // group
    c = lax.broadcasted_iota(jnp.int32, (width, width), 1) // group
    gmat = jnp.where(r == c, 1.0 / group, 0.0).astype(BF16)
    hi = xx.astype(BF16)
    lo = (xx - hi.astype(F32)).astype(BF16)
    return _dot(hi, gmat) + _dot(lo, gmat)


def _ada_kernel(c_ref, w_ref, b_ref, o_ref):
    cnd = c_ref[...]
    s = (cnd * jax.nn.sigmoid(cnd)).astype(BF16)
    o_ref[...] = _dot(s, w_ref[...].astype(BF16)) + b_ref[...]


def _ada(conds, w_ada, b_ada):
    n_n = (N_MOD * D_MODEL) // TN_ADA
    return pl.pallas_call(
        _ada_kernel,
        grid=(DEPTH, n_n),
        in_specs=[
            pl.BlockSpec((N_COND, D_MODEL), lambda l, n: (0, 0)),
            pl.BlockSpec((None, D_MODEL, TN_ADA), lambda l, n: (l, 0, n)),
            pl.BlockSpec((None, 1, TN_ADA), lambda l, n: (l, 0, n)),
        ],
        out_specs=pl.BlockSpec((None, N_COND, TN_ADA), lambda l, n: (l, 0, n)),
        out_shape=jax.ShapeDtypeStruct((DEPTH, N_COND, N_MOD * D_MODEL), F32),
        compiler_params=pltpu.CompilerParams(
            dimension_semantics=("arbitrary", "arbitrary"), vmem_limit_bytes=VMEM_LIMIT),
        name="ada",
    )(conds, w_ada, b_ada.reshape(DEPTH, 1, N_MOD * D_MODEL))


def _ffn_kernel(sub, x_ref, mod_ref, gpre_ref, gpost_ref, wg_ref, wu_ref, wd_ref, o_ref,
                h_ref, acc_ref):
    j = pl.program_id(1)

    @pl.when(j == 0)
    def _():
        shift = mod_ref[0, pl.ds(3 * sub, 1), :]
        scale = mod_ref[0, pl.ds(3 * sub + 1, 1), :]
        h = _rms(x_ref[...], gpre_ref[...]) * (1.0 + scale) + shift
        h_ref[...] = h.astype(BF16)
        acc_ref[...] = jnp.zeros_like(acc_ref)

    h = h_ref[...]
    g = _dot(h, wg_ref[...].astype(BF16))
    u = _dot(h, wu_ref[...].astype(BF16))
    a = (g * jax.nn.sigmoid(g)) * u
    acc_ref[...] += _dot(a.astype(BF16), wd_ref[...].astype(BF16))

    @pl.when(j == pl.num_programs(1) - 1)
    def _():
        gate = mod_ref[0, pl.ds(3 * sub + 2, 1), :]
        o_ref[...] = x_ref[...] + 0.5 * (gate * _rms(acc_ref[...], gpost_ref[...]))


def _ffn(x, mod_l, gpre, gpost, wg, wu, wd, l, sub):
    tm, tf = TM_FFN, TF_FFN
    return pl.pallas_call(
        functools.partial(_ffn_kernel, sub),
        grid=(N_TOK // tm, D_FF // tf),
        in_specs=[
            pl.BlockSpec((tm, D_MODEL), lambda i, j: (i, 0)),
            pl.BlockSpec((1, N_MOD, D_MODEL), lambda i, j: (_cond_of_tile(i, tm), 0, 0)),
            pl.BlockSpec((1, D_MODEL), lambda i, j: (0, 0)),
            pl.BlockSpec((1, D_MODEL), lambda i, j: (0, 0)),
            pl.BlockSpec((None, D_MODEL, tf), lambda i, j: (l, 0, j)),
            pl.BlockSpec((None, D_MODEL, tf), lambda i, j: (l, 0, j)),
            pl.BlockSpec((None, tf, D_MODEL), lambda i, j: (l, j, 0)),
        ],
        out_specs=pl.BlockSpec((tm, D_MODEL), lambda i, j: (i, 0)),
        out_shape=jax.ShapeDtypeStruct((N_TOK, D_MODEL), F32),
        scratch_shapes=[pltpu.VMEM((tm, D_MODEL), BF16), pltpu.VMEM((tm, D_MODEL), F32)],
        compiler_params=pltpu.CompilerParams(
            dimension_semantics=("parallel", "arbitrary"), vmem_limit_bytes=VMEM_LIMIT),
        name=f"ffn{sub}",
    )(x, mod_l, gpre, gpost, wg, wu, wd)


def _swap_blocks(x, blk):
    w = x.shape[-1]
    lane = lax.broadcasted_iota(jnp.int32, x.shape, x.ndim - 1)
    first = (lane % (2 * blk)) < blk
    return jnp.where(first, pltpu.roll(x, w - blk, x.ndim - 1), pltpu.roll(x, blk, x.ndim - 1))


def _mixproj_kernel(x_ref, mod_ref, gpre_ref, w_ref, qg_ref, kg_ref, cd_ref, sd_ref, cg_ref,
                    sg_ref, cin_ref, q_ref, kv_ref, own_ref):
    shift = mod_ref[0, pl.ds(3, 1), :]
    scale = mod_ref[0, pl.ds(4, 1), :]
    h = (_rms(x_ref[...], gpre_ref[...]) * (1.0 + scale) + shift).astype(BF16)

    a = _dot(h, w_ref[:, 0:768])
    cin_ref[:, 0:256] = a[:, 0:256]
    cin_ref[:, 256:512] = a[:, 256:512] * a[:, 512:768]
    c = _dot(h, w_ref[:, 1536:2048])
    cin_ref[:, 512:768] = c[:, 0:256] * jax.nn.sigmoid(c[:, 256:512])

    cd = cd_ref[...]
    sd = sd_ref[...]
    b = _dot(h, w_ref[:, 768:1536])
    bq, bk, bv = b[:, 0:256], b[:, 256:512], b[:, 512:768]
    own_ref[:, 0:256] = bk
    own_ref[:, 256:512] = bv
    bq_r = (bq * cd + _swap_blocks(bq, DK_DIFF // 4) * sd) * (DK_DIFF ** -0.5)
    bk_r = bk * cd + _swap_blocks(bk, DK_DIFF // 4) * sd
    q_ref[:, 0:256] = bq_r.astype(BF16)
    kv_ref[:, 0:256] = bk_r.astype(BF16)
    kv_ref[:, 256:512] = bv.astype(BF16)

    cg = cg_ref[...]
    sg = sg_ref[...]
    d = _dot(h, w_ref[:, 2048:2560])
    dq, dk, dv = d[:, 0:256], d[:, 256:384], d[:, 384:512]
    dq_n = dq * lax.rsqrt(_group_mean(dq * dq, 256, HD_GQA) + EPS) * qg_ref[...]
    dk_n = dk * lax.rsqrt(_group_mean(dk * dk, 128, HD_GQA) + EPS) * kg_ref[...]
    own_ref[:, 512:640] = dk_n
    own_ref[:, 640:768] = dv
    dq_r = (dq_n * cg + _swap_blocks(dq_n, HD_GQA // 4) * sg) * (HD_GQA ** -0.5)
    dk_r = dk_n * cg[:, 0:128] + _swap_blocks(dk_n, HD_GQA // 4) * sg[:, 0:128]
    q_ref[:, 256:512] = dq_r.astype(BF16)
    kv_ref[:, 512:640] = dk_r.astype(BF16)
    kv_ref[:, 640:768] = dv.astype(BF16)
    kv_ref[:, 768:896] = pltpu.roll(dk_r, HD_GQA, 1).astype(BF16)
    kv_ref[:, 896:1024] = pltpu.roll(dv, HD_GQA, 1).astype(BF16)


def _mixproj(x, mod_l, gpre, w_in, qg, kg, tables):
    tm = TM_MIX
    n_ctx_tiles = N_CTX // tm
    per_b = DEC_SEQ // tm

    def tab_map(i):
        return (jnp.where(i < n_ctx_tiles, 0, 1 + (i - n_ctx_tiles) % per_b), 0)

    tab_spec = pl.BlockSpec((tm, 256), tab_map)
    row = lambda w: pl.BlockSpec((tm, w), lambda i: (i, 0))
    return pl.pallas_call(
        _mixproj_kernel,
        grid=(N_TOK // tm,),
        in_specs=[
            row(D_MODEL),
            pl.BlockSpec((1, N_MOD, D_MODEL), lambda i: (_cond_of_tile(i, tm), 0, 0)),
            pl.BlockSpec((1, D_MODEL), lambda i: (0, 0)),
            pl.BlockSpec((D_MODEL, MIX_IN), lambda i: (0, 0)),
            pl.BlockSpec((1, 256), lambda i: (0, 0)),
            pl.BlockSpec((1, 128), lambda i: (0, 0)),
            tab_spec, tab_spec, tab_spec, tab_spec,
        ],
        out_specs=[row(768), row(512), row(1024), row(768)],
        out_shape=[
            jax.ShapeDtypeStruct((N_TOK, 768), F32),
            jax.ShapeDtypeStruct((N_TOK, 512), BF16),
            jax.ShapeDtypeStruct((N_TOK, 1024), BF16),
            jax.ShapeDtypeStruct((N_TOK, 768), F32),
        ],
        compiler_params=pltpu.CompilerParams(
            dimension_semantics=("arbitrary",), vmem_limit_bytes=VMEM_LIMIT),
        name="mixproj",
    )(x, mod_l, gpre, w_in, qg, kg, *tables)


def _rope_tables(tm):
    t = jnp.arange(DEC_SEQ)
    row = (t // GRID_W).astype(F32)
    col = (t % GRID_W).astype(F32)

    def build(rot_dim, width):
        n_freq = rot_dim // 4
        inv = ROPE_THETA ** (-jnp.arange(n_freq, dtype=F32) / n_freq)
        ang_row = row[:, None] * inv
        ang_col = col[:, None] * inv
        ang = jnp.concatenate([ang_row, ang_row, ang_col, ang_col], axis=-1)
        sign = jnp.concatenate([-jnp.ones(n_freq), jnp.ones(n_freq)] * 2).astype(F32)
        reps = width // rot_dim
        cos = jnp.tile(jnp.cos(ang), (1, reps))
        sin = jnp.tile(jnp.sin(ang) * sign, (1, reps))
        cos = jnp.concatenate([jnp.ones((tm, width), F32), cos], axis=0)
        sin = jnp.concatenate([jnp.zeros((tm, width), F32), sin], axis=0)
        return cos, sin

    cd, sd = build(DK_DIFF, 256)
    cg, sg = build(HD_GQA, 256)
    return cd, sd, cg, sg


def _softmax_parts(s):
    e = jnp.exp(s - jnp.max(s, axis=-1, keepdims=True))
    return e, 1.0 / jnp.sum(e, axis=-1, keepdims=True)


def _attn_kernel(lam_init, q_ref, kv_ref, lq1_ref, lk1_ref, lq2_ref, lk2_ref, subln_ref, *rest):
    o_ref = rest[-1]
    tq = q_ref.shape[0]
    lam = (jnp.exp(jnp.sum(lq1_ref[...] * lk1_ref[...], axis=-1, keepdims=True))
           - jnp.exp(jnp.sum(lq2_ref[...] * lk2_ref[...], axis=-1, keepdims=True))
           + lam_init)

    qb = q_ref[:, 0:256]
    kb = kv_ref[:, 0:256]
    vb = kv_ref[:, 256:512]
    lane = lax.broadcasted_iota(jnp.int32, (1, 256), 1)
    zero = jnp.zeros((), BF16)
    acc_b = jnp.zeros((tq, 256), F32)
    for hd in range(4):
        ps = []
        for m in range(2):
            qm = jnp.where(lane // DK_DIFF == 2 * hd + m, qb, zero)
            e, inv = _softmax_parts(_dot_nt(qm, kb))
            ps.append(e * inv)
        w = ps[0] - lam * ps[1]
        o = _dot(w.astype(BF16), vb)
        acc_b = acc_b + jnp.where(lane // DV_DIFF == hd, o, 0.0)
    nrm = lax.rsqrt(_group_mean(acc_b * acc_b, 256, DV_DIFF) + EPS)
    o_ref[:, 0:256] = (acc_b * nrm * subln_ref[...] * (1.0 - lam_init)).astype(BF16)

    lane128 = lax.broadcasted_iota(jnp.int32, (1, 128), 1)
    for half in range(2):
        acc = jnp.zeros((tq, 128), F32)
        qd = q_ref[:, 256 + 128 * half:384 + 128 * half]
        for blk in range(2):
            hd = 2 * half + blk
            base = 512 if hd in (0, 3) else 768
            kd = kv_ref[:, base:base + 128]
            vd = kv_ref[:, base + 128:base + 256]
            qm = jnp.where(lane128 // HD_GQA == blk, qd, zero)
            e, inv = _softmax_parts(_dot_nt(qm, kd))
            o = _dot(e.astype(BF16), vd) * inv
            acc = acc + jnp.where(lane128 // HD_GQA == blk, o, 0.0)
        o_ref[:, 256 + 128 * half:384 + 128 * half] = acc.astype(BF16)


def _attn_params(lq1, lk1, lq2, lk2, subln):
    vec = pl.BlockSpec((1, DK_DIFF), lambda *_: (0, 0))
    specs = [vec, vec, vec, vec, pl.BlockSpec((1, 256), lambda *_: (0, 0))]
    return specs, (lq1, lk1, lq2, lk2, subln)


def _attn_ctx(q_arr, kv_arr, lam_init, params):
    pspecs, pargs = _attn_params(*params)
    return pl.pallas_call(
        functools.partial(_attn_kernel, lam_init),
        grid=(BATCH,),
        in_specs=[pl.BlockSpec((SEQ, 512), lambda b: (b, 0)),
                  pl.BlockSpec((SEQ, 1024), lambda b: (b, 0))] + pspecs,
        out_specs=pl.BlockSpec((SEQ, 512), lambda b: (b, 0)),
        out_shape=jax.ShapeDtypeStruct((N_TOK, 512), BF16),
        compiler_params=pltpu.CompilerParams(
            dimension_semantics=("arbitrary",), vmem_limit_bytes=VMEM_LIMIT),
        name="attn_ctx",
    )(q_arr, kv_arr, *pargs)


def _attn_lat(q_arr, kv_lat, att, lam_init, params):
    tq = TQ_LAT
    nq = DEC_SEQ // tq
    pspecs, pargs = _attn_params(*params)
    row_map = lambda b, i: (N_CTX // tq + b * nq + i, 0)
    return pl.pallas_call(
        functools.partial(_attn_kernel, lam_init),
        grid=(DEC_BATCH, nq),
        in_specs=[pl.BlockSpec((tq, 512), row_map),
                  pl.BlockSpec((None, PAST_LEN + DEC_SEQ, 1024), lambda b, i: (b, 0, 0))]
        + pspecs + [pl.BlockSpec(memory_space=pl.ANY)],
        out_specs=pl.BlockSpec((tq, 512), row_map),
        out_shape=jax.ShapeDtypeStruct((N_TOK, 512), BF16),
        input_output_aliases={7: 0},
        compiler_params=pltpu.CompilerParams(
            dimension_semantics=("arbitrary", "arbitrary"), vmem_limit_bytes=VMEM_LIMIT),
        name="attn_lat",
    )(q_arr, kv_lat, *pargs, att)


def _conv_kernel(seq_len, cin_ref, wsc_ref, wdw_ref, bdw_ref, lng_ref, lnb_ref, pw_ref, *rest):
    o_ref, pad_ref = rest[-2], rest[-1]
    t_tile = cin_ref.shape[0]
    n_seq = t_tile // seq_len
    stride = seq_len + CONV_HALO
    chunks_per_seq = seq_len // R_CONV
    rows = R_CONV

    for q in range(n_seq + 1):
        pad_ref[q * stride:q * stride + CONV_HALO, :] = jnp.zeros((CONV_HALO, 512), F32)
    for q in range(n_seq):
        pad_ref[CONV_HALO + q * stride:CONV_HALO + q * stride + seq_len, :] = (
            cin_ref[q * seq_len:(q + 1) * seq_len, 256:768])

    pw = pw_ref[...]
    bdw = bdw_ref[...]
    lng = lng_ref[...]
    lnb = lnb_ref[...]

    def body(c, carry):
        q = c // chunks_per_seq
        r = c % chunks_per_seq
        base = pl.multiple_of(q * stride + r * rows, 8)
        row0 = pl.multiple_of(c * rows, 8)
        win = pad_ref[pl.ds(base, rows + 2 * CONV_HALO), :]
        win_s = win[:, 0:256]
        win_u = win[:, 256:512]

        s_acc = None
        for k in range(SCONV_K):
            off = CONV_HALO + k - SCONV_K // 2
            term = wsc_ref[pl.ds(k, 1), :] * win_s[off:off + rows]
            s_acc = term if s_acc is None else s_acc + term
        a_out = cin_ref[pl.ds(row0, rows), 0:256] * s_acc

        acc = None
        for res in range(8):
            part = None
            for a in range(-2, 2):
                k = 8 * a + res + CCM_K // 2
                if 0 <= k < CCM_K:
                    lo = CONV_HALO + 8 * a
                    term = wdw_ref[pl.ds(k, 1), :] * win_u[lo:lo + rows + 8]
                    part = term if part is None else part + term
            part = part[res:res + rows]
            acc = part if acc is None else acc + part
        u = acc + bdw
        mu = jnp.mean(u, axis=-1, keepdims=True)
        var = jnp.mean(jnp.square(u - mu), axis=-1, keepdims=True)
        y = (u - mu) * lax.rsqrt(var + EPS) * lng + lnb
        y = y * jax.nn.sigmoid(y)
        c_out = _dot(y.astype(BF16), pw)
        o_ref[pl.ds(row0, rows), 0:256] = a_out.astype(BF16)
        o_ref[pl.ds(row0, rows), 256:512] = c_out.astype(BF16)
        return carry

    lax.fori_loop(0, t_tile // rows, body, 0)


def _conv(cin, prev, seq_len, tile0, n_tiles, params):
    t = T_CONV
    n_seq = t // seq_len
    const = lambda shape: pl.BlockSpec(shape, lambda i: (0,) * len(shape))
    in_specs = [pl.BlockSpec((t, 768), lambda i: (tile0 + i, 0)),
                const((SCONV_K, 256)), const((CCM_K, 256)), const((1, 256)), const((1, 256)),
                const((1, 256)), const((256, 256))]
    args = [cin, *params]
    aliases = {}
    if prev is not None:
        in_specs.append(pl.BlockSpec(memory_space=pl.ANY))
        args.append(prev)
        aliases = {len(args) - 1: 0}
    return pl.pallas_call(
        functools.partial(_conv_kernel, seq_len),
        grid=(n_tiles,),
        in_specs=in_specs,
        out_specs=pl.BlockSpec((t, 512), lambda i: (tile0 + i, 0)),
        out_shape=jax.ShapeDtypeStruct((N_TOK, 512), BF16),
        scratch_shapes=[pltpu.VMEM((n_seq * (seq_len + CONV_HALO) + CONV_HALO, 512), F32)],
        input_output_aliases=aliases,
        compiler_params=pltpu.CompilerParams(
            dimension_semantics=("arbitrary",), vmem_limit_bytes=VMEM_LIMIT),
        name=f"conv{seq_len}",
    )(*args)


def _outproj_kernel(ac_ref, att_ref, x_ref, mod_ref, gpost_ref, w_ref, o_ref):
    y = (_dot(ac_ref[:, 0:256], w_ref[0:256, :]) + _dot(att_ref[:, 0:256], w_ref[256:512, :])
         + _dot(ac_ref[:, 256:512], w_ref[512:768, :]) + _dot(att_ref[:, 256:512], w_ref[768:1024, :]))
    gate = mod_ref[0, pl.ds(5, 1), :]
    o_ref[...] = x_ref[...] + gate * _rms(y, gpost_ref[...])


def _outproj(ac, att, x, mod_l, gpost, w_out):
    tm = TM_OUT
    row = lambda w: pl.BlockSpec((tm, w), lambda i: (i, 0))
    return pl.pallas_call(
        _outproj_kernel,
        grid=(N_TOK // tm,),
        in_specs=[row(512), row(512), row(D_MODEL),
                  pl.BlockSpec((1, N_MOD, D_MODEL), lambda i: (_cond_of_tile(i, tm), 0, 0)),
                  pl.BlockSpec((1, D_MODEL), lambda i: (0, 0)),
                  pl.BlockSpec((D_MODEL, D_MODEL), lambda i: (0, 0))],
        out_specs=row(D_MODEL),
        out_shape=jax.ShapeDtypeStruct((N_TOK, D_MODEL), F32),
        compiler_params=pltpu.CompilerParams(
            dimension_semantics=("arbitrary",), vmem_limit_bytes=VMEM_LIMIT),
        name="outproj",
    )(ac, att, x, mod_l, gpost, w_out)


def _swap_halves(a):
    return jnp.concatenate([a[..., HD_GQA:], a[..., :HD_GQA]], axis=-1)


def kernel(x_prompt, x_sample, cache_diff_k, cache_diff_v, cache_gqa_k, cache_gqa_v, c, c_ctx, w_ada, b_ada, norm_pre, norm_post, ffn1_gate, ffn1_up, ffn1_down, ffn2_gate, ffn2_up, ffn2_down, w_mix_in, w_mix_out, sconv_w, diff_lq1, diff_lk1, diff_lq2, diff_lk2, diff_subln, ccm_dw_w, ccm_dw_b, ccm_ln_g, ccm_ln_b, ccm_pw, gqa_qnorm, gqa_knorm):
    x = jnp.concatenate([x_prompt.reshape(N_CTX, D_MODEL), x_sample.reshape(N_LAT, D_MODEL)], axis=0)
    conds = jnp.concatenate(
        [c_ctx[None, :], c, jnp.zeros((N_COND - 1 - DEC_BATCH, D_MODEL), F32)], axis=0)
    mod = _ada(conds, w_ada, b_ada).reshape(DEPTH, N_COND, N_MOD, D_MODEL)
    tables = _rope_tables(TM_MIX)

    owns = []
    for l in range(DEPTH):
        mod_l = mod[l]
        lam_init = 0.8 - 0.6 * math.exp(-0.3 * l)
        x = _ffn(x, mod_l, norm_pre[l, 0][None], norm_post[l, 0][None],
                 ffn1_gate, ffn1_up, ffn1_down, l, 0)

        cin, q_arr, kv_arr, own = _mixproj(
            x, mod_l, norm_pre[l, 1][None], w_mix_in[l].astype(BF16),
            jnp.tile(gqa_qnorm[l], 4)[None], jnp.tile(gqa_knorm[l], 2)[None], tables)
        owns.append(own[:N_CTX])

        cdk = cache_gqa_k[:, l].reshape(DEC_BATCH, PAST_LEN, 128)
        cdv = cache_gqa_v[:, l].reshape(DEC_BATCH, PAST_LEN, 128)
        cache_kv = jnp.concatenate(
            [cache_diff_k[:, l].reshape(DEC_BATCH, PAST_LEN, 256),
             cache_diff_v[:, l].reshape(DEC_BATCH, PAST_LEN, 256),
             cdk, cdv, _swap_halves(cdk), _swap_halves(cdv)], axis=-1).astype(BF16)
        kv_lat = jnp.concatenate(
            [cache_kv, kv_arr[N_CTX:].reshape(DEC_BATCH, DEC_SEQ, 1024)], axis=1)

        aparams = (diff_lq1[l][None], diff_lk1[l][None], diff_lq2[l][None], diff_lk2[l][None],
                   jnp.tile(diff_subln[l], 4)[None])
        att = _attn_ctx(q_arr, kv_arr, lam_init, aparams)
        att = _attn_lat(q_arr, kv_lat, att, lam_init, aparams)

        cparams = (sconv_w[l], ccm_dw_w[l], ccm_dw_b[l][None], ccm_ln_g[l][None],
                   ccm_ln_b[l][None], ccm_pw[l].astype(BF16))
        ac = _conv(cin, None, SEQ, 0, N_CTX // T_CONV, cparams)
        ac = _conv(cin, ac, DEC_SEQ, N_CTX // T_CONV, N_LAT // T_CONV, cparams)

        x = _outproj(ac, att, x, mod_l, norm_post[l, 1][None], w_mix_out[l].astype(BF16))
        x = _ffn(x, mod_l, norm_pre[l, 2][None], norm_post[l, 2][None],
                 ffn2_gate, ffn2_up, ffn2_down, l, 2)

    own = jnp.stack([o.reshape(BATCH, SEQ, 768) for o in owns], axis=1)
    new_diff_k = own[..., 0:256].reshape(BATCH, DEPTH, SEQ, 4, 2, DK_DIFF)
    new_diff_v = own[..., 256:512].reshape(BATCH, DEPTH, SEQ, 4, DV_DIFF)
    new_gqa_k = own[..., 512:640].reshape(BATCH, DEPTH, SEQ, 2, HD_GQA)
    new_gqa_v = own[..., 640:768].reshape(BATCH, DEPTH, SEQ, 2, HD_GQA)
    y_prompt = x[:N_CTX].reshape(BATCH, SEQ, D_MODEL)
    y_sample = x[N_CTX:].reshape(DEC_BATCH, DEC_SEQ, D_MODEL)
    return (y_prompt, y_sample, new_diff_k, new_diff_v, new_gqa_k, new_gqa_v)
```

```python
import functools
import math

import numpy as np

import jax
import jax.numpy as jnp
from jax import lax
from jax.experimental import pallas as pl
from jax.experimental.pallas import tpu as pltpu

F32 = jnp.float32
BF16 = jnp.bfloat16

D_MODEL = 1024
BATCH = 16
SEQ = 256
DEPTH = 2
DEC_BATCH = 2
DEC_SEQ = 2048
PAST_LEN = 256
GRID_W = 64
GRID_H = DEC_SEQ // GRID_W
ROPE_THETA = 10000.0
EPS = 1e-6
N_MOD = 9
D_FF = 2816
DK_DIFF = 32
DV_DIFF = 64
HD_GQA = 64
CCM_K = 31
SCONV_K = 3
MIX_IN = 2560

N_CTX = BATCH * SEQ
N_LAT = DEC_BATCH * DEC_SEQ
N_TOK = N_CTX + N_LAT
N_COND = 8
S_LAT = DEC_SEQ + PAST_LEN

VMEM_LIMIT = 56 * 1024 * 1024

TM_FFN = 1024
TF_FFN = 256
TM_MIX = 512
TQ_LAT = 256
T_CONV = 2048
R_CONV = 64
CONV_HALO = 16
TM_OUT = 1024


def _cond_of_tile(i, tm):
    n_ctx_tiles = N_CTX // tm
    per_b = DEC_SEQ // tm
    return jnp.where(i < n_ctx_tiles, 0, 1 + (i - n_ctx_tiles) // per_b)


def _mod_row(mod_ref, k, cond):
    return mod_ref[k, pl.ds(cond, 1), :]


def _dot(a, b):
    return jnp.dot(a, b, preferred_element_type=F32)


def _dot_nt(a, b):
    return lax.dot_general(a, b, (((1,), (1,)), ((), ())), preferred_element_type=F32)


def _rms(x, g):
    ms = jnp.mean(x * x, axis=-1, keepdims=True)
    return x * lax.rsqrt(ms + EPS) * g


def _group_mean(xx, width, group):
    r = lax.broadcasted_iota(jnp.int32, (width, width), 0) // group
    c = lax.broadcasted_iota(jnp.int32, (width, width), 1) // group
    gmat = jnp.where(r == c, 1.0 / group, 0.0).astype(BF16)
    hi = xx.astype(BF16)
    lo = (xx - hi.astype(F32)).astype(BF16)
    return _dot(hi, gmat) + _dot(lo, gmat)


def _const_spec(shape):
    return pl.BlockSpec(shape, lambda *_: (0,) * len(shape))


def _layer_spec(shape, l, **kw):
    return pl.BlockSpec((None,) + shape, lambda *_: (l,) + (0,) * len(shape), **kw)


_MOD_SHAPE = (N_MOD, N_COND, D_MODEL)


def _ada_kernel(c_ref, w_ref, b_ref, o_ref):
    cnd = c_ref[...]
    s = (cnd * jax.nn.sigmoid(cnd)).astype(BF16)
    o_ref[...] = _dot(s, w_ref[...].astype(BF16)) + b_ref[...]


def _ada(conds, w_ada, b_ada):
    return pl.pallas_call(
        _ada_kernel,
        grid=(DEPTH, N_MOD),
        in_specs=[
            pl.BlockSpec((N_COND, D_MODEL), lambda l, n: (0, 0)),
            pl.BlockSpec((None, D_MODEL, D_MODEL), lambda l, n: (l, 0, n)),
            pl.BlockSpec((None, None, 1, D_MODEL), lambda l, n: (l, n, 0, 0)),
        ],
        out_specs=pl.BlockSpec((None, None, N_COND, D_MODEL), lambda l, n: (l, n, 0, 0)),
        out_shape=jax.ShapeDtypeStruct((DEPTH,) + _MOD_SHAPE, F32),
        compiler_params=pltpu.CompilerParams(
            dimension_semantics=("arbitrary", "arbitrary"), vmem_limit_bytes=VMEM_LIMIT),
        name="ada",
    )(conds, w_ada, b_ada.reshape(DEPTH, N_MOD, 1, D_MODEL))


def _ffn_kernel(sub, n_x, n_o, *refs):
    x_refs = refs[:n_x]
    mod_ref, gpre_ref, gpost_ref, wg_ref, wu_ref, wd_ref = refs[n_x:n_x + 6]
    o_refs = refs[n_x + 6:n_x + 6 + n_o]
    h_ref, acc_ref = refs[n_x + 6 + n_o:]
    i = pl.program_id(0)
    j = pl.program_id(1)
    is_ctx = i < N_CTX // TM_FFN
    cond = _cond_of_tile(i, TM_FFN)

    def load_x():
        if n_x == 1:
            return x_refs[0][...]
        return jnp.where(is_ctx, x_refs[0][...], x_refs[1][...])

    @pl.when(j == 0)
    def _():
        shift = _mod_row(mod_ref, 3 * sub, cond)
        scale = _mod_row(mod_ref, 3 * sub + 1, cond)
        h = _rms(load_x(), gpre_ref[pl.ds(sub, 1), :]) * (1.0 + scale) + shift
        h_ref[...] = h.astype(BF16)
        acc_ref[...] = jnp.zeros_like(acc_ref)

    h = h_ref[...]
    g = _dot(h, wg_ref[...].astype(BF16))
    u = _dot(h, wu_ref[...].astype(BF16))
    a = (g * jax.nn.sigmoid(g)) * u
    acc_ref[...] += _dot(a.astype(BF16), wd_ref[...].astype(BF16))

    @pl.when(j == pl.num_programs(1) - 1)
    def _():
        gate = _mod_row(mod_ref, 3 * sub + 2, cond)
        y = load_x() + 0.5 * (gate * _rms(acc_ref[...], gpost_ref[pl.ds(sub, 1), :]))
        if n_o == 1:
            o_refs[0][...] = y
        else:
            @pl.when(is_ctx)
            def _():
                o_refs[0][...] = y

            @pl.when(jnp.logical_not(is_ctx))
            def _():
                o_refs[1][...] = y


def _ffn(xs, mod, norm_pre, norm_post, wg, wu, wd, l, sub, split_out=False):
    tm, tf = TM_FFN, TF_FFN
    n_ctx_tiles = N_CTX // tm
    ctx_spec = pl.BlockSpec((tm, D_MODEL), lambda i, j: (jnp.minimum(i, n_ctx_tiles - 1), 0))
    lat_spec = pl.BlockSpec((tm, D_MODEL), lambda i, j: (jnp.maximum(i - n_ctx_tiles, 0), 0))
    full_spec = pl.BlockSpec((tm, D_MODEL), lambda i, j: (i, 0))
    x_specs = [full_spec] if len(xs) == 1 else [ctx_spec, lat_spec]
    if split_out:
        out_specs = [ctx_spec, lat_spec]
        out_shape = [jax.ShapeDtypeStruct((N_CTX, D_MODEL), F32),
                     jax.ShapeDtypeStruct((N_LAT, D_MODEL), F32)]
    else:
        out_specs = [full_spec]
        out_shape = [jax.ShapeDtypeStruct((N_TOK, D_MODEL), F32)]
    return pl.pallas_call(
        functools.partial(_ffn_kernel, sub, len(xs), len(out_specs)),
        grid=(N_TOK // tm, D_FF // tf),
        in_specs=x_specs + [
            _layer_spec(_MOD_SHAPE, l),
            _layer_spec((3, D_MODEL), l),
            _layer_spec((3, D_MODEL), l),
            pl.BlockSpec((None, D_MODEL, tf), lambda i, j: (l, 0, j)),
            pl.BlockSpec((None, D_MODEL, tf), lambda i, j: (l, 0, j)),
            pl.BlockSpec((None, tf, D_MODEL), lambda i, j: (l, j, 0)),
        ],
        out_specs=out_specs,
        out_shape=out_shape,
        scratch_shapes=[pltpu.VMEM((tm, D_MODEL), BF16), pltpu.VMEM((tm, D_MODEL), F32)],
        compiler_params=pltpu.CompilerParams(
            dimension_semantics=("arbitrary", "arbitrary"), vmem_limit_bytes=VMEM_LIMIT),
        name=f"ffn{sub}",
    )(*xs, mod, norm_pre, norm_post, wg, wu, wd)


def _swap_blocks(x, blk):
    w = x.shape[-1]
    lane = lax.broadcasted_iota(jnp.int32, x.shape, x.ndim - 1)
    first = (lane % (2 * blk)) < blk
    return jnp.where(first, pltpu.roll(x, w - blk, x.ndim - 1), pltpu.roll(x, blk, x.ndim - 1))


def _rope_tables():
    out = []
    for rot_dim in (DK_DIFF, HD_GQA):
        n_freq = rot_dim // 4
        inv = ROPE_THETA ** (-np.arange(n_freq, dtype=np.float64) / n_freq)
        lane = np.arange(256) % rot_dim
        pos = lane % (rot_dim // 2)
        freq = inv[pos % n_freq]
        sign = np.where(pos < n_freq, -1.0, 1.0)
        coord = np.concatenate([np.arange(GRID_H), np.arange(GRID_W)]).astype(np.float64)
        ang = coord[:, None] * freq[None, :]
        out.append(np.cos(ang))
        out.append(np.sin(ang) * sign[None, :])
    return np.stack(out).astype(np.float32)


def _tile_rope(tab_ref, which, rot_dim, first_grid_row, n_grid_rows, identity):
    col_part = tab_ref[which, GRID_H:GRID_H + GRID_W, :]
    lane = lax.broadcasted_iota(jnp.int32, (1, 256), 1)
    on_row_axis = (lane % rot_dim) < rot_dim // 2
    tab = jnp.concatenate(
        [jnp.where(on_row_axis, tab_ref[which, pl.ds(first_grid_row + r, 1), :], col_part)
         for r in range(n_grid_rows)], axis=0)
    return jnp.where(identity, 1.0 - (which % 2), tab)


def _mixproj_kernel(l, has_prev, x_ref, mod_ref, gpre_ref, w_ref, qg_ref, kg_ref, tab_ref, *rest):
    n_alias = 5 if has_prev else 1
    (cin_ref, q_ref, kvc_ref, kvl_ref, odk_ref, odv_ref, ogk_ref, ogv_ref,
     wbf_ref) = rest[n_alias:]
    tm = TM_MIX
    i = pl.program_id(0)
    n_ctx_tiles = N_CTX // tm
    is_ctx = i < n_ctx_tiles
    cond = _cond_of_tile(i, tm)

    @pl.when(i == 0)
    def _():
        wbf_ref[...] = w_ref[...].astype(BF16)

    shift = _mod_row(mod_ref, 3, cond)
    scale = _mod_row(mod_ref, 4, cond)
    h = (_rms(x_ref[...], gpre_ref[pl.ds(1, 1), :]) * (1.0 + scale) + shift).astype(BF16)

    a = _dot(h, wbf_ref[:, 0:768])
    cin_ref[:, 0:256] = a[:, 0:256]
    cin_ref[:, 256:512] = a[:, 256:512] * a[:, 512:768]
    c = _dot(h, wbf_ref[:, 1536:2048])
    cin_ref[:, 512:768] = c[:, 0:256] * jax.nn.sigmoid(c[:, 256:512])

    grid_rows = tm // GRID_W
    first_row = (jnp.maximum(i - n_ctx_tiles, 0) % (DEC_SEQ // tm)) * grid_rows
    cd = _tile_rope(tab_ref, 0, DK_DIFF, first_row, grid_rows, is_ctx)
    sd = _tile_rope(tab_ref, 1, DK_DIFF, first_row, grid_rows, is_ctx)
    cg = _tile_rope(tab_ref, 2, HD_GQA, first_row, grid_rows, is_ctx)
    sg = _tile_rope(tab_ref, 3, HD_GQA, first_row, grid_rows, is_ctx)

    b = _dot(h, wbf_ref[:, 768:1536])
    bq, bk, bv = b[:, 0:256], b[:, 256:512], b[:, 512:768]
    bq_r = (bq * cd + _swap_blocks(bq, DK_DIFF // 4) * sd) * (DK_DIFF ** -0.5)
    bk_r = bk * cd + _swap_blocks(bk, DK_DIFF // 4) * sd
    q_ref[:, 0:256] = bq_r.astype(BF16)

    d = _dot(h, wbf_ref[:, 2048:2560])
    dq, dk, dv = d[:, 0:256], d[:, 256:384], d[:, 384:512]
    dq_n = dq * lax.rsqrt(_group_mean(dq * dq, 256, HD_GQA) + EPS) * qg_ref[l:l + 1, :]
    dk_n = dk * lax.rsqrt(_group_mean(dk * dk, 128, HD_GQA) + EPS) * kg_ref[l:l + 1, :]
    dq_r = (dq_n * cg + _swap_blocks(dq_n, HD_GQA // 4) * sg) * (HD_GQA ** -0.5)
    dk_r = dk_n * cg[:, 0:128] + _swap_blocks(dk_n, HD_GQA // 4) * sg[:, 0:128]
    q_ref[:, 256:512] = dq_r.astype(BF16)

    pieces = ((0, bk_r), (256, bv), (512, dk_r), (640, dv),
              (768, pltpu.roll(dk_r, HD_GQA, 1)), (896, pltpu.roll(dv, HD_GQA, 1)))

    @pl.when(is_ctx)
    def _():
        for off, val in pieces:
            kvc_ref[:, off:off + val.shape[1]] = val.astype(BF16)
        seqs = tm // SEQ
        odk_ref[...] = bk.reshape(seqs, SEQ, 256)
        odv_ref[...] = bv.reshape(seqs, SEQ, 256)
        ogk_ref[...] = dk_n.reshape(seqs, SEQ, 128)
        ogv_ref[...] = dv.reshape(seqs, SEQ, 128)

    @pl.when(jnp.logical_not(is_ctx))
    def _():
        for off, val in pieces:
            kvl_ref[:, off:off + val.shape[1]] = val.astype(BF16)


def _mixproj(x, mod, norm_pre, w_mix_in, qg, kg, tables, kv_lat_base, prev_own, l):
    tm = TM_MIX
    n_ctx_tiles = N_CTX // tm
    per_b = DEC_SEQ // tm
    seqs = tm // SEQ

    def lat_map(i):
        t = jnp.maximum(i - n_ctx_tiles, 0)
        return (t // per_b, t % per_b, 0)

    ctx_row = lambda i: jnp.minimum(i, n_ctx_tiles - 1)
    row = lambda w: pl.BlockSpec((tm, w), lambda i: (i, 0))
    own_spec = lambda w: pl.BlockSpec((seqs, None, SEQ, w), lambda i: (ctx_row(i), l, 0, 0))
    own_shape = lambda w: jax.ShapeDtypeStruct((BATCH, DEPTH, SEQ, w), F32)
    any_spec = pl.BlockSpec(memory_space=pl.ANY)
    aliased = [kv_lat_base] + (list(prev_own) if prev_own is not None else [])
    n_in = 7
    aliases = {n_in: 3}
    if prev_own is not None:
        aliases.update({n_in + 1 + k: 4 + k for k in range(4)})
    return pl.pallas_call(
        functools.partial(_mixproj_kernel, l, prev_own is not None),
        grid=(N_TOK // tm,),
        in_specs=[
            row(D_MODEL),
            _layer_spec(_MOD_SHAPE, l),
            _layer_spec((3, D_MODEL), l),
            _layer_spec((D_MODEL, MIX_IN), l, pipeline_mode=pl.Buffered(1)),
            _const_spec((DEPTH, 256)),
            _const_spec((DEPTH, 128)),
            _const_spec(tables.shape),
        ] + [any_spec] * len(aliased),
        out_specs=[row(768), row(512),
                   pl.BlockSpec((tm, 1024), lambda i: (ctx_row(i), 0)),
                   pl.BlockSpec((None, tm, 1024), lat_map),
                   own_spec(256), own_spec(256), own_spec(128), own_spec(128)],
        out_shape=[
            jax.ShapeDtypeStruct((N_TOK, 768), F32),
            jax.ShapeDtypeStruct((N_TOK, 512), BF16),
            jax.ShapeDtypeStruct((N_CTX, 1024), BF16),
            jax.ShapeDtypeStruct((DEC_BATCH, S_LAT, 1024), BF16),
            own_shape(256), own_shape(256), own_shape(128), own_shape(128),
        ],
        scratch_shapes=[pltpu.VMEM((D_MODEL, MIX_IN), BF16)],
        input_output_aliases=aliases,
        compiler_params=pltpu.CompilerParams(
            dimension_semantics=("arbitrary",), vmem_limit_bytes=VMEM_LIMIT),
        name="mixproj",
    )(x, mod, norm_pre, w_mix_in, qg, kg, tables, *aliased)


def _softmax_parts(s):
    e = jnp.exp(s - jnp.max(s, axis=-1, keepdims=True))
    return e, 1.0 / jnp.sum(e, axis=-1, keepdims=True)


def _attn_kernel(l, q_ref, kv_ref, lq1_ref, lk1_ref, lq2_ref, lk2_ref, subln_ref, *rest):
    o_ref = rest[-1]
    tq = q_ref.shape[0]
    lam_init = 0.8 - 0.6 * math.exp(-0.3 * l)
    row = slice(l, l + 1)
    lam = (jnp.exp(jnp.sum(lq1_ref[row, :] * lk1_ref[row, :], axis=-1, keepdims=True))
           - jnp.exp(jnp.sum(lq2_ref[row, :] * lk2_ref[row, :], axis=-1, keepdims=True))
           + lam_init)

    qb = q_ref[:, 0:256]
    kb = kv_ref[:, 0:256]
    vb = kv_ref[:, 256:512]
    lane = lax.broadcasted_iota(jnp.int32, (1, 256), 1)
    zero = jnp.zeros((), BF16)
    acc_b = jnp.zeros((tq, 256), F32)
    for hd in range(4):
        ps = []
        for m in range(2):
            qm = jnp.where(lane // DK_DIFF == 2 * hd + m, qb, zero)
            e, inv = _softmax_parts(_dot_nt(qm, kb))
            ps.append(e * inv)
        w = ps[0] - lam * ps[1]
        o = _dot(w.astype(BF16), vb)
        acc_b = acc_b + jnp.where(lane // DV_DIFF == hd, o, 0.0)
    nrm = lax.rsqrt(_group_mean(acc_b * acc_b, 256, DV_DIFF) + EPS)
    o_ref[:, 0:256] = (acc_b * nrm * subln_ref[row, :] * (1.0 - lam_init)).astype(BF16)

    lane128 = lax.broadcasted_iota(jnp.int32, (1, 128), 1)
    for half in range(2):
        acc = jnp.zeros((tq, 128), F32)
        qd = q_ref[:, 256 + 128 * half:384 + 128 * half]
        for blk in range(2):
            hd = 2 * half + blk
            base = 512 if hd in (0, 3) else 768
            kd = kv_ref[:, base:base + 128]
            vd = kv_ref[:, base + 128:base + 256]
            qm = jnp.where(lane128 // HD_GQA == blk, qd, zero)
            e, inv = _softmax_parts(_dot_nt(qm, kd))
            o = _dot(e.astype(BF16), vd) * inv
            acc = acc + jnp.where(lane128 // HD_GQA == blk, o, 0.0)
        o_ref[:, 256 + 128 * half:384 + 128 * half] = acc.astype(BF16)


def _attn_param_specs():
    vec = _const_spec((DEPTH, DK_DIFF))
    return [vec, vec, vec, vec, _const_spec((DEPTH, 256))]


def _attn_ctx(q_arr, kv_ctx, aparams, l):
    return pl.pallas_call(
        functools.partial(_attn_kernel, l),
        grid=(BATCH,),
        in_specs=[pl.BlockSpec((SEQ, 512), lambda b: (b, 0)),
                  pl.BlockSpec((SEQ, 1024), lambda b: (b, 0))] + _attn_param_specs(),
        out_specs=pl.BlockSpec((SEQ, 512), lambda b: (b, 0)),
        out_shape=jax.ShapeDtypeStruct((N_TOK, 512), BF16),
        compiler_params=pltpu.CompilerParams(
            dimension_semantics=("arbitrary",), vmem_limit_bytes=VMEM_LIMIT),
        name="attn_ctx",
    )(q_arr, kv_ctx, *aparams)


def _attn_lat(q_arr, kv_lat, att, aparams, l):
    tq = TQ_LAT
    nq = DEC_SEQ // tq
    row_map = lambda b, i: (N_CTX // tq + b * nq + i, 0)
    return pl.pallas_call(
        functools.partial(_attn_kernel, l),
        grid=(DEC_BATCH, nq),
        in_specs=[pl.BlockSpec((tq, 512), row_map),
                  pl.BlockSpec((None, S_LAT, 1024), lambda b, i: (b, 0, 0))]
        + _attn_param_specs() + [pl.BlockSpec(memory_space=pl.ANY)],
        out_specs=pl.BlockSpec((tq, 512), row_map),
        out_shape=jax.ShapeDtypeStruct((N_TOK, 512), BF16),
        input_output_aliases={7: 0},
        compiler_params=pltpu.CompilerParams(
            dimension_semantics=("arbitrary", "arbitrary"), vmem_limit_bytes=VMEM_LIMIT),
        name="attn_lat",
    )(q_arr, kv_lat, *aparams, att)


def _conv_kernel(l, seq_len, cin_ref, wsc_ref, wdw_ref, bdw_ref, lng_ref, lnb_ref, pw_ref, *rest):
    o_ref, pad_ref = rest[-2], rest[-1]
    t_tile = cin_ref.shape[0]
    n_seq = t_tile // seq_len
    stride = seq_len + CONV_HALO
    chunks_per_seq = seq_len // R_CONV
    rows = R_CONV

    for q in range(n_seq + 1):
        pad_ref[q * stride:q * stride + CONV_HALO, :] = jnp.zeros((CONV_HALO, 512), F32)
    for q in range(n_seq):
        pad_ref[CONV_HALO + q * stride:CONV_HALO + q * stride + seq_len, :] = (
            cin_ref[q * seq_len:(q + 1) * seq_len, 256:768])

    pw = pw_ref[...].astype(BF16)
    bdw = bdw_ref[l:l + 1, :]
    lng = lng_ref[l:l + 1, :]
    lnb = lnb_ref[l:l + 1, :]

    def body(c, carry):
        q = c // chunks_per_seq
        r = c % chunks_per_seq
        base = pl.multiple_of(q * stride + r * rows, 8)
        row0 = pl.multiple_of(c * rows, 8)
        win = pad_ref[pl.ds(base, rows + 2 * CONV_HALO), :]
        win_s = win[:, 0:256]
        win_u = win[:, 256:512]

        s_acc = None
        for k in range(SCONV_K):
            off = CONV_HALO + k - SCONV_K // 2
            term = wsc_ref[pl.ds(k, 1), :] * win_s[off:off + rows]
            s_acc = term if s_acc is None else s_acc + term
        a_out = cin_ref[pl.ds(row0, rows), 0:256] * s_acc

        acc = None
        for res in range(8):
            part = None
            for a in range(-2, 2):
                k = 8 * a + res + CCM_K // 2
                if 0 <= k < CCM_K:
                    lo = CONV_HALO + 8 * a
                    term = wdw_ref[pl.ds(k, 1), :] * win_u[lo:lo + rows + 8]
                    part = term if part is None else part + term
            part = part[res:res + rows]
            acc = part if acc is None else acc + part
        u = acc + bdw
        mu = jnp.mean(u, axis=-1, keepdims=True)
        var = jnp.mean(jnp.square(u - mu), axis=-1, keepdims=True)
        y = (u - mu) * lax.rsqrt(var + EPS) * lng + lnb
        y = y * jax.nn.sigmoid(y)
        c_out = _dot(y.astype(BF16), pw)
        o_ref[pl.ds(row0, rows), 0:256] = a_out.astype(BF16)
        o_ref[pl.ds(row0, rows), 256:512] = c_out.astype(BF16)
        return carry

    lax.fori_loop(0, t_tile // rows, body, 0)


def _conv(cin, prev, seq_len, tile0, n_tiles, cparams, l):
    t = T_CONV
    n_seq = t // seq_len
    vec = _const_spec((DEPTH, 256))
    in_specs = [pl.BlockSpec((t, 768), lambda i: (tile0 + i, 0)),
                _layer_spec((SCONV_K, 256), l), _layer_spec((CCM_K, 256), l), vec, vec, vec,
                _layer_spec((256, 256), l)]
    args = [cin, *cparams]
    aliases = {}
    if prev is not None:
        in_specs.append(pl.BlockSpec(memory_space=pl.ANY))
        args.append(prev)
        aliases = {len(args) - 1: 0}
    return pl.pallas_call(
        functools.partial(_conv_kernel, l, seq_len),
        grid=(n_tiles,),
        in_specs=in_specs,
        out_specs=pl.BlockSpec((t, 512), lambda i: (tile0 + i, 0)),
        out_shape=jax.ShapeDtypeStruct((N_TOK, 512), BF16),
        scratch_shapes=[pltpu.VMEM((n_seq * (seq_len + CONV_HALO) + CONV_HALO, 512), F32)],
        input_output_aliases=aliases,
        compiler_params=pltpu.CompilerParams(
            dimension_semantics=("arbitrary",), vmem_limit_bytes=VMEM_LIMIT),
        name=f"conv{seq_len}",
    )(*args)


def _outproj_kernel(ac_ref, att_ref, x_ref, mod_ref, gpost_ref, w_ref, o_ref, wbf_ref):
    i = pl.program_id(0)

    @pl.when(i == 0)
    def _():
        wbf_ref[...] = w_ref[...].astype(BF16)

    y = (_dot(ac_ref[:, 0:256], wbf_ref[0:256, :]) + _dot(att_ref[:, 0:256], wbf_ref[256:512, :])
         + _dot(ac_ref[:, 256:512], wbf_ref[512:768, :])
         + _dot(att_ref[:, 256:512], wbf_ref[768:1024, :]))
    gate = _mod_row(mod_ref, 5, _cond_of_tile(i, TM_OUT))
    o_ref[...] = x_ref[...] + gate * _rms(y, gpost_ref[pl.ds(1, 1), :])


def _outproj(ac, att, x, mod, norm_post, w_mix_out, l):
    tm = TM_OUT
    row = lambda w: pl.BlockSpec((tm, w), lambda i: (i, 0))
    return pl.pallas_call(
        _outproj_kernel,
        grid=(N_TOK // tm,),
        in_specs=[row(512), row(512), row(D_MODEL),
                  _layer_spec(_MOD_SHAPE, l),
                  _layer_spec((3, D_MODEL), l),
                  _layer_spec((D_MODEL, D_MODEL), l, pipeline_mode=pl.Buffered(1))],
        out_specs=row(D_MODEL),
        out_shape=jax.ShapeDtypeStruct((N_TOK, D_MODEL), F32),
        scratch_shapes=[pltpu.VMEM((D_MODEL, D_MODEL), BF16)],
        compiler_params=pltpu.CompilerParams(
            dimension_semantics=("arbitrary",), vmem_limit_bytes=VMEM_LIMIT),
        name="outproj",
    )(ac, att, x, mod, norm_post, w_mix_out)


def _swap_halves(a):
    return jnp.concatenate([a[..., HD_GQA:], a[..., :HD_GQA]], axis=-1)


def _kv_lat_base(cache_diff_k, cache_diff_v, cache_gqa_k, cache_gqa_v, l):
    cdk = cache_gqa_k[:, l].reshape(DEC_BATCH, PAST_LEN, 128)
    cdv = cache_gqa_v[:, l].reshape(DEC_BATCH, PAST_LEN, 128)
    cache_kv = jnp.concatenate(
        [cache_diff_k[:, l].reshape(DEC_BATCH, PAST_LEN, 256),
         cache_diff_v[:, l].reshape(DEC_BATCH, PAST_LEN, 256),
         cdk, cdv, _swap_halves(cdk), _swap_halves(cdv)], axis=-1).astype(BF16)
    return jnp.pad(cache_kv, ((0, 0), (DEC_SEQ, 0), (0, 0)))


def kernel(x_prompt, x_sample, cache_diff_k, cache_diff_v, cache_gqa_k, cache_gqa_v, c, c_ctx, w_ada, b_ada, norm_pre, norm_post, ffn1_gate, ffn1_up, ffn1_down, ffn2_gate, ffn2_up, ffn2_down, w_mix_in, w_mix_out, sconv_w, diff_lq1, diff_lk1, diff_lq2, diff_lk2, diff_subln, ccm_dw_w, ccm_dw_b, ccm_ln_g, ccm_ln_b, ccm_pw, gqa_qnorm, gqa_knorm):
    conds = jnp.concatenate(
        [c_ctx[None, :], c, jnp.zeros((N_COND - 1 - DEC_BATCH, D_MODEL), F32)], axis=0)
    mod = _ada(conds, w_ada, b_ada)
    tables = jnp.asarray(_rope_tables())
    qg = jnp.tile(gqa_qnorm, (1, 4))
    kg = jnp.tile(gqa_knorm, (1, 2))
    aparams = (diff_lq1, diff_lk1, diff_lq2, diff_lk2, jnp.tile(diff_subln, (1, 4)))
    cparams = (sconv_w, ccm_dw_w, ccm_dw_b, ccm_ln_g, ccm_ln_b, ccm_pw)

    xs = (x_prompt.reshape(N_CTX, D_MODEL), x_sample.reshape(N_LAT, D_MODEL))
    own = None
    for l in range(DEPTH):
        (x,) = _ffn(xs, mod, norm_pre, norm_post, ffn1_gate, ffn1_up, ffn1_down, l, 0)

        base = _kv_lat_base(cache_diff_k, cache_diff_v, cache_gqa_k, cache_gqa_v, l)
        cin, q_arr, kv_ctx, kv_lat, *own = _mixproj(
            x, mod, norm_pre, w_mix_in, qg, kg, tables, base, own, l)

        att = _attn_ctx(q_arr, kv_ctx, aparams, l)
        att = _attn_lat(q_arr, kv_lat, att, aparams, l)

        ac = _conv(cin, None, SEQ, 0, N_CTX // T_CONV, cparams, l)
        ac = _conv(cin, ac, DEC_SEQ, N_CTX // T_CONV, N_LAT // T_CONV, cparams, l)

        x = _outproj(ac, att, x, mod, norm_post, w_mix_out, l)
        xs = _ffn((x,), mod, norm_pre, norm_post, ffn2_gate, ffn2_up, ffn2_down, l, 2,
                  split_out=(l == DEPTH - 1))

    y_prompt, y_sample = xs
    odk, odv, ogk, ogv = own
    return (y_prompt.reshape(BATCH, SEQ, D_MODEL),
            y_sample.reshape(DEC_BATCH, DEC_SEQ, D_MODEL),
            odk.reshape(BATCH, DEPTH, SEQ, 4, 2, DK_DIFF),
            odv.reshape(BATCH, DEPTH, SEQ, 4, DV_DIFF),
            ogk.reshape(BATCH, DEPTH, SEQ, 2, HD_GQA),
            ogv.reshape(BATCH, DEPTH, SEQ, 2, HD_GQA))
```

```python
import functools
import math

import numpy as np

import jax
import jax.numpy as jnp
from jax import lax
from jax.experimental import pallas as pl
from jax.experimental.pallas import tpu as pltpu

F32 = jnp.float32
BF16 = jnp.bfloat16

D_MODEL = 1024
BATCH = 16
SEQ = 256
DEPTH = 2
DEC_BATCH = 2
DEC_SEQ = 2048
PAST_LEN = 256
GRID_W = 64
GRID_H = DEC_SEQ // GRID_W
ROPE_THETA = 10000.0
EPS = 1e-6
N_MOD = 9
D_FF = 2816
DK_DIFF = 32
DV_DIFF = 64
HD_GQA = 64
CCM_K = 31
SCONV_K = 3
MIX_IN = 2560

N_CTX = BATCH * SEQ
N_LAT = DEC_BATCH * DEC_SEQ
N_TOK = N_CTX + N_LAT
N_COND = 8
S_LAT = DEC_SEQ + PAST_LEN

LOG2E = math.log2(math.e)

VMEM_LIMIT = 56 * 1024 * 1024

TM_FFN = 1024
TF_FFN = 256
TM_MIX = 512
TQ_LAT = 256
T_CONV = 2048
R_CONV = 64
CONV_HALO = 16
TM_OUT = 1024


def _cond_of_tile(i, tm):
    n_ctx_tiles = N_CTX // tm
    per_b = DEC_SEQ // tm
    return jnp.where(i < n_ctx_tiles, 0, 1 + (i - n_ctx_tiles) // per_b)


def _mod_row(mod_ref, k, cond):
    return mod_ref[k, pl.ds(cond, 1), :]


def _dot(a, b):
    return jnp.dot(a, b, preferred_element_type=F32)


def _dot_nt(a, b):
    return lax.dot_general(a, b, (((1,), (1,)), ((), ())), preferred_element_type=F32)


def _rms(x, g):
    ms = jnp.mean(x * x, axis=-1, keepdims=True)
    return x * lax.rsqrt(ms + EPS) * g


def _group_mean(xx, width, group):
    r = lax.broadcasted_iota(jnp.int32, (width, width), 0) // group
    c = lax.broadcasted_iota(jnp.int32, (width, width), 1) // group
    gmat = jnp.where(r == c, 1.0 / group, 0.0).astype(BF16)
    hi = xx.astype(BF16)
    lo = (xx - hi.astype(F32)).astype(BF16)
    return _dot(hi, gmat) + _dot(lo, gmat)


def _const_spec(shape):
    return pl.BlockSpec(shape, lambda *_: (0,) * len(shape))


def _layer_spec(shape, l, **kw):
    return pl.BlockSpec((None,) + shape, lambda *_: (l,) + (0,) * len(shape), **kw)


_MOD_SHAPE = (N_MOD, N_COND, D_MODEL)


def _ada_kernel(c_ref, w_ref, b_ref, o_ref):
    cnd = c_ref[...]
    s = (cnd * jax.nn.sigmoid(cnd)).astype(BF16)
    o_ref[...] = _dot(s, w_ref[...].astype(BF16)) + b_ref[...]


def _ada(conds, w_ada, b_ada):
    return pl.pallas_call(
        _ada_kernel,
        grid=(DEPTH, N_MOD),
        in_specs=[
            pl.BlockSpec((N_COND, D_MODEL), lambda l, n: (0, 0)),
            pl.BlockSpec((None, D_MODEL, D_MODEL), lambda l, n: (l, 0, n)),
            pl.BlockSpec((None, None, 1, D_MODEL), lambda l, n: (l, n, 0, 0)),
        ],
        out_specs=pl.BlockSpec((None, None, N_COND, D_MODEL), lambda l, n: (l, n, 0, 0)),
        out_shape=jax.ShapeDtypeStruct((DEPTH,) + _MOD_SHAPE, F32),
        compiler_params=pltpu.CompilerParams(
            dimension_semantics=("arbitrary", "arbitrary"), vmem_limit_bytes=VMEM_LIMIT),
        name="ada",
    )(conds, w_ada, b_ada.reshape(DEPTH, N_MOD, 1, D_MODEL))


def _ffn_kernel(sub, n_x, n_o, *refs):
    x_refs = refs[:n_x]
    mod_ref, gpre_ref, gpost_ref, wg_ref, wu_ref, wd_ref = refs[n_x:n_x + 6]
    o_refs = refs[n_x + 6:n_x + 6 + n_o]
    h_ref, acc_ref = refs[n_x + 6 + n_o:]
    i = pl.program_id(0)
    j = pl.program_id(1)
    is_ctx = i < N_CTX // TM_FFN
    cond = _cond_of_tile(i, TM_FFN)

    def load_x():
        if n_x == 1:
            return x_refs[0][...]
        return jnp.where(is_ctx, x_refs[0][...], x_refs[1][...])

    @pl.when(j == 0)
    def _():
        shift = _mod_row(mod_ref, 3 * sub, cond)
        scale = _mod_row(mod_ref, 3 * sub + 1, cond)
        h = _rms(load_x(), gpre_ref[pl.ds(sub, 1), :]) * (1.0 + scale) + shift
        h_ref[...] = h.astype(BF16)
        acc_ref[...] = jnp.zeros_like(acc_ref)

    h = h_ref[...]
    g = _dot(h, wg_ref[...].astype(BF16))
    u = _dot(h, wu_ref[...].astype(BF16))
    a = (g * jax.nn.sigmoid(g)) * u
    acc_ref[...] += _dot(a.astype(BF16), wd_ref[...].astype(BF16))

    @pl.when(j == pl.num_programs(1) - 1)
    def _():
        gate = _mod_row(mod_ref, 3 * sub + 2, cond)
        y = load_x() + 0.5 * (gate * _rms(acc_ref[...], gpost_ref[pl.ds(sub, 1), :]))
        if n_o == 1:
            o_refs[0][...] = y
        else:
            @pl.when(is_ctx)
            def _():
                o_refs[0][...] = y

            @pl.when(jnp.logical_not(is_ctx))
            def _():
                o_refs[1][...] = y


def _ffn(xs, mod, norm_pre, norm_post, wg, wu, wd, l, sub, split_out=False):
    tm, tf = TM_FFN, TF_FFN
    n_ctx_tiles = N_CTX // tm
    ctx_spec = pl.BlockSpec((tm, D_MODEL), lambda i, j: (jnp.minimum(i, n_ctx_tiles - 1), 0))
    lat_spec = pl.BlockSpec((tm, D_MODEL), lambda i, j: (jnp.maximum(i - n_ctx_tiles, 0), 0))
    full_spec = pl.BlockSpec((tm, D_MODEL), lambda i, j: (i, 0))
    x_specs = [full_spec] if len(xs) == 1 else [ctx_spec, lat_spec]
    if split_out:
        out_specs = [ctx_spec, lat_spec]
        out_shape = [jax.ShapeDtypeStruct((N_CTX, D_MODEL), F32),
                     jax.ShapeDtypeStruct((N_LAT, D_MODEL), F32)]
    else:
        out_specs = [full_spec]
        out_shape = [jax.ShapeDtypeStruct((N_TOK, D_MODEL), F32)]
    return pl.pallas_call(
        functools.partial(_ffn_kernel, sub, len(xs), len(out_specs)),
        grid=(N_TOK // tm, D_FF // tf),
        in_specs=x_specs + [
            _layer_spec(_MOD_SHAPE, l),
            _layer_spec((3, D_MODEL), l),
            _layer_spec((3, D_MODEL), l),
            pl.BlockSpec((None, D_MODEL, tf), lambda i, j: (l, 0, j)),
            pl.BlockSpec((None, D_MODEL, tf), lambda i, j: (l, 0, j)),
            pl.BlockSpec((None, tf, D_MODEL), lambda i, j: (l, j, 0)),
        ],
        out_specs=out_specs,
        out_shape=out_shape,
        scratch_shapes=[pltpu.VMEM((tm, D_MODEL), BF16), pltpu.VMEM((tm, D_MODEL), F32)],
        compiler_params=pltpu.CompilerParams(
            dimension_semantics=("arbitrary", "arbitrary"), vmem_limit_bytes=VMEM_LIMIT),
        name=f"ffn{sub}",
    )(*xs, mod, norm_pre, norm_post, wg, wu, wd)


def _swap_blocks(x, blk):
    w = x.shape[-1]
    lane = lax.broadcasted_iota(jnp.int32, x.shape, x.ndim - 1)
    first = (lane % (2 * blk)) < blk
    return jnp.where(first, pltpu.roll(x, w - blk, x.ndim - 1), pltpu.roll(x, blk, x.ndim - 1))


def _rope_tables():
    out = []
    for rot_dim in (DK_DIFF, HD_GQA):
        n_freq = rot_dim // 4
        inv = ROPE_THETA ** (-np.arange(n_freq, dtype=np.float64) / n_freq)
        lane = np.arange(256) % rot_dim
        pos = lane % (rot_dim // 2)
        freq = inv[pos % n_freq]
        sign = np.where(pos < n_freq, -1.0, 1.0)
        coord = np.concatenate([np.arange(GRID_H), np.arange(GRID_W)]).astype(np.float64)
        ang = coord[:, None] * freq[None, :]
        out.append(np.cos(ang))
        out.append(np.sin(ang) * sign[None, :])
    return np.stack(out).astype(np.float32)


def _tile_rope(tab_ref, which, rot_dim, first_grid_row, n_grid_rows, identity):
    col_part = tab_ref[which, GRID_H:GRID_H + GRID_W, :]
    lane = lax.broadcasted_iota(jnp.int32, (1, 256), 1)
    on_row_axis = (lane % rot_dim) < rot_dim // 2
    tab = jnp.concatenate(
        [jnp.where(on_row_axis, tab_ref[which, pl.ds(first_grid_row + r, 1), :], col_part)
         for r in range(n_grid_rows)], axis=0)
    return jnp.where(identity, 1.0 - (which % 2), tab)


def _mixproj_kernel(l, has_prev, x_ref, mod_ref, gpre_ref, w_ref, qg_ref, kg_ref, tab_ref, *rest):
    n_alias = 6 if has_prev else 2
    (cin_ref, qt_ref, kc_ref, kl_ref, vtc_ref, vtl_ref, odk_ref, odv_ref, ogk_ref, ogv_ref,
     wbf_ref) = rest[n_alias:]
    tm = TM_MIX
    i = pl.program_id(0)
    n_ctx_tiles = N_CTX // tm
    is_ctx = i < n_ctx_tiles
    cond = _cond_of_tile(i, tm)

    @pl.when(i == 0)
    def _():
        wbf_ref[...] = w_ref[...].astype(BF16)

    shift = _mod_row(mod_ref, 3, cond)
    scale = _mod_row(mod_ref, 4, cond)
    h = (_rms(x_ref[...], gpre_ref[pl.ds(1, 1), :]) * (1.0 + scale) + shift).astype(BF16)

    a = _dot(h, wbf_ref[:, 0:768])
    cin_ref[:, 0:256] = a[:, 0:256]
    cin_ref[:, 256:512] = a[:, 256:512] * a[:, 512:768]
    c = _dot(h, wbf_ref[:, 1536:2048])
    cin_ref[:, 512:768] = c[:, 0:256] * jax.nn.sigmoid(c[:, 256:512])

    grid_rows = tm // GRID_W
    first_row = (jnp.maximum(i - n_ctx_tiles, 0) % (DEC_SEQ // tm)) * grid_rows
    cd = _tile_rope(tab_ref, 0, DK_DIFF, first_row, grid_rows, is_ctx)
    sd = _tile_rope(tab_ref, 1, DK_DIFF, first_row, grid_rows, is_ctx)
    cg = _tile_rope(tab_ref, 2, HD_GQA, first_row, grid_rows, is_ctx)
    sg = _tile_rope(tab_ref, 3, HD_GQA, first_row, grid_rows, is_ctx)

    b = _dot(h, wbf_ref[:, 768:1536])
    bq, bk, bv = b[:, 0:256], b[:, 256:512], b[:, 512:768]
    bq_r = (bq * cd + _swap_blocks(bq, DK_DIFF // 4) * sd) * (DK_DIFF ** -0.5 * LOG2E)
    bk_r = bk * cd + _swap_blocks(bk, DK_DIFF // 4) * sd
    qt_ref[0:256, :] = bq_r.T.astype(BF16)

    d = _dot(h, wbf_ref[:, 2048:2560])
    dq, dk, dv = d[:, 0:256], d[:, 256:384], d[:, 384:512]
    dq_n = dq * lax.rsqrt(_group_mean(dq * dq, 256, HD_GQA) + EPS) * qg_ref[l:l + 1, :]
    dk_n = dk * lax.rsqrt(_group_mean(dk * dk, 128, HD_GQA) + EPS) * kg_ref[l:l + 1, :]
    dq_r = (dq_n * cg + _swap_blocks(dq_n, HD_GQA // 4) * sg) * (HD_GQA ** -0.5 * LOG2E)
    dk_r = dk_n * cg[:, 0:128] + _swap_blocks(dk_n, HD_GQA // 4) * sg[:, 0:128]
    qt_ref[256:512, :] = dq_r.T.astype(BF16)

    bk_b = bk_r.astype(BF16)
    dk_b = jnp.concatenate([dk_r, jnp.zeros_like(dk_r)], axis=1).astype(BF16)
    bv_t = bv.T.astype(BF16)
    dv_t = jnp.concatenate([dv, jnp.zeros_like(dv)], axis=1).T.astype(BF16)

    @pl.when(is_ctx)
    def _():
        kc_ref[0] = bk_b
        kc_ref[1] = dk_b
        vtc_ref[0] = bv_t
        vtc_ref[1] = dv_t
        seqs = tm // SEQ
        odk_ref[...] = bk.reshape(seqs, SEQ, 256)
        odv_ref[...] = bv.reshape(seqs, SEQ, 256)
        ogk_ref[...] = dk_n.reshape(seqs, SEQ, 128)
        ogv_ref[...] = dv.reshape(seqs, SEQ, 128)

    @pl.when(jnp.logical_not(is_ctx))
    def _():
        kl_ref[0] = bk_b
        kl_ref[1] = dk_b
        vtl_ref[0] = bv_t
        vtl_ref[1] = dv_t


def _mixproj(x, mod, norm_pre, w_mix_in, qg, kg, tables, lat_base, prev_own, l):
    tm = TM_MIX
    n_ctx_tiles = N_CTX // tm
    per_b = DEC_SEQ // tm
    seqs = tm // SEQ

    def lat_tile(i):
        t = jnp.maximum(i - n_ctx_tiles, 0)
        return t // per_b, t % per_b

    ctx_row = lambda i: jnp.minimum(i, n_ctx_tiles - 1)
    row = lambda w: pl.BlockSpec((tm, w), lambda i: (i, 0))
    own_spec = lambda w: pl.BlockSpec((seqs, None, SEQ, w), lambda i: (ctx_row(i), l, 0, 0))
    own_shape = lambda w: jax.ShapeDtypeStruct((BATCH, DEPTH, SEQ, w), F32)
    any_spec = pl.BlockSpec(memory_space=pl.ANY)
    aliased = list(lat_base) + (list(prev_own) if prev_own is not None else [])
    n_in = 7
    aliases = {n_in: 3, n_in + 1: 5}
    if prev_own is not None:
        aliases.update({n_in + 2 + k: 6 + k for k in range(4)})
    return pl.pallas_call(
        functools.partial(_mixproj_kernel, l, prev_own is not None),
        grid=(N_TOK // tm,),
        in_specs=[
            row(D_MODEL),
            _layer_spec(_MOD_SHAPE, l),
            _layer_spec((3, D_MODEL), l),
            _layer_spec((D_MODEL, MIX_IN), l, pipeline_mode=pl.Buffered(1)),
            _const_spec((DEPTH, 256)),
            _const_spec((DEPTH, 128)),
            _const_spec(tables.shape),
        ] + [any_spec] * len(aliased),
        out_specs=[row(768),
                   pl.BlockSpec((512, tm), lambda i: (0, i)),
                   pl.BlockSpec((2, tm, 256), lambda i: (0, ctx_row(i), 0)),
                   pl.BlockSpec((None, 2, tm, 256), lambda i: (lat_tile(i)[0], 0, lat_tile(i)[1], 0)),
                   pl.BlockSpec((2, 256, tm), lambda i: (0, 0, ctx_row(i))),
                   pl.BlockSpec((None, 2, 256, tm), lambda i: (lat_tile(i)[0], 0, 0, lat_tile(i)[1])),
                   own_spec(256), own_spec(256), own_spec(128), own_spec(128)],
        out_shape=[
            jax.ShapeDtypeStruct((N_TOK, 768), F32),
            jax.ShapeDtypeStruct((512, N_TOK), BF16),
            jax.ShapeDtypeStruct((2, N_CTX, 256), BF16),
            jax.ShapeDtypeStruct((DEC_BATCH, 2, S_LAT, 256), BF16),
            jax.ShapeDtypeStruct((2, 256, N_CTX), BF16),
            jax.ShapeDtypeStruct((DEC_BATCH, 2, 256, S_LAT), BF16),
            own_shape(256), own_shape(256), own_shape(128), own_shape(128),
        ],
        scratch_shapes=[pltpu.VMEM((D_MODEL, MIX_IN), BF16)],
        input_output_aliases=aliases,
        compiler_params=pltpu.CompilerParams(
            dimension_semantics=("arbitrary",), vmem_limit_bytes=VMEM_LIMIT),
        name="mixproj",
    )(x, mod, norm_pre, w_mix_in, qg, kg, tables, *aliased)


N_UNITS = 12
ATTN_SLOTS = 3


def _attn_kernel(l, qt_ref, k_ref, vt_ref, lq1_ref, lk1_ref, lq2_ref, lk2_ref, subln_ref, *rest):
    n = ATTN_SLOTS
    o_ref, qs_ref = rest[-3 - 3 * n], rest[-2 - 3 * n]
    s_refs = rest[-1 - 3 * n:-1 - 2 * n]
    e_refs = rest[-1 - 2 * n:-1 - n]
    il_refs = rest[-1 - n:-1]
    oacc_ref = rest[-1]
    tq = qt_ref.shape[1]
    half = k_ref.shape[1] // 2
    zero = jnp.zeros((), BF16)

    feat = lax.broadcasted_iota(jnp.int32, (256, 1), 0)
    qb = qt_ref[0:256, :]
    for j in range(8):
        qs_ref[j] = jnp.where(feat // DK_DIFF == j, qb, zero)
    for hd in range(4):
        g = hd // 2
        qs_ref[8 + hd] = jnp.zeros((256, tq), BF16)
        qs_ref[8 + hd, HD_GQA * g:HD_GQA * (g + 1), :] = (
            qt_ref[256 + HD_GQA * hd:256 + HD_GQA * (hd + 1), :])

    def scores(u):
        plane = 0 if u < 8 else 1
        s_ref = s_refs[u % n]
        s_ref[0:half, :] = _dot(k_ref[plane, 0:half, :], qs_ref[u])
        s_ref[half:, :] = _dot(k_ref[plane, half:, :], qs_ref[u])

    def softmax(u):
        s = s_refs[u % n][...]
        e = jnp.exp2(s - jnp.max(s, axis=0, keepdims=True))
        il_refs[u % n][...] = 1.0 / jnp.sum(e, axis=0, keepdims=True)
        e_refs[u % n][...] = e.astype(BF16)

    def values(u):
        plane = 0 if u < 8 else 1
        vrow = DV_DIFF * (u // 2) if u < 8 else HD_GQA * ((u - 8) // 2)
        e_ref = e_refs[u % n]
        o = (_dot(vt_ref[plane, vrow:vrow + DV_DIFF, 0:half], e_ref[0:half, :])
             + _dot(vt_ref[plane, vrow:vrow + DV_DIFF, half:], e_ref[half:, :]))
        oacc_ref[u] = o * il_refs[u % n][...]

    for t in range(N_UNITS + 2):
        if t < N_UNITS:
            scores(t)
        if 1 <= t <= N_UNITS:
            softmax(t - 1)
        if t >= 2:
            values(t - 2)

    lam_init = 0.8 - 0.6 * math.exp(-0.3 * l)
    row = slice(l, l + 1)
    lam = (jnp.exp(jnp.sum(lq1_ref[row, :] * lk1_ref[row, :], axis=-1, keepdims=True))
           - jnp.exp(jnp.sum(lq2_ref[row, :] * lk2_ref[row, :], axis=-1, keepdims=True))
           + lam_init)
    heads = []
    for hd in range(4):
        o = oacc_ref[2 * hd] - lam * oacc_ref[2 * hd + 1]
        nrm = lax.rsqrt(jnp.mean(o * o, axis=0, keepdims=True) + EPS)
        heads.append(o * nrm * (subln_ref[:, l:l + 1] * (1.0 - lam_init)))
    heads += [oacc_ref[8 + hd] for hd in range(4)]
    o_ref[...] = jnp.concatenate(heads, axis=0).T.astype(BF16)


def _attn_param_specs():
    vec = _const_spec((DEPTH, DK_DIFF))
    return [vec, vec, vec, vec, _const_spec((DV_DIFF, DEPTH))]


def _attn_scratch(tq, s_len):
    n = ATTN_SLOTS
    return ([pltpu.VMEM((N_UNITS, 256, tq), BF16)]
            + [pltpu.VMEM((s_len, tq), F32)] * n
            + [pltpu.VMEM((s_len, tq), BF16)] * n
            + [pltpu.VMEM((1, tq), F32)] * n
            + [pltpu.VMEM((N_UNITS, DV_DIFF, tq), F32)])


def _attn_ctx(qt, k_ctx, vt_ctx, aparams, l):
    return pl.pallas_call(
        functools.partial(_attn_kernel, l),
        grid=(BATCH,),
        in_specs=[pl.BlockSpec((512, SEQ), lambda b: (0, b)),
                  pl.BlockSpec((2, SEQ, 256), lambda b: (0, b, 0)),
                  pl.BlockSpec((2, 256, SEQ), lambda b: (0, 0, b))] + _attn_param_specs(),
        out_specs=pl.BlockSpec((SEQ, 512), lambda b: (b, 0)),
        out_shape=jax.ShapeDtypeStruct((N_TOK, 512), BF16),
        scratch_shapes=_attn_scratch(SEQ, SEQ),
        compiler_params=pltpu.CompilerParams(
            dimension_semantics=("arbitrary",), vmem_limit_bytes=VMEM_LIMIT),
        name="attn_ctx",
    )(qt, k_ctx, vt_ctx, *aparams)


def _attn_lat(qt, k_lat, vt_lat, att, aparams, l):
    tq = TQ_LAT
    nq = DEC_SEQ // tq
    tile = lambda b, i: N_CTX // tq + b * nq + i
    return pl.pallas_call(
        functools.partial(_attn_kernel, l),
        grid=(DEC_BATCH, nq),
        in_specs=[pl.BlockSpec((512, tq), lambda b, i: (0, tile(b, i))),
                  pl.BlockSpec((None, 2, S_LAT, 256), lambda b, i: (b, 0, 0, 0)),
                  pl.BlockSpec((None, 2, 256, S_LAT), lambda b, i: (b, 0, 0, 0))]
        + _attn_param_specs() + [pl.BlockSpec(memory_space=pl.ANY)],
        out_specs=pl.BlockSpec((tq, 512), lambda b, i: (tile(b, i), 0)),
        out_shape=jax.ShapeDtypeStruct((N_TOK, 512), BF16),
        scratch_shapes=_attn_scratch(tq, S_LAT),
        input_output_aliases={8: 0},
        compiler_params=pltpu.CompilerParams(
            dimension_semantics=("arbitrary", "arbitrary"), vmem_limit_bytes=VMEM_LIMIT),
        name="attn_lat",
    )(qt, k_lat, vt_lat, *aparams, att)


def _conv_kernel(l, seq_len, cin_ref, wsc_ref, wdw_ref, bdw_ref, lng_ref, lnb_ref, pw_ref, *rest):
    o_ref, pad_ref = rest[-2], rest[-1]
    t_tile = cin_ref.shape[0]
    n_seq = t_tile // seq_len
    stride = seq_len + CONV_HALO
    chunks_per_seq = seq_len // R_CONV
    rows = R_CONV

    for q in range(n_seq + 1):
        pad_ref[q * stride:q * stride + CONV_HALO, :] = jnp.zeros((CONV_HALO, 512), F32)
    for q in range(n_seq):
        pad_ref[CONV_HALO + q * stride:CONV_HALO + q * stride + seq_len, :] = (
            cin_ref[q * seq_len:(q + 1) * seq_len, 256:768])

    pw = pw_ref[...].astype(BF16)
    bdw = bdw_ref[l:l + 1, :]
    lng = lng_ref[l:l + 1, :]
    lnb = lnb_ref[l:l + 1, :]

    def body(c, carry):
        q = c // chunks_per_seq
        r = c % chunks_per_seq
        base = pl.multiple_of(q * stride + r * rows, 8)
        row0 = pl.multiple_of(c * rows, 8)
        win = pad_ref[pl.ds(base, rows + 2 * CONV_HALO), :]
        win_s = win[:, 0:256]
        win_u = win[:, 256:512]

        s_acc = None
        for k in range(SCONV_K):
            off = CONV_HALO + k - SCONV_K // 2
            term = wsc_ref[pl.ds(k, 1), :] * win_s[off:off + rows]
            s_acc = term if s_acc is None else s_acc + term
        a_out = cin_ref[pl.ds(row0, rows), 0:256] * s_acc

        acc = None
        for res in range(8):
            part = None
            for a in range(-2, 2):
                k = 8 * a + res + CCM_K // 2
                if 0 <= k < CCM_K:
                    lo = CONV_HALO + 8 * a
                    term = wdw_ref[pl.ds(k, 1), :] * win_u[lo:lo + rows + 8]
                    part = term if part is None else part + term
            part = part[res:res + rows]
            acc = part if acc is None else acc + part
        u = acc + bdw
        mu = jnp.mean(u, axis=-1, keepdims=True)
        var = jnp.mean(jnp.square(u - mu), axis=-1, keepdims=True)
        y = (u - mu) * lax.rsqrt(var + EPS) * lng + lnb
        y = y * jax.nn.sigmoid(y)
        c_out = _dot(y.astype(BF16), pw)
        o_ref[pl.ds(row0, rows), 0:256] = a_out.astype(BF16)
        o_ref[pl.ds(row0, rows), 256:512] = c_out.astype(BF16)
        return carry

    lax.fori_loop(0, t_tile // rows, body, 0)


def _conv(cin, prev, seq_len, tile0, n_tiles, cparams, l):
    t = T_CONV
    n_seq = t // seq_len
    vec = _const_spec((DEPTH, 256))
    in_specs = [pl.BlockSpec((t, 768), lambda i: (tile0 + i, 0)),
                _layer_spec((SCONV_K, 256), l), _layer_spec((CCM_K, 256), l), vec, vec, vec,
                _layer_spec((256, 256), l)]
    args = [cin, *cparams]
    aliases = {}
    if prev is not None:
        in_specs.append(pl.BlockSpec(memory_space=pl.ANY))
        args.append(prev)
        aliases = {len(args) - 1: 0}
    return pl.pallas_call(
        functools.partial(_conv_kernel, l, seq_len),
        grid=(n_tiles,),
        in_specs=in_specs,
        out_specs=pl.BlockSpec((t, 512), lambda i: (tile0 + i, 0)),
        out_shape=jax.ShapeDtypeStruct((N_TOK, 512), BF16),
        scratch_shapes=[pltpu.VMEM((n_seq * (seq_len + CONV_HALO) + CONV_HALO, 512), F32)],
        input_output_aliases=aliases,
        compiler_params=pltpu.CompilerParams(
            dimension_semantics=("arbitrary",), vmem_limit_bytes=VMEM_LIMIT),
        name=f"conv{seq_len}",
    )(*args)


def _outproj_kernel(ac_ref, att_ref, x_ref, mod_ref, gpost_ref, w_ref, o_ref, wbf_ref):
    i = pl.program_id(0)

    @pl.when(i == 0)
    def _():
        wbf_ref[...] = w_ref[...].astype(BF16)

    y = (_dot(ac_ref[:, 0:256], wbf_ref[0:256, :]) + _dot(att_ref[:, 0:256], wbf_ref[256:512, :])
         + _dot(ac_ref[:, 256:512], wbf_ref[512:768, :])
         + _dot(att_ref[:, 256:512], wbf_ref[768:1024, :]))
    gate = _mod_row(mod_ref, 5, _cond_of_tile(i, TM_OUT))
    o_ref[...] = x_ref[...] + gate * _rms(y, gpost_ref[pl.ds(1, 1), :])


def _outproj(ac, att, x, mod, norm_post, w_mix_out, l):
    tm = TM_OUT
    row = lambda w: pl.BlockSpec((tm, w), lambda i: (i, 0))
    return pl.pallas_call(
        _outproj_kernel,
        grid=(N_TOK // tm,),
        in_specs=[row(512), row(512), row(D_MODEL),
                  _layer_spec(_MOD_SHAPE, l),
                  _layer_spec((3, D_MODEL), l),
                  _layer_spec((D_MODEL, D_MODEL), l, pipeline_mode=pl.Buffered(1))],
        out_specs=row(D_MODEL),
        out_shape=jax.ShapeDtypeStruct((N_TOK, D_MODEL), F32),
        scratch_shapes=[pltpu.VMEM((D_MODEL, D_MODEL), BF16)],
        compiler_params=pltpu.CompilerParams(
            dimension_semantics=("arbitrary",), vmem_limit_bytes=VMEM_LIMIT),
        name="outproj",
    )(ac, att, x, mod, norm_post, w_mix_out)


def _kv_lat_base(cache_diff_k, cache_diff_v, cache_gqa_k, cache_gqa_v, l):
    pad128 = lambda a: jnp.pad(a.reshape(DEC_BATCH, PAST_LEN, 128), ((0, 0), (0, 0), (0, 128)))
    keys = jnp.stack([cache_diff_k[:, l].reshape(DEC_BATCH, PAST_LEN, 256),
                      pad128(cache_gqa_k[:, l])], axis=1).astype(BF16)
    vals = jnp.stack([cache_diff_v[:, l].reshape(DEC_BATCH, PAST_LEN, 256),
                      pad128(cache_gqa_v[:, l])], axis=1).astype(BF16)
    vals_t = jnp.swapaxes(vals, 2, 3)
    return (jnp.pad(keys, ((0, 0), (0, 0), (DEC_SEQ, 0), (0, 0))),
            jnp.pad(vals_t, ((0, 0), (0, 0), (0, 0), (DEC_SEQ, 0))))


def kernel(x_prompt, x_sample, cache_diff_k, cache_diff_v, cache_gqa_k, cache_gqa_v, c, c_ctx, w_ada, b_ada, norm_pre, norm_post, ffn1_gate, ffn1_up, ffn1_down, ffn2_gate, ffn2_up, ffn2_down, w_mix_in, w_mix_out, sconv_w, diff_lq1, diff_lk1, diff_lq2, diff_lk2, diff_subln, ccm_dw_w, ccm_dw_b, ccm_ln_g, ccm_ln_b, ccm_pw, gqa_qnorm, gqa_knorm):
    conds = jnp.concatenate(
        [c_ctx[None, :], c, jnp.zeros((N_COND - 1 - DEC_BATCH, D_MODEL), F32)], axis=0)
    mod = _ada(conds, w_ada, b_ada)
    tables = jnp.asarray(_rope_tables())
    qg = jnp.tile(gqa_qnorm, (1, 4))
    kg = jnp.tile(gqa_knorm, (1, 2))
    aparams = (diff_lq1, diff_lk1, diff_lq2, diff_lk2, diff_subln.T)
    cparams = (sconv_w, ccm_dw_w, ccm_dw_b, ccm_ln_g, ccm_ln_b, ccm_pw)

    xs = (x_prompt.reshape(N_CTX, D_MODEL), x_sample.reshape(N_LAT, D_MODEL))
    own = None
    for l in range(DEPTH):
        (x,) = _ffn(xs, mod, norm_pre, norm_post, ffn1_gate, ffn1_up, ffn1_down, l, 0)

        base = _kv_lat_base(cache_diff_k, cache_diff_v, cache_gqa_k, cache_gqa_v, l)
        cin, qt, k_ctx, k_lat, vt_ctx, vt_lat, *own = _mixproj(
            x, mod, norm_pre, w_mix_in, qg, kg, tables, base, own, l)

        att = _attn_ctx(qt, k_ctx, vt_ctx, aparams, l)
        att = _attn_lat(qt, k_lat, vt_lat, att, aparams, l)

        ac = _conv(cin, None, SEQ, 0, N_CTX // T_CONV, cparams, l)
        ac = _conv(cin, ac, DEC_SEQ, N_CTX // T_CONV, N_LAT // T_CONV, cparams, l)

        x = _outproj(ac, att, x, mod, norm_post, w_mix_out, l)
        xs = _ffn((x,), mod, norm_pre, norm_post, ffn2_gate, ffn2_up, ffn2_down, l, 2,
                  split_out=(l == DEPTH - 1))

    y_prompt, y_sample = xs
    odk, odv, ogk, ogv = own
    return (y_prompt.reshape(BATCH, SEQ, D_MODEL),
            y_sample.reshape(DEC_BATCH, DEC_SEQ, D_MODEL),
            odk.reshape(BATCH, DEPTH, SEQ, 4, 2, DK_DIFF),
            odv.reshape(BATCH, DEPTH, SEQ, 4, DV_DIFF),
            ogk.reshape(BATCH, DEPTH, SEQ, 2, HD_GQA),
            ogv.reshape(BATCH, DEPTH, SEQ, 2, HD_GQA))
```

```python
import functools
import math

import numpy as np

import jax
import jax.numpy as jnp
from jax import lax
from jax.experimental import pallas as pl
from jax.experimental.pallas import tpu as pltpu

F32 = jnp.float32
BF16 = jnp.bfloat16

D_MODEL = 1024
BATCH = 16
SEQ = 256
DEPTH = 2
DEC_BATCH = 2
DEC_SEQ = 2048
PAST_LEN = 256
GRID_W = 64
GRID_H = DEC_SEQ // GRID_W
ROPE_THETA = 10000.0
EPS = 1e-6
N_MOD = 9
D_FF = 2816
DK_DIFF = 32
DV_DIFF = 64
HD_GQA = 64
CCM_K = 31
SCONV_K = 3
MIX_IN = 2560

N_CTX = BATCH * SEQ
N_LAT = DEC_BATCH * DEC_SEQ
N_TOK = N_CTX + N_LAT
N_COND = 8
S_LAT = DEC_SEQ + PAST_LEN

LOG2E = math.log2(math.e)

VMEM_LIMIT = 56 * 1024 * 1024

TM_FFN = 2048
TM_FFN_SPLIT = 1024
TF_FFN = 256
TM_MIX = 512
TQ_LAT = 256
T_CONV = 2048
R_CONV = 64
CONV_HALO = 16
TM_OUT = 1024


def _cond_of_tile(i, tm):
    n_ctx_tiles = N_CTX // tm
    per_b = DEC_SEQ // tm
    return jnp.where(i < n_ctx_tiles, 0, 1 + (i - n_ctx_tiles) // per_b)


def _mod_row(mod_ref, k, cond):
    return mod_ref[k, pl.ds(cond, 1), :]


def _dot(a, b):
    return jnp.dot(a, b, preferred_element_type=F32)


def _dot_nt(a, b):
    return lax.dot_general(a, b, (((1,), (1,)), ((), ())), preferred_element_type=F32)


def _rms(x, g):
    ms = jnp.mean(x * x, axis=-1, keepdims=True)
    return x * lax.rsqrt(ms + EPS) * g


def _group_mean(xx, width, group):
    r = lax.broadcasted_iota(jnp.int32, (width, width), 0) // group
    c = lax.broadcasted_iota(jnp.int32, (width, width), 1) // group
    gmat = jnp.where(r == c, 1.0 / group, 0.0).astype(BF16)
    hi = xx.astype(BF16)
    lo = (xx - hi.astype(F32)).astype(BF16)
    return _dot(hi, gmat) + _dot(lo, gmat)


def _const_spec(shape):
    return pl.BlockSpec(shape, lambda *_: (0,) * len(shape))


def _layer_spec(shape, l, **kw):
    return pl.BlockSpec((None,) + shape, lambda *_: (l,) + (0,) * len(shape), **kw)


_MOD_SHAPE = (N_MOD, N_COND, D_MODEL)


def _ada_kernel(c_ref, w_ref, b_ref, o_ref):
    cnd = c_ref[...]
    s = (cnd * jax.nn.sigmoid(cnd)).astype(BF16)
    o_ref[...] = _dot(s, w_ref[...].astype(BF16)) + b_ref[...]


def _ada(conds, w_ada, b_ada):
    return pl.pallas_call(
        _ada_kernel,
        grid=(DEPTH, N_MOD),
        in_specs=[
            pl.BlockSpec((N_COND, D_MODEL), lambda l, n: (0, 0)),
            pl.BlockSpec((None, D_MODEL, D_MODEL), lambda l, n: (l, 0, n)),
            pl.BlockSpec((None, None, 1, D_MODEL), lambda l, n: (l, n, 0, 0)),
        ],
        out_specs=pl.BlockSpec((None, None, N_COND, D_MODEL), lambda l, n: (l, n, 0, 0)),
        out_shape=jax.ShapeDtypeStruct((DEPTH,) + _MOD_SHAPE, F32),
        compiler_params=pltpu.CompilerParams(
            dimension_semantics=("arbitrary", "arbitrary"), vmem_limit_bytes=VMEM_LIMIT),
        name="ada",
    )(conds, w_ada, b_ada.reshape(DEPTH, N_MOD, 1, D_MODEL))


def _ffn_kernel(sub, tm, n_x, n_o, *refs):
    x_refs = refs[:n_x]
    mod_ref, gpre_ref, gpost_ref, wg_ref, wu_ref, wd_ref = refs[n_x:n_x + 6]
    o_refs = refs[n_x + 6:n_x + 6 + n_o]
    h_ref = refs[n_x + 6 + n_o]
    acc_ref = o_refs[0] if n_o == 1 else refs[n_x + 7 + n_o]
    i = pl.program_id(0)
    j = pl.program_id(1)
    is_ctx = i < N_CTX // tm
    cond = _cond_of_tile(i, tm)

    def load_x():
        if n_x == 1:
            return x_refs[0][...]
        return jnp.where(is_ctx, x_refs[0][...], x_refs[1][...])

    @pl.when(j == 0)
    def _():
        shift = _mod_row(mod_ref, 3 * sub, cond)
        scale = _mod_row(mod_ref, 3 * sub + 1, cond)
        h = _rms(load_x(), gpre_ref[pl.ds(sub, 1), :]) * (1.0 + scale) + shift
        h_ref[...] = h.astype(BF16)
        acc_ref[...] = jnp.zeros_like(acc_ref)

    h = h_ref[...]
    g = _dot(h, wg_ref[...].astype(BF16))
    u = _dot(h, wu_ref[...].astype(BF16))
    a = (g * jax.nn.sigmoid(g)) * u
    acc_ref[...] += _dot(a.astype(BF16), wd_ref[...].astype(BF16))

    @pl.when(j == pl.num_programs(1) - 1)
    def _():
        gate = _mod_row(mod_ref, 3 * sub + 2, cond)
        y = load_x() + 0.5 * (gate * _rms(acc_ref[...], gpost_ref[pl.ds(sub, 1), :]))
        if n_o == 1:
            o_refs[0][...] = y
        else:
            @pl.when(is_ctx)
            def _():
                o_refs[0][...] = y

            @pl.when(jnp.logical_not(is_ctx))
            def _():
                o_refs[1][...] = y


def _ffn(xs, mod, norm_pre, norm_post, wg, wu, wd, l, sub, split_out=False):
    tf = TF_FFN
    tm = TM_FFN_SPLIT if (len(xs) == 2 or split_out) else TM_FFN
    n_ctx_tiles = N_CTX // tm
    ctx_spec = pl.BlockSpec((tm, D_MODEL), lambda i, j: (jnp.minimum(i, n_ctx_tiles - 1), 0))
    lat_spec = pl.BlockSpec((tm, D_MODEL), lambda i, j: (jnp.maximum(i - n_ctx_tiles, 0), 0))
    full_spec = pl.BlockSpec((tm, D_MODEL), lambda i, j: (i, 0))
    x_specs = [full_spec] if len(xs) == 1 else [ctx_spec, lat_spec]
    if split_out:
        out_specs = [ctx_spec, lat_spec]
        out_shape = [jax.ShapeDtypeStruct((N_CTX, D_MODEL), F32),
                     jax.ShapeDtypeStruct((N_LAT, D_MODEL), F32)]
    else:
        out_specs = [full_spec]
        out_shape = [jax.ShapeDtypeStruct((N_TOK, D_MODEL), F32)]
    scratch = [pltpu.VMEM((tm, D_MODEL), BF16)]
    if len(out_specs) == 2:
        scratch.append(pltpu.VMEM((tm, D_MODEL), F32))
    return pl.pallas_call(
        functools.partial(_ffn_kernel, sub, tm, len(xs), len(out_specs)),
        grid=(N_TOK // tm, D_FF // tf),
        in_specs=x_specs + [
            _layer_spec(_MOD_SHAPE, l),
            _layer_spec((3, D_MODEL), l),
            _layer_spec((3, D_MODEL), l),
            pl.BlockSpec((None, D_MODEL, tf), lambda i, j: (l, 0, j)),
            pl.BlockSpec((None, D_MODEL, tf), lambda i, j: (l, 0, j)),
            pl.BlockSpec((None, tf, D_MODEL), lambda i, j: (l, j, 0)),
        ],
        out_specs=out_specs,
        out_shape=out_shape,
        scratch_shapes=scratch,
        compiler_params=pltpu.CompilerParams(
            dimension_semantics=("arbitrary", "arbitrary"), vmem_limit_bytes=VMEM_LIMIT),
        name=f"ffn{sub}",
    )(*xs, mod, norm_pre, norm_post, wg, wu, wd)


def _swap_blocks(x, blk):
    w = x.shape[-1]
    lane = lax.broadcasted_iota(jnp.int32, x.shape, x.ndim - 1)
    first = (lane % (2 * blk)) < blk
    return jnp.where(first, pltpu.roll(x, w - blk, x.ndim - 1), pltpu.roll(x, blk, x.ndim - 1))


def _rope_tables():
    out = []
    for rot_dim in (DK_DIFF, HD_GQA):
        n_freq = rot_dim // 4
        inv = ROPE_THETA ** (-np.arange(n_freq, dtype=np.float64) / n_freq)
        lane = np.arange(256) % rot_dim
        pos = lane % (rot_dim // 2)
        freq = inv[pos % n_freq]
        sign = np.where(pos < n_freq, -1.0, 1.0)
        coord = np.concatenate([np.arange(GRID_H), np.arange(GRID_W)]).astype(np.float64)
        ang = coord[:, None] * freq[None, :]
        out.append(np.cos(ang))
        out.append(np.sin(ang) * sign[None, :])
    return np.stack(out).astype(np.float32)


def _tile_rope(tab_ref, which, rot_dim, first_grid_row, n_grid_rows, identity):
    col_part = tab_ref[which, GRID_H:GRID_H + GRID_W, :]
    lane = lax.broadcasted_iota(jnp.int32, (1, 256), 1)
    on_row_axis = (lane % rot_dim) < rot_dim // 2
    tab = jnp.concatenate(
        [jnp.where(on_row_axis, tab_ref[which, pl.ds(first_grid_row + r, 1), :], col_part)
         for r in range(n_grid_rows)], axis=0)
    return jnp.where(identity, 1.0 - (which % 2), tab)


def _mixproj_kernel(l, has_prev, x_ref, mod_ref, gpre_ref, w_ref, qg_ref, kg_ref, tab_ref, *rest):
    n_alias = 6 if has_prev else 2
    (cin_ref, qt_ref, kc_ref, kl_ref, vtc_ref, vtl_ref, odk_ref, odv_ref, ogk_ref, ogv_ref,
     wbf_ref) = rest[n_alias:]
    tm = TM_MIX
    i = pl.program_id(0)
    n_ctx_tiles = N_CTX // tm
    is_ctx = i < n_ctx_tiles
    cond = _cond_of_tile(i, tm)

    @pl.when(i == 0)
    def _():
        wbf_ref[...] = w_ref[...].astype(BF16)

    shift = _mod_row(mod_ref, 3, cond)
    scale = _mod_row(mod_ref, 4, cond)
    h = (_rms(x_ref[...], gpre_ref[pl.ds(1, 1), :]) * (1.0 + scale) + shift).astype(BF16)

    a = _dot(h, wbf_ref[:, 0:768])
    cin_ref[:, 0:256] = a[:, 0:256]
    cin_ref[:, 256:512] = a[:, 256:512] * a[:, 512:768]
    c = _dot(h, wbf_ref[:, 1536:2048])
    cin_ref[:, 512:768] = c[:, 0:256] * jax.nn.sigmoid(c[:, 256:512])

    grid_rows = tm // GRID_W
    first_row = (jnp.maximum(i - n_ctx_tiles, 0) % (DEC_SEQ // tm)) * grid_rows
    cd = _tile_rope(tab_ref, 0, DK_DIFF, first_row, grid_rows, is_ctx)
    sd = _tile_rope(tab_ref, 1, DK_DIFF, first_row, grid_rows, is_ctx)
    cg = _tile_rope(tab_ref, 2, HD_GQA, first_row, grid_rows, is_ctx)
    sg = _tile_rope(tab_ref, 3, HD_GQA, first_row, grid_rows, is_ctx)

    b = _dot(h, wbf_ref[:, 768:1536])
    bq, bk, bv = b[:, 0:256], b[:, 256:512], b[:, 512:768]
    bq_r = (bq * cd + _swap_blocks(bq, DK_DIFF // 4) * sd) * (DK_DIFF ** -0.5 * LOG2E)
    bk_r = bk * cd + _swap_blocks(bk, DK_DIFF // 4) * sd
    qt_ref[0:256, :] = bq_r.T.astype(BF16)

    d = _dot(h, wbf_ref[:, 2048:2560])
    dq, dk, dv = d[:, 0:256], d[:, 256:384], d[:, 384:512]
    dq_n = dq * lax.rsqrt(_group_mean(dq * dq, 256, HD_GQA) + EPS) * qg_ref[l:l + 1, :]
    dk_n = dk * lax.rsqrt(_group_mean(dk * dk, 128, HD_GQA) + EPS) * kg_ref[l:l + 1, :]
    dq_r = (dq_n * cg + _swap_blocks(dq_n, HD_GQA // 4) * sg) * (HD_GQA ** -0.5 * LOG2E)
    dk_r = dk_n * cg[:, 0:128] + _swap_blocks(dk_n, HD_GQA // 4) * sg[:, 0:128]
    qt_ref[256:512, :] = dq_r.T.astype(BF16)

    bk_b = bk_r.astype(BF16)
    dk_b = jnp.concatenate([dk_r, jnp.zeros_like(dk_r)], axis=1).astype(BF16)
    bv_t = bv.T.astype(BF16)
    dv_t = jnp.concatenate([dv, jnp.zeros_like(dv)], axis=1).T.astype(BF16)

    @pl.when(is_ctx)
    def _():
        kc_ref[0] = bk_b
        kc_ref[1] = dk_b
        vtc_ref[0] = bv_t
        vtc_ref[1] = dv_t
        seqs = tm // SEQ
        odk_ref[...] = bk.reshape(seqs, SEQ, 256)
        odv_ref[...] = bv.reshape(seqs, SEQ, 256)
        ogk_ref[...] = dk_n.reshape(seqs, SEQ, 128)
        ogv_ref[...] = dv.reshape(seqs, SEQ, 128)

    @pl.when(jnp.logical_not(is_ctx))
    def _():
        kl_ref[0] = bk_b
        kl_ref[1] = dk_b
        vtl_ref[0] = bv_t
        vtl_ref[1] = dv_t


def _mixproj(x, mod, norm_pre, w_mix_in, qg, kg, tables, lat_base, prev_own, l):
    tm = TM_MIX
    n_ctx_tiles = N_CTX // tm
    per_b = DEC_SEQ // tm
    seqs = tm // SEQ

    def lat_tile(i):
        t = jnp.maximum(i - n_ctx_tiles, 0)
        return t // per_b, t % per_b

    ctx_row = lambda i: jnp.minimum(i, n_ctx_tiles - 1)
    row = lambda w: pl.BlockSpec((tm, w), lambda i: (i, 0))
    own_spec = lambda w: pl.BlockSpec((seqs, None, SEQ, w), lambda i: (ctx_row(i), l, 0, 0))
    own_shape = lambda w: jax.ShapeDtypeStruct((BATCH, DEPTH, SEQ, w), F32)
    any_spec = pl.BlockSpec(memory_space=pl.ANY)
    aliased = list(lat_base) + (list(prev_own) if prev_own is not None else [])
    n_in = 7
    aliases = {n_in: 3, n_in + 1: 5}
    if prev_own is not None:
        aliases.update({n_in + 2 + k: 6 + k for k in range(4)})
    return pl.pallas_call(
        functools.partial(_mixproj_kernel, l, prev_own is not None),
        grid=(N_TOK // tm,),
        in_specs=[
            row(D_MODEL),
            _layer_spec(_MOD_SHAPE, l),
            _layer_spec((3, D_MODEL), l),
            _layer_spec((D_MODEL, MIX_IN), l, pipeline_mode=pl.Buffered(1)),
            _const_spec((DEPTH, 256)),
            _const_spec((DEPTH, 128)),
            _const_spec(tables.shape),
        ] + [any_spec] * len(aliased),
        out_specs=[row(768),
                   pl.BlockSpec((512, tm), lambda i: (0, i)),
                   pl.BlockSpec((2, tm, 256), lambda i: (0, ctx_row(i), 0)),
                   pl.BlockSpec((None, 2, tm, 256), lambda i: (lat_tile(i)[0], 0, lat_tile(i)[1], 0)),
                   pl.BlockSpec((2, 256, tm), lambda i: (0, 0, ctx_row(i))),
                   pl.BlockSpec((None, 2, 256, tm), lambda i: (lat_tile(i)[0], 0, 0, lat_tile(i)[1])),
                   own_spec(256), own_spec(256), own_spec(128), own_spec(128)],
        out_shape=[
            jax.ShapeDtypeStruct((N_TOK, 768), F32),
            jax.ShapeDtypeStruct((512, N_TOK), BF16),
            jax.ShapeDtypeStruct((2, N_CTX, 256), BF16),
            jax.ShapeDtypeStruct((DEC_BATCH, 2, S_LAT, 256), BF16),
            jax.ShapeDtypeStruct((2, 256, N_CTX), BF16),
            jax.ShapeDtypeStruct((DEC_BATCH, 2, 256, S_LAT), BF16),
            own_shape(256), own_shape(256), own_shape(128), own_shape(128),
        ],
        scratch_shapes=[pltpu.VMEM((D_MODEL, MIX_IN), BF16)],
        input_output_aliases=aliases,
        compiler_params=pltpu.CompilerParams(
            dimension_semantics=("arbitrary",), vmem_limit_bytes=VMEM_LIMIT),
        name="mixproj",
    )(x, mod, norm_pre, w_mix_in, qg, kg, tables, *aliased)


N_UNITS = 12
ATTN_SLOTS = 3


def _attn_kernel(l, qt_ref, k_ref, vt_ref, lq1_ref, lk1_ref, lq2_ref, lk2_ref, subln_ref, *rest):
    n = ATTN_SLOTS
    o_ref, qs_ref = rest[-3 - 3 * n], rest[-2 - 3 * n]
    s_refs = rest[-1 - 3 * n:-1 - 2 * n]
    e_refs = rest[-1 - 2 * n:-1 - n]
    il_refs = rest[-1 - n:-1]
    oacc_ref = rest[-1]
    tq = qt_ref.shape[1]
    half = k_ref.shape[1] // 2
    zero = jnp.zeros((), BF16)

    feat = lax.broadcasted_iota(jnp.int32, (256, 1), 0)
    qb = qt_ref[0:256, :]
    for j in range(8):
        qs_ref[j] = jnp.where(feat // DK_DIFF == j, qb, zero)
    for hd in range(4):
        g = hd // 2
        qs_ref[8 + hd] = jnp.zeros((256, tq), BF16)
        qs_ref[8 + hd, HD_GQA * g:HD_GQA * (g + 1), :] = (
            qt_ref[256 + HD_GQA * hd:256 + HD_GQA * (hd + 1), :])

    def scores(u):
        plane = 0 if u < 8 else 1
        s_ref = s_refs[u % n]
        s_ref[0:half, :] = _dot(k_ref[plane, 0:half, :], qs_ref[u])
        s_ref[half:, :] = _dot(k_ref[plane, half:, :], qs_ref[u])

    def softmax(u):
        s = s_refs[u % n][...]
        e = jnp.exp2(s - jnp.max(s, axis=0, keepdims=True))
        il_refs[u % n][...] = 1.0 / jnp.sum(e, axis=0, keepdims=True)
        e_refs[u % n][...] = e.astype(BF16)

    def values(u):
        plane = 0 if u < 8 else 1
        vrow = DV_DIFF * (u // 2) if u < 8 else HD_GQA * ((u - 8) // 2)
        e_ref = e_refs[u % n]
        o = (_dot(vt_ref[plane, vrow:vrow + DV_DIFF, 0:half], e_ref[0:half, :])
             + _dot(vt_ref[plane, vrow:vrow + DV_DIFF, half:], e_ref[half:, :]))
        oacc_ref[u] = o * il_refs[u % n][...]

    for t in range(N_UNITS + 2):
        if t < N_UNITS:
            scores(t)
        if 1 <= t <= N_UNITS:
            softmax(t - 1)
        if t >= 2:
            values(t - 2)

    lam_init = 0.8 - 0.6 * math.exp(-0.3 * l)
    row = slice(l, l + 1)
    lam = (jnp.exp(jnp.sum(lq1_ref[row, :] * lk1_ref[row, :], axis=-1, keepdims=True))
           - jnp.exp(jnp.sum(lq2_ref[row, :] * lk2_ref[row, :], axis=-1, keepdims=True))
           + lam_init)
    heads = []
    for hd in range(4):
        o = oacc_ref[2 * hd] - lam * oacc_ref[2 * hd + 1]
        nrm = lax.rsqrt(jnp.mean(o * o, axis=0, keepdims=True) + EPS)
        heads.append(o * nrm * (subln_ref[:, l:l + 1] * (1.0 - lam_init)))
    heads += [oacc_ref[8 + hd] for hd in range(4)]
    o_ref[...] = jnp.concatenate(heads, axis=0).T.astype(BF16)


def _attn_param_specs():
    vec = _const_spec((DEPTH, DK_DIFF))
    return [vec, vec, vec, vec, _const_spec((DV_DIFF, DEPTH))]


def _attn_scratch(tq, s_len):
    n = ATTN_SLOTS
    return ([pltpu.VMEM((N_UNITS, 256, tq), BF16)]
            + [pltpu.VMEM((s_len, tq), F32)] * n
            + [pltpu.VMEM((s_len, tq), BF16)] * n
            + [pltpu.VMEM((1, tq), F32)] * n
            + [pltpu.VMEM((N_UNITS, DV_DIFF, tq), F32)])


def _attn_ctx(qt, k_ctx, vt_ctx, aparams, l):
    return pl.pallas_call(
        functools.partial(_attn_kernel, l),
        grid=(BATCH,),
        in_specs=[pl.BlockSpec((512, SEQ), lambda b: (0, b)),
                  pl.BlockSpec((2, SEQ, 256), lambda b: (0, b, 0)),
                  pl.BlockSpec((2, 256, SEQ), lambda b: (0, 0, b))] + _attn_param_specs(),
        out_specs=pl.BlockSpec((SEQ, 512), lambda b: (b, 0)),
        out_shape=jax.ShapeDtypeStruct((N_TOK, 512), BF16),
        scratch_shapes=_attn_scratch(SEQ, SEQ),
        compiler_params=pltpu.CompilerParams(
            dimension_semantics=("arbitrary",), vmem_limit_bytes=VMEM_LIMIT),
        name="attn_ctx",
    )(qt, k_ctx, vt_ctx, *aparams)


def _attn_lat(qt, k_lat, vt_lat, att, aparams, l):
    tq = TQ_LAT
    nq = DEC_SEQ // tq
    tile = lambda b, i: N_CTX // tq + b * nq + i
    return pl.pallas_call(
        functools.partial(_attn_kernel, l),
        grid=(DEC_BATCH, nq),
        in_specs=[pl.BlockSpec((512, tq), lambda b, i: (0, tile(b, i))),
                  pl.BlockSpec((None, 2, S_LAT, 256), lambda b, i: (b, 0, 0, 0)),
                  pl.BlockSpec((None, 2, 256, S_LAT), lambda b, i: (b, 0, 0, 0))]
        + _attn_param_specs() + [pl.BlockSpec(memory_space=pl.ANY)],
        out_specs=pl.BlockSpec((tq, 512), lambda b, i: (tile(b, i), 0)),
        out_shape=jax.ShapeDtypeStruct((N_TOK, 512), BF16),
        scratch_shapes=_attn_scratch(tq, S_LAT),
        input_output_aliases={8: 0},
        compiler_params=pltpu.CompilerParams(
            dimension_semantics=("arbitrary", "arbitrary"), vmem_limit_bytes=VMEM_LIMIT),
        name="attn_lat",
    )(qt, k_lat, vt_lat, *aparams, att)


def _conv_kernel(l, seq_len, cin_ref, wsc_ref, wdw_ref, bdw_ref, lng_ref, lnb_ref, pw_ref, *rest):
    o_ref, pad_ref = rest[-2], rest[-1]
    t_tile = cin_ref.shape[0]
    n_seq = t_tile // seq_len
    stride = seq_len + CONV_HALO
    chunks_per_seq = seq_len // R_CONV
    rows = R_CONV

    for q in range(n_seq + 1):
        pad_ref[q * stride:q * stride + CONV_HALO, :] = jnp.zeros((CONV_HALO, 512), F32)
    for q in range(n_seq):
        pad_ref[CONV_HALO + q * stride:CONV_HALO + q * stride + seq_len, :] = (
            cin_ref[q * seq_len:(q + 1) * seq_len, 256:768])

    pw = pw_ref[...].astype(BF16)
    bdw = bdw_ref[l:l + 1, :]
    lng = lng_ref[l:l + 1, :]
    lnb = lnb_ref[l:l + 1, :]

    def body(c, carry):
        q = c // chunks_per_seq
        r = c % chunks_per_seq
        base = pl.multiple_of(q * stride + r * rows, 8)
        row0 = pl.multiple_of(c * rows, 8)
        win = pad_ref[pl.ds(base, rows + 2 * CONV_HALO), :]
        win_s = win[:, 0:256]
        win_u = win[:, 256:512]

        s_acc = None
        for k in range(SCONV_K):
            off = CONV_HALO + k - SCONV_K // 2
            term = wsc_ref[pl.ds(k, 1), :] * win_s[off:off + rows]
            s_acc = term if s_acc is None else s_acc + term
        a_out = cin_ref[pl.ds(row0, rows), 0:256] * s_acc

        acc = None
        for res in range(8):
            part = None
            for a in range(-2, 2):
                k = 8 * a + res + CCM_K // 2
                if 0 <= k < CCM_K:
                    lo = CONV_HALO + 8 * a
                    term = wdw_ref[pl.ds(k, 1), :] * win_u[lo:lo + rows + 8]
                    part = term if part is None else part + term
            part = part[res:res + rows]
            acc = part if acc is None else acc + part
        u = acc + bdw
        mu = jnp.mean(u, axis=-1, keepdims=True)
        var = jnp.mean(jnp.square(u - mu), axis=-1, keepdims=True)
        y = (u - mu) * lax.rsqrt(var + EPS) * lng + lnb
        y = y * jax.nn.sigmoid(y)
        c_out = _dot(y.astype(BF16), pw)
        o_ref[pl.ds(row0, rows), 0:256] = a_out.astype(BF16)
        o_ref[pl.ds(row0, rows), 256:512] = c_out.astype(BF16)
        return carry

    lax.fori_loop(0, t_tile // rows, body, 0, unroll=2)


def _conv(cin, prev, seq_len, tile0, n_tiles, cparams, l):
    t = T_CONV
    n_seq = t // seq_len
    vec = _const_spec((DEPTH, 256))
    in_specs = [pl.BlockSpec((t, 768), lambda i: (tile0 + i, 0)),
                _layer_spec((SCONV_K, 256), l), _layer_spec((CCM_K, 256), l), vec, vec, vec,
                _layer_spec((256, 256), l)]
    args = [cin, *cparams]
    aliases = {}
    if prev is not None:
        in_specs.append(pl.BlockSpec(memory_space=pl.ANY))
        args.append(prev)
        aliases = {len(args) - 1: 0}
    return pl.pallas_call(
        functools.partial(_conv_kernel, l, seq_len),
        grid=(n_tiles,),
        in_specs=in_specs,
        out_specs=pl.BlockSpec((t, 512), lambda i: (tile0 + i, 0)),
        out_shape=jax.ShapeDtypeStruct((N_TOK, 512), BF16),
        scratch_shapes=[pltpu.VMEM((n_seq * (seq_len + CONV_HALO) + CONV_HALO, 512), F32)],
        input_output_aliases=aliases,
        compiler_params=pltpu.CompilerParams(
            dimension_semantics=("arbitrary",), vmem_limit_bytes=VMEM_LIMIT),
        name=f"conv{seq_len}",
    )(*args)


def _outproj_kernel(ac_ref, att_ref, x_ref, mod_ref, gpost_ref, w_ref, o_ref, wbf_ref):
    i = pl.program_id(0)

    @pl.when(i == 0)
    def _():
        wbf_ref[...] = w_ref[...].astype(BF16)

    y = (_dot(ac_ref[:, 0:256], wbf_ref[0:256, :]) + _dot(att_ref[:, 0:256], wbf_ref[256:512, :])
         + _dot(ac_ref[:, 256:512], wbf_ref[512:768, :])
         + _dot(att_ref[:, 256:512], wbf_ref[768:1024, :]))
    gate = _mod_row(mod_ref, 5, _cond_of_tile(i, TM_OUT))
    o_ref[...] = x_ref[...] + gate * _rms(y, gpost_ref[pl.ds(1, 1), :])


def _outproj(ac, att, x, mod, norm_post, w_mix_out, l):
    tm = TM_OUT
    row = lambda w: pl.BlockSpec((tm, w), lambda i: (i, 0))
    return pl.pallas_call(
        _outproj_kernel,
        grid=(N_TOK // tm,),
        in_specs=[row(512), row(512), row(D_MODEL),
                  _layer_spec(_MOD_SHAPE, l),
                  _layer_spec((3, D_MODEL), l),
                  _layer_spec((D_MODEL, D_MODEL), l, pipeline_mode=pl.Buffered(1))],
        out_specs=row(D_MODEL),
        out_shape=jax.ShapeDtypeStruct((N_TOK, D_MODEL), F32),
        scratch_shapes=[pltpu.VMEM((D_MODEL, D_MODEL), BF16)],
        compiler_params=pltpu.CompilerParams(
            dimension_semantics=("arbitrary",), vmem_limit_bytes=VMEM_LIMIT),
        name="outproj",
    )(ac, att, x, mod, norm_post, w_mix_out)


def _kv_lat_base(cache_diff_k, cache_diff_v, cache_gqa_k, cache_gqa_v, l):
    pad128 = lambda a: jnp.pad(a.reshape(DEC_BATCH, PAST_LEN, 128), ((0, 0), (0, 0), (0, 128)))
    keys = jnp.stack([cache_diff_k[:, l].reshape(DEC_BATCH, PAST_LEN, 256),
                      pad128(cache_gqa_k[:, l])], axis=1).astype(BF16)
    vals = jnp.stack([cache_diff_v[:, l].reshape(DEC_BATCH, PAST_LEN, 256),
                      pad128(cache_gqa_v[:, l])], axis=1).astype(BF16)
    vals_t = jnp.swapaxes(vals, 2, 3)
    return (jnp.pad(keys, ((0, 0), (0, 0), (DEC_SEQ, 0), (0, 0))),
            jnp.pad(vals_t, ((0, 0), (0, 0), (0, 0), (DEC_SEQ, 0))))


def kernel(x_prompt, x_sample, cache_diff_k, cache_diff_v, cache_gqa_k, cache_gqa_v, c, c_ctx, w_ada, b_ada, norm_pre, norm_post, ffn1_gate, ffn1_up, ffn1_down, ffn2_gate, ffn2_up, ffn2_down, w_mix_in, w_mix_out, sconv_w, diff_lq1, diff_lk1, diff_lq2, diff_lk2, diff_subln, ccm_dw_w, ccm_dw_b, ccm_ln_g, ccm_ln_b, ccm_pw, gqa_qnorm, gqa_knorm):
    conds = jnp.concatenate(
        [c_ctx[None, :], c, jnp.zeros((N_COND - 1 - DEC_BATCH, D_MODEL), F32)], axis=0)
    mod = _ada(conds, w_ada, b_ada)
    tables = jnp.asarray(_rope_tables())
    qg = jnp.tile(gqa_qnorm, (1, 4))
    kg = jnp.tile(gqa_knorm, (1, 2))
    aparams = (diff_lq1, diff_lk1, diff_lq2, diff_lk2, diff_subln.T)
    cparams = (sconv_w, ccm_dw_w, ccm_dw_b, ccm_ln_g, ccm_ln_b, ccm_pw)

    xs = (x_prompt.reshape(N_CTX, D_MODEL), x_sample.reshape(N_LAT, D_MODEL))
    own = None
    for l in range(DEPTH):
        (x,) = _ffn(xs, mod, norm_pre, norm_post, ffn1_gate, ffn1_up, ffn1_down, l, 0)

        base = _kv_lat_base(cache_diff_k, cache_diff_v, cache_gqa_k, cache_gqa_v, l)
        cin, qt, k_ctx, k_lat, vt_ctx, vt_lat, *own = _mixproj(
            x, mod, norm_pre, w_mix_in, qg, kg, tables, base, own, l)

        att = _attn_ctx(qt, k_ctx, vt_ctx, aparams, l)
        att = _attn_lat(qt, k_lat, vt_lat, att, aparams, l)

        ac = _conv(cin, None, SEQ, 0, N_CTX // T_CONV, cparams, l)
        ac = _conv(cin, ac, DEC_SEQ, N_CTX // T_CONV, N_LAT // T_CONV, cparams, l)

        x = _outproj(ac, att, x, mod, norm_post, w_mix_out, l)
        xs = _ffn((x,), mod, norm_pre, norm_post, ffn2_gate, ffn2_up, ffn2_down, l, 2,
                  split_out=(l == DEPTH - 1))

    y_prompt, y_sample = xs
    odk, odv, ogk, ogv = own
    return (y_prompt.reshape(BATCH, SEQ, D_MODEL),
            y_sample.reshape(DEC_BATCH, DEC_SEQ, D_MODEL),
            odk.reshape(BATCH, DEPTH, SEQ, 4, 2, DK_DIFF),
            odv.reshape(BATCH, DEPTH, SEQ, 4, DV_DIFF),
            ogk.reshape(BATCH, DEPTH, SEQ, 2, HD_GQA),
            ogv.reshape(BATCH, DEPTH, SEQ, 2, HD_GQA))
```

```python
import functools
import math

import numpy as np

import jax
import jax.numpy as jnp
from jax import lax
from jax.experimental import pallas as pl
from jax.experimental.pallas import tpu as pltpu

F32 = jnp.float32
BF16 = jnp.bfloat16

D_MODEL = 1024
BATCH = 16
SEQ = 256
DEPTH = 2
DEC_BATCH = 2
DEC_SEQ = 2048
PAST_LEN = 256
GRID_W = 64
GRID_H = DEC_SEQ // GRID_W
ROPE_THETA = 10000.0
EPS = 1e-6
N_MOD = 9
D_FF = 2816
DK_DIFF = 32
DV_DIFF = 64
HD_GQA = 64
CCM_K = 31
SCONV_K = 3
MIX_IN = 2560

N_CTX = BATCH * SEQ
N_LAT = DEC_BATCH * DEC_SEQ
N_TOK = N_CTX + N_LAT
N_COND = 8
S_LAT = DEC_SEQ + PAST_LEN

LOG2E = math.log2(math.e)

VMEM_LIMIT = 56 * 1024 * 1024

TM_FFN = 512
FFN_SUB = 256
FFN_CHUNK = 128
FFN_STAGE_SLOTS = 4
TM_MIX = 512
TQ_LAT = 256
T_CONV = 2048
R_CONV = 64
CONV_HALO = 16
TM_OUT = 1024


def _cond_of_tile(i, tm):
    n_ctx_tiles = N_CTX // tm
    per_b = DEC_SEQ // tm
    return jnp.where(i < n_ctx_tiles, 0, 1 + (i - n_ctx_tiles) // per_b)


def _mod_row(mod_ref, k, cond):
    return mod_ref[k, pl.ds(cond, 1), :]


def _dot(a, b):
    return jnp.dot(a, b, preferred_element_type=F32)


def _dot_nt(a, b):
    return lax.dot_general(a, b, (((1,), (1,)), ((), ())), preferred_element_type=F32)


def _rms(x, g):
    ms = jnp.mean(x * x, axis=-1, keepdims=True)
    return x * lax.rsqrt(ms + EPS) * g


def _group_mean(xx, width, group):
    r = lax.broadcasted_iota(jnp.int32, (width, width), 0) // group
    c = lax.broadcasted_iota(jnp.int32, (width, width), 1) // group
    gmat = jnp.where(r == c, 1.0 / group, 0.0).astype(BF16)
    hi = xx.astype(BF16)
    lo = (xx - hi.astype(F32)).astype(BF16)
    return _dot(hi, gmat) + _dot(lo, gmat)


def _const_spec(shape):
    return pl.BlockSpec(shape, lambda *_: (0,) * len(shape))


def _layer_spec(shape, l, **kw):
    return pl.BlockSpec((None,) + shape, lambda *_: (l,) + (0,) * len(shape), **kw)


_MOD_SHAPE = (N_MOD, N_COND, D_MODEL)


def _ada_kernel(c_ref, w_ref, b_ref, o_ref):
    cnd = c_ref[...]
    s = (cnd * jax.nn.sigmoid(cnd)).astype(BF16)
    o_ref[...] = _dot(s, w_ref[...].astype(BF16)) + b_ref[...]


def _ada(conds, w_ada, b_ada):
    return pl.pallas_call(
        _ada_kernel,
        grid=(DEPTH, N_MOD),
        in_specs=[
            pl.BlockSpec((N_COND, D_MODEL), lambda l, n: (0, 0)),
            pl.BlockSpec((None, D_MODEL, D_MODEL), lambda l, n: (l, 0, n)),
            pl.BlockSpec((None, None, 1, D_MODEL), lambda l, n: (l, n, 0, 0)),
        ],
        out_specs=pl.BlockSpec((None, None, N_COND, D_MODEL), lambda l, n: (l, n, 0, 0)),
        out_shape=jax.ShapeDtypeStruct((DEPTH,) + _MOD_SHAPE, F32),
        compiler_params=pltpu.CompilerParams(
            dimension_semantics=("arbitrary", "arbitrary"), vmem_limit_bytes=VMEM_LIMIT),
        name="ada",
    )(conds, w_ada, b_ada.reshape(DEPTH, N_MOD, 1, D_MODEL))


def _stage_weights(jobs, stage_ref, sem_ref):
    depth = FFN_STAGE_SLOTS
    copies = [pltpu.make_async_copy(src, stage_ref.at[k % depth], sem_ref.at[k % depth])
              for k, (src, _) in enumerate(jobs)]
    for k in range(min(depth - 1, len(jobs))):
        copies[k].start()
    for k, (_, store) in enumerate(jobs):
        if k + depth - 1 < len(jobs):
            copies[k + depth - 1].start()
        copies[k].wait()
        store(stage_ref[k % depth].astype(BF16))


def _ffn_kernel(l, sub, tm, n_x, n_o, *refs):
    x_refs = refs[:n_x]
    mod_ref, gpre_ref, gpost_ref, wg_hbm, wu_hbm, wd_hbm = refs[n_x:n_x + 6]
    o_refs = refs[n_x + 6:n_x + 6 + n_o]
    wg_ref, wu_ref, wd_ref, stage_in, stage_dn, sem_in, sem_dn = refs[n_x + 6 + n_o:]
    i = pl.program_id(0)
    is_ctx = i < N_CTX // tm
    cond = _cond_of_tile(i, tm)

    @pl.when(i == 0)
    def _():
        c = FFN_CHUNK
        n_chunks = D_FF // c

        def col_store(dst, j):
            def store(v):
                dst[:, j * c:(j + 1) * c] = v
            return store

        def row_store(dst, j):
            def store(v):
                dst[j * c:(j + 1) * c, :] = v
            return store

        _stage_weights(
            [(src.at[l, :, pl.ds(j * c, c)], col_store(dst, j))
             for src, dst in ((wg_hbm, wg_ref), (wu_hbm, wu_ref)) for j in range(n_chunks)],
            stage_in, sem_in)
        _stage_weights(
            [(wd_hbm.at[l, pl.ds(j * c, c), :], row_store(wd_ref, j)) for j in range(n_chunks)],
            stage_dn, sem_dn)

    if n_x == 1:
        x = x_refs[0][...]
    else:
        x = jnp.where(is_ctx, x_refs[0][...], x_refs[1][...])
    shift = _mod_row(mod_ref, 3 * sub, cond)
    scale = _mod_row(mod_ref, 3 * sub + 1, cond)
    gate = _mod_row(mod_ref, 3 * sub + 2, cond)
    h = (_rms(x, gpre_ref[pl.ds(sub, 1), :]) * (1.0 + scale) + shift).astype(BF16)

    outs = []
    for s in range(tm // FFN_SUB):
        rows = slice(s * FFN_SUB, (s + 1) * FFN_SUB)
        g = _dot(h[rows], wg_ref[...])
        u = _dot(h[rows], wu_ref[...])
        a = ((g * jax.nn.sigmoid(g)) * u).astype(BF16)
        y = _dot(a, wd_ref[...])
        outs.append(x[rows] + 0.5 * (gate * _rms(y, gpost_ref[pl.ds(sub, 1), :])))
    y = jnp.concatenate(outs, axis=0)
    if n_o == 1:
        o_refs[0][...] = y
    else:
        @pl.when(is_ctx)
        def _():
            o_refs[0][...] = y

        @pl.when(jnp.logical_not(is_ctx))
        def _():
            o_refs[1][...] = y


def _ffn(xs, mod, norm_pre, norm_post, wg, wu, wd, l, sub, split_out=False):
    tm = TM_FFN
    n_ctx_tiles = N_CTX // tm
    ctx_spec = pl.BlockSpec((tm, D_MODEL), lambda i: (jnp.minimum(i, n_ctx_tiles - 1), 0))
    lat_spec = pl.BlockSpec((tm, D_MODEL), lambda i: (jnp.maximum(i - n_ctx_tiles, 0), 0))
    full_spec = pl.BlockSpec((tm, D_MODEL), lambda i: (i, 0))
    x_specs = [full_spec] if len(xs) == 1 else [ctx_spec, lat_spec]
    if split_out:
        out_specs = [ctx_spec, lat_spec]
        out_shape = [jax.ShapeDtypeStruct((N_CTX, D_MODEL), F32),
                     jax.ShapeDtypeStruct((N_LAT, D_MODEL), F32)]
    else:
        out_specs = [full_spec]
        out_shape = [jax.ShapeDtypeStruct((N_TOK, D_MODEL), F32)]
    hbm = pl.BlockSpec(memory_space=pl.ANY)
    return pl.pallas_call(
        functools.partial(_ffn_kernel, l, sub, tm, len(xs), len(out_specs)),
        grid=(N_TOK // tm,),
        in_specs=x_specs + [
            _layer_spec(_MOD_SHAPE, l),
            _layer_spec((3, D_MODEL), l),
            _layer_spec((3, D_MODEL), l),
            hbm, hbm, hbm,
        ],
        out_specs=out_specs,
        out_shape=out_shape,
        scratch_shapes=[
            pltpu.VMEM((D_MODEL, D_FF), BF16), pltpu.VMEM((D_MODEL, D_FF), BF16),
            pltpu.VMEM((D_FF, D_MODEL), BF16),
            pltpu.VMEM((FFN_STAGE_SLOTS, D_MODEL, FFN_CHUNK), F32),
            pltpu.VMEM((FFN_STAGE_SLOTS, FFN_CHUNK, D_MODEL), F32),
            pltpu.SemaphoreType.DMA((FFN_STAGE_SLOTS,)),
            pltpu.SemaphoreType.DMA((FFN_STAGE_SLOTS,)),
        ],
        compiler_params=pltpu.CompilerParams(
            dimension_semantics=("arbitrary",), vmem_limit_bytes=VMEM_LIMIT),
        name=f"ffn{sub}",
    )(*xs, mod, norm_pre, norm_post, wg, wu, wd)


def _swap_blocks(x, blk):
    w = x.shape[-1]
    lane = lax.broadcasted_iota(jnp.int32, x.shape, x.ndim - 1)
    first = (lane % (2 * blk)) < blk
    return jnp.where(first, pltpu.roll(x, w - blk, x.ndim - 1), pltpu.roll(x, blk, x.ndim - 1))


def _rope_tables():
    out = []
    for rot_dim in (DK_DIFF, HD_GQA):
        n_freq = rot_dim // 4
        inv = ROPE_THETA ** (-np.arange(n_freq, dtype=np.float64) / n_freq)
        lane = np.arange(256) % rot_dim
        pos = lane % (rot_dim // 2)
        freq = inv[pos % n_freq]
        sign = np.where(pos < n_freq, -1.0, 1.0)
        coord = np.concatenate([np.arange(GRID_H), np.arange(GRID_W)]).astype(np.float64)
        ang = coord[:, None] * freq[None, :]
        out.append(np.cos(ang))
        out.append(np.sin(ang) * sign[None, :])
    return np.stack(out).astype(np.float32)


def _tile_rope(tab_ref, which, rot_dim, first_grid_row, n_grid_rows, identity):
    col_part = tab_ref[which, GRID_H:GRID_H + GRID_W, :]
    lane = lax.broadcasted_iota(jnp.int32, (1, 256), 1)
    on_row_axis = (lane % rot_dim) < rot_dim // 2
    tab = jnp.concatenate(
        [jnp.where(on_row_axis, tab_ref[which, pl.ds(first_grid_row + r, 1), :], col_part)
         for r in range(n_grid_rows)], axis=0)
    return jnp.where(identity, 1.0 - (which % 2), tab)


def _mixproj_kernel(l, has_prev, x_ref, mod_ref, gpre_ref, w_ref, qg_ref, kg_ref, tab_ref, *rest):
    n_alias = 6 if has_prev else 2
    (cin_ref, qt_ref, kc_ref, kl_ref, vtc_ref, vtl_ref, odk_ref, odv_ref, ogk_ref, ogv_ref,
     wbf_ref) = rest[n_alias:]
    tm = TM_MIX
    i = pl.program_id(0)
    n_ctx_tiles = N_CTX // tm
    is_ctx = i < n_ctx_tiles
    cond = _cond_of_tile(i, tm)

    @pl.when(i == 0)
    def _():
        wbf_ref[...] = w_ref[...].astype(BF16)

    shift = _mod_row(mod_ref, 3, cond)
    scale = _mod_row(mod_ref, 4, cond)
    h = (_rms(x_ref[...], gpre_ref[pl.ds(1, 1), :]) * (1.0 + scale) + shift).astype(BF16)

    a = _dot(h, wbf_ref[:, 0:768])
    cin_ref[:, 0:256] = a[:, 0:256]
    cin_ref[:, 256:512] = a[:, 256:512] * a[:, 512:768]
    c = _dot(h, wbf_ref[:, 1536:2048])
    cin_ref[:, 512:768] = c[:, 0:256] * jax.nn.sigmoid(c[:, 256:512])

    grid_rows = tm // GRID_W
    first_row = (jnp.maximum(i - n_ctx_tiles, 0) % (DEC_SEQ // tm)) * grid_rows
    cd = _tile_rope(tab_ref, 0, DK_DIFF, first_row, grid_rows, is_ctx)
    sd = _tile_rope(tab_ref, 1, DK_DIFF, first_row, grid_rows, is_ctx)
    cg = _tile_rope(tab_ref, 2, HD_GQA, first_row, grid_rows, is_ctx)
    sg = _tile_rope(tab_ref, 3, HD_GQA, first_row, grid_rows, is_ctx)

    b = _dot(h, wbf_ref[:, 768:1536])
    bq, bk, bv = b[:, 0:256], b[:, 256:512], b[:, 512:768]
    bq_r = (bq * cd + _swap_blocks(bq, DK_DIFF // 4) * sd) * (DK_DIFF ** -0.5 * LOG2E)
    bk_r = bk * cd + _swap_blocks(bk, DK_DIFF // 4) * sd
    qt_ref[0:256, :] = bq_r.T.astype(BF16)

    d = _dot(h, wbf_ref[:, 2048:2560])
    dq, dk, dv = d[:, 0:256], d[:, 256:384], d[:, 384:512]
    dq_n = dq * lax.rsqrt(_group_mean(dq * dq, 256, HD_GQA) + EPS) * qg_ref[l:l + 1, :]
    dk_n = dk * lax.rsqrt(_group_mean(dk * dk, 128, HD_GQA) + EPS) * kg_ref[l:l + 1, :]
    dq_r = (dq_n * cg + _swap_blocks(dq_n, HD_GQA // 4) * sg) * (HD_GQA ** -0.5 * LOG2E)
    dk_r = dk_n * cg[:, 0:128] + _swap_blocks(dk_n, HD_GQA // 4) * sg[:, 0:128]
    qt_ref[256:512, :] = dq_r.T.astype(BF16)

    bk_b = bk_r.astype(BF16)
    dk_b = jnp.concatenate([dk_r, jnp.zeros_like(dk_r)], axis=1).astype(BF16)
    bv_t = bv.T.astype(BF16)
    dv_t = jnp.concatenate([dv, jnp.zeros_like(dv)], axis=1).T.astype(BF16)

    @pl.when(is_ctx)
    def _():
        kc_ref[0] = bk_b
        kc_ref[1] = dk_b
        vtc_ref[0] = bv_t
        vtc_ref[1] = dv_t
        seqs = tm // SEQ
        odk_ref[...] = bk.reshape(seqs, SEQ, 256)
        odv_ref[...] = bv.reshape(seqs, SEQ, 256)
        ogk_ref[...] = dk_n.reshape(seqs, SEQ, 128)
        ogv_ref[...] = dv.reshape(seqs, SEQ, 128)

    @pl.when(jnp.logical_not(is_ctx))
    def _():
        kl_ref[0] = bk_b
        kl_ref[1] = dk_b
        vtl_ref[0] = bv_t
        vtl_ref[1] = dv_t


def _mixproj(x, mod, norm_pre, w_mix_in, qg, kg, tables, lat_base, prev_own, l):
    tm = TM_MIX
    n_ctx_tiles = N_CTX // tm
    per_b = DEC_SEQ // tm
    seqs = tm // SEQ

    def lat_tile(i):
        t = jnp.maximum(i - n_ctx_tiles, 0)
        return t // per_b, t % per_b

    ctx_row = lambda i: jnp.minimum(i, n_ctx_tiles - 1)
    row = lambda w: pl.BlockSpec((tm, w), lambda i: (i, 0))
    own_spec = lambda w: pl.BlockSpec((seqs, None, SEQ, w), lambda i: (ctx_row(i), l, 0, 0))
    own_shape = lambda w: jax.ShapeDtypeStruct((BATCH, DEPTH, SEQ, w), F32)
    any_spec = pl.BlockSpec(memory_space=pl.ANY)
    aliased = list(lat_base) + (list(prev_own) if prev_own is not None else [])
    n_in = 7
    aliases = {n_in: 3, n_in + 1: 5}
    if prev_own is not None:
        aliases.update({n_in + 2 + k: 6 + k for k in range(4)})
    return pl.pallas_call(
        functools.partial(_mixproj_kernel, l, prev_own is not None),
        grid=(N_TOK // tm,),
        in_specs=[
            row(D_MODEL),
            _layer_spec(_MOD_SHAPE, l),
            _layer_spec((3, D_MODEL), l),
            _layer_spec((D_MODEL, MIX_IN), l, pipeline_mode=pl.Buffered(1)),
            _const_spec((DEPTH, 256)),
            _const_spec((DEPTH, 128)),
            _const_spec(tables.shape),
        ] + [any_spec] * len(aliased),
        out_specs=[row(768),
                   pl.BlockSpec((512, tm), lambda i: (0, i)),
                   pl.BlockSpec((2, tm, 256), lambda i: (0, ctx_row(i), 0)),
                   pl.BlockSpec((None, 2, tm, 256), lambda i: (lat_tile(i)[0], 0, lat_tile(i)[1], 0)),
                   pl.BlockSpec((2, 256, tm), lambda i: (0, 0, ctx_row(i))),
                   pl.BlockSpec((None, 2, 256, tm), lambda i: (lat_tile(i)[0], 0, 0, lat_tile(i)[1])),
                   own_spec(256), own_spec(256), own_spec(128), own_spec(128)],
        out_shape=[
            jax.ShapeDtypeStruct((N_TOK, 768), F32),
            jax.ShapeDtypeStruct((512, N_TOK), BF16),
            jax.ShapeDtypeStruct((2, N_CTX, 256), BF16),
            jax.ShapeDtypeStruct((DEC_BATCH, 2, S_LAT, 256), BF16),
            jax.ShapeDtypeStruct((2, 256, N_CTX), BF16),
            jax.ShapeDtypeStruct((DEC_BATCH, 2, 256, S_LAT), BF16),
            own_shape(256), own_shape(256), own_shape(128), own_shape(128),
        ],
        scratch_shapes=[pltpu.VMEM((D_MODEL, MIX_IN), BF16)],
        input_output_aliases=aliases,
        compiler_params=pltpu.CompilerParams(
            dimension_semantics=("arbitrary",), vmem_limit_bytes=VMEM_LIMIT),
        name="mixproj",
    )(x, mod, norm_pre, w_mix_in, qg, kg, tables, *aliased)


N_UNITS = 12
ATTN_SLOTS = 3


def _attn_kernel(l, qt_ref, k_ref, vt_ref, lq1_ref, lk1_ref, lq2_ref, lk2_ref, subln_ref, *rest):
    n = ATTN_SLOTS
    o_ref, qs_ref = rest[-3 - 3 * n], rest[-2 - 3 * n]
    s_refs = rest[-1 - 3 * n:-1 - 2 * n]
    e_refs = rest[-1 - 2 * n:-1 - n]
    il_refs = rest[-1 - n:-1]
    oacc_ref = rest[-1]
    tq = qt_ref.shape[1]
    half = k_ref.shape[1] // 2
    zero = jnp.zeros((), BF16)

    feat = lax.broadcasted_iota(jnp.int32, (256, 1), 0)
    qb = qt_ref[0:256, :]
    for j in range(8):
        qs_ref[j] = jnp.where(feat // DK_DIFF == j, qb, zero)
    for hd in range(4):
        g = hd // 2
        qs_ref[8 + hd] = jnp.zeros((256, tq), BF16)
        qs_ref[8 + hd, HD_GQA * g:HD_GQA * (g + 1), :] = (
            qt_ref[256 + HD_GQA * hd:256 + HD_GQA * (hd + 1), :])

    def scores(u):
        plane = 0 if u < 8 else 1
        s_ref = s_refs[u % n]
        s_ref[0:half, :] = _dot(k_ref[plane, 0:half, :], qs_ref[u])
        s_ref[half:, :] = _dot(k_ref[plane, half:, :], qs_ref[u])

    def softmax(u):
        s = s_refs[u % n][...]
        e = jnp.exp2(s - jnp.max(s, axis=0, keepdims=True))
        il_refs[u % n][...] = 1.0 / jnp.sum(e, axis=0, keepdims=True)
        e_refs[u % n][...] = e.astype(BF16)

    def values(u):
        plane = 0 if u < 8 else 1
        vrow = DV_DIFF * (u // 2) if u < 8 else HD_GQA * ((u - 8) // 2)
        e_ref = e_refs[u % n]
        o = (_dot(vt_ref[plane, vrow:vrow + DV_DIFF, 0:half], e_ref[0:half, :])
             + _dot(vt_ref[plane, vrow:vrow + DV_DIFF, half:], e_ref[half:, :]))
        oacc_ref[u] = o * il_refs[u % n][...]

    for t in range(N_UNITS + 2):
        if t < N_UNITS:
            scores(t)
        if 1 <= t <= N_UNITS:
            softmax(t - 1)
        if t >= 2:
            values(t - 2)

    lam_init = 0.8 - 0.6 * math.exp(-0.3 * l)
    row = slice(l, l + 1)
    lam = (jnp.exp(jnp.sum(lq1_ref[row, :] * lk1_ref[row, :], axis=-1, keepdims=True))
           - jnp.exp(jnp.sum(lq2_ref[row, :] * lk2_ref[row, :], axis=-1, keepdims=True))
           + lam_init)
    heads = []
    for hd in range(4):
        o = oacc_ref[2 * hd] - lam * oacc_ref[2 * hd + 1]
        nrm = lax.rsqrt(jnp.mean(o * o, axis=0, keepdims=True) + EPS)
        heads.append(o * nrm * (subln_ref[:, l:l + 1] * (1.0 - lam_init)))
    heads += [oacc_ref[8 + hd] for hd in range(4)]
    o_ref[...] = jnp.concatenate(heads, axis=0).T.astype(BF16)


def _attn_param_specs():
    vec = _const_spec((DEPTH, DK_DIFF))
    return [vec, vec, vec, vec, _const_spec((DV_DIFF, DEPTH))]


def _attn_scratch(tq, s_len):
    n = ATTN_SLOTS
    return ([pltpu.VMEM((N_UNITS, 256, tq), BF16)]
            + [pltpu.VMEM((s_len, tq), F32)] * n
            + [pltpu.VMEM((s_len, tq), BF16)] * n
            + [pltpu.VMEM((1, tq), F32)] * n
            + [pltpu.VMEM((N_UNITS, DV_DIFF, tq), F32)])


def _attn_ctx(qt, k_ctx, vt_ctx, aparams, l):
    return pl.pallas_call(
        functools.partial(_attn_kernel, l),
        grid=(BATCH,),
        in_specs=[pl.BlockSpec((512, SEQ), lambda b: (0, b)),
                  pl.BlockSpec((2, SEQ, 256), lambda b: (0, b, 0)),
                  pl.BlockSpec((2, 256, SEQ), lambda b: (0, 0, b))] + _attn_param_specs(),
        out_specs=pl.BlockSpec((SEQ, 512), lambda b: (b, 0)),
        out_shape=jax.ShapeDtypeStruct((N_TOK, 512), BF16),
        scratch_shapes=_attn_scratch(SEQ, SEQ),
        compiler_params=pltpu.CompilerParams(
            dimension_semantics=("arbitrary",), vmem_limit_bytes=VMEM_LIMIT),
        name="attn_ctx",
    )(qt, k_ctx, vt_ctx, *aparams)


def _attn_lat(qt, k_lat, vt_lat, att, aparams, l):
    tq = TQ_LAT
    nq = DEC_SEQ // tq
    tile = lambda b, i: N_CTX // tq + b * nq + i
    return pl.pallas_call(
        functools.partial(_attn_kernel, l),
        grid=(DEC_BATCH, nq),
        in_specs=[pl.BlockSpec((512, tq), lambda b, i: (0, tile(b, i))),
                  pl.BlockSpec((None, 2, S_LAT, 256), lambda b, i: (b, 0, 0, 0)),
                  pl.BlockSpec((None, 2, 256, S_LAT), lambda b, i: (b, 0, 0, 0))]
        + _attn_param_specs() + [pl.BlockSpec(memory_space=pl.ANY)],
        out_specs=pl.BlockSpec((tq, 512), lambda b, i: (tile(b, i), 0)),
        out_shape=jax.ShapeDtypeStruct((N_TOK, 512), BF16),
        scratch_shapes=_attn_scratch(tq, S_LAT),
        input_output_aliases={8: 0},
        compiler_params=pltpu.CompilerParams(
            dimension_semantics=("arbitrary", "arbitrary"), vmem_limit_bytes=VMEM_LIMIT),
        name="attn_lat",
    )(qt, k_lat, vt_lat, *aparams, att)


def _conv_kernel(l, seq_len, cin_ref, wsc_ref, wdw_ref, bdw_ref, lng_ref, lnb_ref, pw_ref, *rest):
    o_ref, pad_ref = rest[-2], rest[-1]
    t_tile = cin_ref.shape[0]
    n_seq = t_tile // seq_len
    stride = seq_len + CONV_HALO
    chunks_per_seq = seq_len // R_CONV
    rows = R_CONV

    for q in range(n_seq + 1):
        pad_ref[q * stride:q * stride + CONV_HALO, :] = jnp.zeros((CONV_HALO, 512), F32)
    for q in range(n_seq):
        pad_ref[CONV_HALO + q * stride:CONV_HALO + q * stride + seq_len, :] = (
            cin_ref[q * seq_len:(q + 1) * seq_len, 256:768])

    pw = pw_ref[...].astype(BF16)
    bdw = bdw_ref[l:l + 1, :]
    lng = lng_ref[l:l + 1, :]
    lnb = lnb_ref[l:l + 1, :]

    def body(c, carry):
        q = c // chunks_per_seq
        r = c % chunks_per_seq
        base = pl.multiple_of(q * stride + r * rows, 8)
        row0 = pl.multiple_of(c * rows, 8)
        win = pad_ref[pl.ds(base, rows + 2 * CONV_HALO), :]
        win_s = win[:, 0:256]
        win_u = win[:, 256:512]

        s_acc = None
        for k in range(SCONV_K):
            off = CONV_HALO + k - SCONV_K // 2
            term = wsc_ref[pl.ds(k, 1), :] * win_s[off:off + rows]
            s_acc = term if s_acc is None else s_acc + term
        a_out = cin_ref[pl.ds(row0, rows), 0:256] * s_acc

        acc = None
        for res in range(8):
            part = None
            for a in range(-2, 2):
                k = 8 * a + res + CCM_K // 2
                if 0 <= k < CCM_K:
                    lo = CONV_HALO + 8 * a
                    term = wdw_ref[pl.ds(k, 1), :] * win_u[lo:lo + rows + 8]
                    part = term if part is None else part + term
            part = part[res:res + rows]
            acc = part if acc is None else acc + part
        u = acc + bdw
        mu = jnp.mean(u, axis=-1, keepdims=True)
        var = jnp.mean(jnp.square(u - mu), axis=-1, keepdims=True)
        y = (u - mu) * lax.rsqrt(var + EPS) * lng + lnb
        y = y * jax.nn.sigmoid(y)
        c_out = _dot(y.astype(BF16), pw)
        o_ref[pl.ds(row0, rows), 0:256] = a_out.astype(BF16)
        o_ref[pl.ds(row0, rows), 256:512] = c_out.astype(BF16)
        return carry

    lax.fori_loop(0, t_tile // rows, body, 0, unroll=2)


def _conv(cin, prev, seq_len, tile0, n_tiles, cparams, l):
    t = T_CONV
    n_seq = t // seq_len
    vec = _const_spec((DEPTH, 256))
    in_specs = [pl.BlockSpec((t, 768), lambda i: (tile0 + i, 0)),
                _layer_spec((SCONV_K, 256), l), _layer_spec((CCM_K, 256), l), vec, vec, vec,
                _layer_spec((256, 256), l)]
    args = [cin, *cparams]
    aliases = {}
    if prev is not None:
        in_specs.append(pl.BlockSpec(memory_space=pl.ANY))
        args.append(prev)
        aliases = {len(args) - 1: 0}
    return pl.pallas_call(
        functools.partial(_conv_kernel, l, seq_len),
        grid=(n_tiles,),
        in_specs=in_specs,
        out_specs=pl.BlockSpec((t, 512), lambda i: (tile0 + i, 0)),
        out_shape=jax.ShapeDtypeStruct((N_TOK, 512), BF16),
        scratch_shapes=[pltpu.VMEM((n_seq * (seq_len + CONV_HALO) + CONV_HALO, 512), F32)],
        input_output_aliases=aliases,
        compiler_params=pltpu.CompilerParams(
            dimension_semantics=("arbitrary",), vmem_limit_bytes=VMEM_LIMIT),
        name=f"conv{seq_len}",
    )(*args)


def _outproj_kernel(ac_ref, att_ref, x_ref, mod_ref, gpost_ref, w_ref, o_ref, wbf_ref):
    i = pl.program_id(0)

    @pl.when(i == 0)
    def _():
        wbf_ref[...] = w_ref[...].astype(BF16)

    y = (_dot(ac_ref[:, 0:256], wbf_ref[0:256, :]) + _dot(att_ref[:, 0:256], wbf_ref[256:512, :])
         + _dot(ac_ref[:, 256:512], wbf_ref[512:768, :])
         + _dot(att_ref[:, 256:512], wbf_ref[768:1024, :]))
    gate = _mod_row(mod_ref, 5, _cond_of_tile(i, TM_OUT))
    o_ref[...] = x_ref[...] + gate * _rms(y, gpost_ref[pl.ds(1, 1), :])


def _outproj(ac, att, x, mod, norm_post, w_mix_out, l):
    tm = TM_OUT
    row = lambda w: pl.BlockSpec((tm, w), lambda i: (i, 0))
    return pl.pallas_call(
        _outproj_kernel,
        grid=(N_TOK // tm,),
        in_specs=[row(512), row(512), row(D_MODEL),
                  _layer_spec(_MOD_SHAPE, l),
                  _layer_spec((3, D_MODEL), l),
                  _layer_spec((D_MODEL, D_MODEL), l, pipeline_mode=pl.Buffered(1))],
        out_specs=row(D_MODEL),
        out_shape=jax.ShapeDtypeStruct((N_TOK, D_MODEL), F32),
        scratch_shapes=[pltpu.VMEM((D_MODEL, D_MODEL), BF16)],
        compiler_params=pltpu.CompilerParams(
            dimension_semantics=("arbitrary",), vmem_limit_bytes=VMEM_LIMIT),
        name="outproj",
    )(ac, att, x, mod, norm_post, w_mix_out)


def _kv_lat_base(cache_diff_k, cache_diff_v, cache_gqa_k, cache_gqa_v, l):
    pad128 = lambda a: jnp.pad(a.reshape(DEC_BATCH, PAST_LEN, 128), ((0, 0), (0, 0), (0, 128)))
    keys = jnp.stack([cache_diff_k[:, l].reshape(DEC_BATCH, PAST_LEN, 256),
                      pad128(cache_gqa_k[:, l])], axis=1).astype(BF16)
    vals = jnp.stack([cache_diff_v[:, l].reshape(DEC_BATCH, PAST_LEN, 256),
                      pad128(cache_gqa_v[:, l])], axis=1).astype(BF16)
    vals_t = jnp.swapaxes(vals, 2, 3)
    return (jnp.pad(keys, ((0, 0), (0, 0), (DEC_SEQ, 0), (0, 0))),
            jnp.pad(vals_t, ((0, 0), (0, 0), (0, 0), (DEC_SEQ, 0))))


def kernel(x_prompt, x_sample, cache_diff_k, cache_diff_v, cache_gqa_k, cache_gqa_v, c, c_ctx, w_ada, b_ada, norm_pre, norm_post, ffn1_gate, ffn1_up, ffn1_down, ffn2_gate, ffn2_up, ffn2_down, w_mix_in, w_mix_out, sconv_w, diff_lq1, diff_lk1, diff_lq2, diff_lk2, diff_subln, ccm_dw_w, ccm_dw_b, ccm_ln_g, ccm_ln_b, ccm_pw, gqa_qnorm, gqa_knorm):
    conds = jnp.concatenate(
        [c_ctx[None, :], c, jnp.zeros((N_COND - 1 - DEC_BATCH, D_MODEL), F32)], axis=0)
    mod = _ada(conds, w_ada, b_ada)
    tables = jnp.asarray(_rope_tables())
    qg = jnp.tile(gqa_qnorm, (1, 4))
    kg = jnp.tile(gqa_knorm, (1, 2))
    aparams = (diff_lq1, diff_lk1, diff_lq2, diff_lk2, diff_subln.T)
    cparams = (sconv_w, ccm_dw_w, ccm_dw_b, ccm_ln_g, ccm_ln_b, ccm_pw)

    xs = (x_prompt.reshape(N_CTX, D_MODEL), x_sample.reshape(N_LAT, D_MODEL))
    own = None
    for l in range(DEPTH):
        (x,) = _ffn(xs, mod, norm_pre, norm_post, ffn1_gate, ffn1_up, ffn1_down, l, 0)

        base = _kv_lat_base(cache_diff_k, cache_diff_v, cache_gqa_k, cache_gqa_v, l)
        cin, qt, k_ctx, k_lat, vt_ctx, vt_lat, *own = _mixproj(
            x, mod, norm_pre, w_mix_in, qg, kg, tables, base, own, l)

        att = _attn_ctx(qt, k_ctx, vt_ctx, aparams, l)
        att = _attn_lat(qt, k_lat, vt_lat, att, aparams, l)

        ac = _conv(cin, None, SEQ, 0, N_CTX // T_CONV, cparams, l)
        ac = _conv(cin, ac, DEC_SEQ, N_CTX // T_CONV, N_LAT // T_CONV, cparams, l)

        x = _outproj(ac, att, x, mod, norm_post, w_mix_out, l)
        xs = _ffn((x,), mod, norm_pre, norm_post, ffn2_gate, ffn2_up, ffn2_down, l, 2,
                  split_out=(l == DEPTH - 1))

    y_prompt, y_sample = xs
    odk, odv, ogk, ogv = own
    return (y_prompt.reshape(BATCH, SEQ, D_MODEL),
            y_sample.reshape(DEC_BATCH, DEC_SEQ, D_MODEL),
            odk.reshape(BATCH, DEPTH, SEQ, 4, 2, DK_DIFF),
            odv.reshape(BATCH, DEPTH, SEQ, 4, DV_DIFF),
            ogk.reshape(BATCH, DEPTH, SEQ, 2, HD_GQA),
            ogv.reshape(BATCH, DEPTH, SEQ, 2, HD_GQA))
```

```python
import functools
import math

import numpy as np

import jax
import jax.numpy as jnp
from jax import lax
from jax.experimental import pallas as pl
from jax.experimental.pallas import tpu as pltpu

F32 = jnp.float32
BF16 = jnp.bfloat16

D_MODEL = 1024
BATCH = 16
SEQ = 256
DEPTH = 2
DEC_BATCH = 2
DEC_SEQ = 2048
PAST_LEN = 256
GRID_W = 64
GRID_H = DEC_SEQ // GRID_W
ROPE_THETA = 10000.0
EPS = 1e-6
N_MOD = 9
D_FF = 2816
DK_DIFF = 32
DV_DIFF = 64
HD_GQA = 64
CCM_K = 31
SCONV_K = 3
MIX_IN = 2560

N_CTX = BATCH * SEQ
N_LAT = DEC_BATCH * DEC_SEQ
N_TOK = N_CTX + N_LAT
N_COND = 8
S_LAT = DEC_SEQ + PAST_LEN

LOG2E = math.log2(math.e)

VMEM_LIMIT = 56 * 1024 * 1024

TM_FFN = 512
FFN_SUB = 256
FFN_CHUNK = 256
FFN_STAGE_SLOTS = 3
TM_MIX = 512
TQ_LAT = 256
T_CONV = 2048
R_CONV = 64
CONV_HALO = 16
TM_OUT = 1024


def _cond_of_tile(i, tm):
    n_ctx_tiles = N_CTX // tm
    per_b = DEC_SEQ // tm
    return jnp.where(i < n_ctx_tiles, 0, 1 + (i - n_ctx_tiles) // per_b)


def _mod_row(mod_ref, k, cond):
    return mod_ref[k, pl.ds(cond, 1), :]


def _dot(a, b):
    return jnp.dot(a, b, preferred_element_type=F32)


def _dot_nt(a, b):
    return lax.dot_general(a, b, (((1,), (1,)), ((), ())), preferred_element_type=F32)


def _rms(x, g):
    ms = jnp.mean(x * x, axis=-1, keepdims=True)
    return x * lax.rsqrt(ms + EPS) * g


def _group_mean(xx, width, group):
    r = lax.broadcasted_iota(jnp.int32, (width, width), 0) // group
    c = lax.broadcasted_iota(jnp.int32, (width, width), 1) // group
    gmat = jnp.where(r == c, 1.0 / group, 0.0).astype(BF16)
    hi = xx.astype(BF16)
    lo = (xx - hi.astype(F32)).astype(BF16)
    return _dot(hi, gmat) + _dot(lo, gmat)


def _const_spec(shape):
    return pl.BlockSpec(shape, lambda *_: (0,) * len(shape))


def _layer_spec(shape, l, **kw):
    return pl.BlockSpec((None,) + shape, lambda *_: (l,) + (0,) * len(shape), **kw)


_MOD_SHAPE = (N_MOD, N_COND, D_MODEL)


def _ada_kernel(c_ref, w_ref, b_ref, o_ref):
    cnd = c_ref[...]
    s = (cnd * jax.nn.sigmoid(cnd)).astype(BF16)
    o_ref[...] = _dot(s, w_ref[...].astype(BF16)) + b_ref[...]


def _ada(conds, w_ada, b_ada):
    return pl.pallas_call(
        _ada_kernel,
        grid=(DEPTH, N_MOD),
        in_specs=[
            pl.BlockSpec((N_COND, D_MODEL), lambda l, n: (0, 0)),
            pl.BlockSpec((None, D_MODEL, D_MODEL), lambda l, n: (l, 0, n)),
            pl.BlockSpec((None, None, 1, D_MODEL), lambda l, n: (l, n, 0, 0)),
        ],
        out_specs=pl.BlockSpec((None, None, N_COND, D_MODEL), lambda l, n: (l, n, 0, 0)),
        out_shape=jax.ShapeDtypeStruct((DEPTH,) + _MOD_SHAPE, F32),
        compiler_params=pltpu.CompilerParams(
            dimension_semantics=("arbitrary", "arbitrary"), vmem_limit_bytes=VMEM_LIMIT),
        name="ada",
    )(conds, w_ada, b_ada.reshape(DEPTH, N_MOD, 1, D_MODEL))


class _WeightStager:
    def __init__(self, jobs, stage_ref, sem_ref):
        self.jobs, self.stage_ref, self.depth = jobs, stage_ref, stage_ref.shape[0]
        self.copies = [
            pltpu.make_async_copy(src, stage_ref.at[k % self.depth], sem_ref.at[k % self.depth])
            for k, (src, _) in enumerate(jobs)]

    def prime(self):
        for k in range(min(self.depth - 1, len(self.jobs))):
            self.copies[k].start()

    def finish(self, k):
        if k + self.depth - 1 < len(self.jobs):
            self.copies[k + self.depth - 1].start()
        self.copies[k].wait()
        self.jobs[k][1](self.stage_ref[k % self.depth].astype(BF16))


def _ffn_kernel(l, sub, tm, n_x, n_o, *refs):
    x_refs = refs[:n_x]
    mod_ref, gpre_ref, gpost_ref, wg_hbm, wu_hbm, wd_hbm = refs[n_x:n_x + 6]
    o_refs = refs[n_x + 6:n_x + 6 + n_o]
    wg_ref, wu_ref, wd_ref = refs[n_x + 6 + n_o:]
    i = pl.program_id(0)
    is_ctx = i < N_CTX // tm
    cond = _cond_of_tile(i, tm)
    c = FFN_CHUNK
    n_chunks = D_FF // c

    if n_x == 1:
        x = x_refs[0][...]
    else:
        x = jnp.where(is_ctx, x_refs[0][...], x_refs[1][...])
    shift = _mod_row(mod_ref, 3 * sub, cond)
    scale = _mod_row(mod_ref, 3 * sub + 1, cond)
    gate = _mod_row(mod_ref, 3 * sub + 2, cond)
    gpost = gpost_ref[pl.ds(sub, 1), :]
    h = (_rms(x, gpre_ref[pl.ds(sub, 1), :]) * (1.0 + scale) + shift).astype(BF16)

    def swiglu(hh, cols):
        g = _dot(hh, wg_ref[:, cols])
        u = _dot(hh, wu_ref[:, cols])
        a = ((g * jax.nn.sigmoid(g)) * u).astype(BF16)
        return _dot(a, wd_ref[cols, :])

    def emit(y):
        if n_o == 1:
            o_refs[0][...] = y
        else:
            @pl.when(is_ctx)
            def _():
                o_refs[0][...] = y

            @pl.when(jnp.logical_not(is_ctx))
            def _():
                o_refs[1][...] = y

    @pl.when(i == 0)
    def _():
        def col_store(dst, j):
            def store(v):
                dst[:, j * c:(j + 1) * c] = v
            return store

        def row_store(j):
            def store(v):
                wd_ref[j * c:(j + 1) * c, :] = v
            return store

        up_jobs = [(src.at[l, :, pl.ds(j * c, c)], col_store(dst, j))
                   for j in range(n_chunks) for src, dst in ((wg_hbm, wg_ref), (wu_hbm, wu_ref))]
        down_jobs = [(wd_hbm.at[l, pl.ds(j * c, c), :], row_store(j)) for j in range(n_chunks)]

        def run(stage_up, stage_down, sem_up, sem_down):
            up = _WeightStager(up_jobs, stage_up, sem_up)
            down = _WeightStager(down_jobs, stage_down, sem_down)
            up.prime()
            down.prime()
            y = None
            for j in range(n_chunks):
                up.finish(2 * j)
                up.finish(2 * j + 1)
                down.finish(j)
                part = swiglu(h, slice(j * c, (j + 1) * c))
                y = part if y is None else y + part
            emit(x + 0.5 * (gate * _rms(y, gpost)))

        pl.run_scoped(run,
                      pltpu.VMEM((2 * FFN_STAGE_SLOTS, D_MODEL, c), F32),
                      pltpu.VMEM((FFN_STAGE_SLOTS, c, D_MODEL), F32),
                      pltpu.SemaphoreType.DMA((2 * FFN_STAGE_SLOTS,)),
                      pltpu.SemaphoreType.DMA((FFN_STAGE_SLOTS,)))

    @pl.when(i != 0)
    def _():
        outs = []
        for s in range(tm // FFN_SUB):
            rows = slice(s * FFN_SUB, (s + 1) * FFN_SUB)
            y = swiglu(h[rows], slice(None))
            outs.append(x[rows] + 0.5 * (gate * _rms(y, gpost)))
        emit(jnp.concatenate(outs, axis=0))


def _ffn(xs, mod, norm_pre, norm_post, wg, wu, wd, l, sub, split_out=False):
    tm = TM_FFN
    n_ctx_tiles = N_CTX // tm
    ctx_spec = pl.BlockSpec((tm, D_MODEL), lambda i: (jnp.minimum(i, n_ctx_tiles - 1), 0))
    lat_spec = pl.BlockSpec((tm, D_MODEL), lambda i: (jnp.maximum(i - n_ctx_tiles, 0), 0))
    full_spec = pl.BlockSpec((tm, D_MODEL), lambda i: (i, 0))
    x_specs = [full_spec] if len(xs) == 1 else [ctx_spec, lat_spec]
    if split_out:
        out_specs = [ctx_spec, lat_spec]
        out_shape = [jax.ShapeDtypeStruct((N_CTX, D_MODEL), F32),
                     jax.ShapeDtypeStruct((N_LAT, D_MODEL), F32)]
    else:
        out_specs = [full_spec]
        out_shape = [jax.ShapeDtypeStruct((N_TOK, D_MODEL), F32)]
    hbm = pl.BlockSpec(memory_space=pl.ANY)
    return pl.pallas_call(
        functools.partial(_ffn_kernel, l, sub, tm, len(xs), len(out_specs)),
        grid=(N_TOK // tm,),
        in_specs=x_specs + [
            _layer_spec(_MOD_SHAPE, l),
            _layer_spec((3, D_MODEL), l),
            _layer_spec((3, D_MODEL), l),
            hbm, hbm, hbm,
        ],
        out_specs=out_specs,
        out_shape=out_shape,
        scratch_shapes=[
            pltpu.VMEM((D_MODEL, D_FF), BF16), pltpu.VMEM((D_MODEL, D_FF), BF16),
            pltpu.VMEM((D_FF, D_MODEL), BF16),
        ],
        compiler_params=pltpu.CompilerParams(
            dimension_semantics=("arbitrary",), vmem_limit_bytes=VMEM_LIMIT),
        name=f"ffn{sub}",
    )(*xs, mod, norm_pre, norm_post, wg, wu, wd)


def _swap_blocks(x, blk):
    w = x.shape[-1]
    lane = lax.broadcasted_iota(jnp.int32, x.shape, x.ndim - 1)
    first = (lane % (2 * blk)) < blk
    return jnp.where(first, pltpu.roll(x, w - blk, x.ndim - 1), pltpu.roll(x, blk, x.ndim - 1))


def _rope_tables():
    out = []
    for rot_dim in (DK_DIFF, HD_GQA):
        n_freq = rot_dim // 4
        inv = ROPE_THETA ** (-np.arange(n_freq, dtype=np.float64) / n_freq)
        lane = np.arange(256) % rot_dim
        pos = lane % (rot_dim // 2)
        freq = inv[pos % n_freq]
        sign = np.where(pos < n_freq, -1.0, 1.0)
        coord = np.concatenate([np.arange(GRID_H), np.arange(GRID_W)]).astype(np.float64)
        ang = coord[:, None] * freq[None, :]
        out.append(np.cos(ang))
        out.append(np.sin(ang) * sign[None, :])
    return np.stack(out).astype(np.float32)


def _tile_rope(tab_ref, which, rot_dim, first_grid_row, n_grid_rows, identity):
    col_part = tab_ref[which, GRID_H:GRID_H + GRID_W, :]
    lane = lax.broadcasted_iota(jnp.int32, (1, 256), 1)
    on_row_axis = (lane % rot_dim) < rot_dim // 2
    tab = jnp.concatenate(
        [jnp.where(on_row_axis, tab_ref[which, pl.ds(first_grid_row + r, 1), :], col_part)
         for r in range(n_grid_rows)], axis=0)
    return jnp.where(identity, 1.0 - (which % 2), tab)


def _mixproj_kernel(l, has_prev, x_ref, mod_ref, gpre_ref, w_ref, qg_ref, kg_ref, tab_ref, *rest):
    n_alias = 6 if has_prev else 2
    (cin_ref, qt_ref, kc_ref, kl_ref, vtc_ref, vtl_ref, odk_ref, odv_ref, ogk_ref, ogv_ref,
     wbf_ref) = rest[n_alias:]
    tm = TM_MIX
    i = pl.program_id(0)
    n_ctx_tiles = N_CTX // tm
    is_ctx = i < n_ctx_tiles
    cond = _cond_of_tile(i, tm)

    @pl.when(i == 0)
    def _():
        wbf_ref[...] = w_ref[...].astype(BF16)

    shift = _mod_row(mod_ref, 3, cond)
    scale = _mod_row(mod_ref, 4, cond)
    h = (_rms(x_ref[...], gpre_ref[pl.ds(1, 1), :]) * (1.0 + scale) + shift).astype(BF16)

    a = _dot(h, wbf_ref[:, 0:768])
    cin_ref[:, 0:256] = a[:, 0:256]
    cin_ref[:, 256:512] = a[:, 256:512] * a[:, 512:768]
    c = _dot(h, wbf_ref[:, 1536:2048])
    cin_ref[:, 512:768] = c[:, 0:256] * jax.nn.sigmoid(c[:, 256:512])

    grid_rows = tm // GRID_W
    first_row = (jnp.maximum(i - n_ctx_tiles, 0) % (DEC_SEQ // tm)) * grid_rows
    cd = _tile_rope(tab_ref, 0, DK_DIFF, first_row, grid_rows, is_ctx)
    sd = _tile_rope(tab_ref, 1, DK_DIFF, first_row, grid_rows, is_ctx)
    cg = _tile_rope(tab_ref, 2, HD_GQA, first_row, grid_rows, is_ctx)
    sg = _tile_rope(tab_ref, 3, HD_GQA, first_row, grid_rows, is_ctx)

    b = _dot(h, wbf_ref[:, 768:1536])
    bq, bk, bv = b[:, 0:256], b[:, 256:512], b[:, 512:768]
    bq_r = (bq * cd + _swap_blocks(bq, DK_DIFF // 4) * sd) * (DK_DIFF ** -0.5 * LOG2E)
    bk_r = bk * cd + _swap_blocks(bk, DK_DIFF // 4) * sd
    qt_ref[0:256, :] = bq_r.T.astype(BF16)

    d = _dot(h, wbf_ref[:, 2048:2560])
    dq, dk, dv = d[:, 0:256], d[:, 256:384], d[:, 384:512]
    dq_n = dq * lax.rsqrt(_group_mean(dq * dq, 256, HD_GQA) + EPS) * qg_ref[l:l + 1, :]
    dk_n = dk * lax.rsqrt(_group_mean(dk * dk, 128, HD_GQA) + EPS) * kg_ref[l:l + 1, :]
    dq_r = (dq_n * cg + _swap_blocks(dq_n, HD_GQA // 4) * sg) * (HD_GQA ** -0.5 * LOG2E)
    dk_r = dk_n * cg[:, 0:128] + _swap_blocks(dk_n, HD_GQA // 4) * sg[:, 0:128]
    qt_ref[256:512, :] = dq_r.T.astype(BF16)

    bk_b = bk_r.astype(BF16)
    dk_b = jnp.concatenate([dk_r, jnp.zeros_like(dk_r)], axis=1).astype(BF16)
    bv_t = bv.T.astype(BF16)
    dv_t = jnp.concatenate([dv, jnp.zeros_like(dv)], axis=1).T.astype(BF16)

    @pl.when(is_ctx)
    def _():
        kc_ref[0] = bk_b
        kc_ref[1] = dk_b
        vtc_ref[0] = bv_t
        vtc_ref[1] = dv_t
        seqs = tm // SEQ
        odk_ref[...] = bk.reshape(seqs, SEQ, 256)
        odv_ref[...] = bv.reshape(seqs, SEQ, 256)
        ogk_ref[...] = dk_n.reshape(seqs, SEQ, 128)
        ogv_ref[...] = dv.reshape(seqs, SEQ, 128)

    @pl.when(jnp.logical_not(is_ctx))
    def _():
        kl_ref[0] = bk_b
        kl_ref[1] = dk_b
        vtl_ref[0] = bv_t
        vtl_ref[1] = dv_t


def _mixproj(x, mod, norm_pre, w_mix_in, qg, kg, tables, lat_base, prev_own, l):
    tm = TM_MIX
    n_ctx_tiles = N_CTX // tm
    per_b = DEC_SEQ // tm
    seqs = tm // SEQ

    def lat_tile(i):
        t = jnp.maximum(i - n_ctx_tiles, 0)
        return t // per_b, t % per_b

    ctx_row = lambda i: jnp.minimum(i, n_ctx_tiles - 1)
    row = lambda w: pl.BlockSpec((tm, w), lambda i: (i, 0))
    own_spec = lambda w: pl.BlockSpec((seqs, None, SEQ, w), lambda i: (ctx_row(i), l, 0, 0))
    own_shape = lambda w: jax.ShapeDtypeStruct((BATCH, DEPTH, SEQ, w), F32)
    any_spec = pl.BlockSpec(memory_space=pl.ANY)
    aliased = list(lat_base) + (list(prev_own) if prev_own is not None else [])
    n_in = 7
    aliases = {n_in: 3, n_in + 1: 5}
    if prev_own is not None:
        aliases.update({n_in + 2 + k: 6 + k for k in range(4)})
    return pl.pallas_call(
        functools.partial(_mixproj_kernel, l, prev_own is not None),
        grid=(N_TOK // tm,),
        in_specs=[
            row(D_MODEL),
            _layer_spec(_MOD_SHAPE, l),
            _layer_spec((3, D_MODEL), l),
            _layer_spec((D_MODEL, MIX_IN), l, pipeline_mode=pl.Buffered(1)),
            _const_spec((DEPTH, 256)),
            _const_spec((DEPTH, 128)),
            _const_spec(tables.shape),
        ] + [any_spec] * len(aliased),
        out_specs=[row(768),
                   pl.BlockSpec((512, tm), lambda i: (0, i)),
                   pl.BlockSpec((2, tm, 256), lambda i: (0, ctx_row(i), 0)),
                   pl.BlockSpec((None, 2, tm, 256), lambda i: (lat_tile(i)[0], 0, lat_tile(i)[1], 0)),
                   pl.BlockSpec((2, 256, tm), lambda i: (0, 0, ctx_row(i))),
                   pl.BlockSpec((None, 2, 256, tm), lambda i: (lat_tile(i)[0], 0, 0, lat_tile(i)[1])),
                   own_spec(256), own_spec(256), own_spec(128), own_spec(128)],
        out_shape=[
            jax.ShapeDtypeStruct((N_TOK, 768), F32),
            jax.ShapeDtypeStruct((512, N_TOK), BF16),
            jax.ShapeDtypeStruct((2, N_CTX, 256), BF16),
            jax.ShapeDtypeStruct((DEC_BATCH, 2, S_LAT, 256), BF16),
            jax.ShapeDtypeStruct((2, 256, N_CTX), BF16),
            jax.ShapeDtypeStruct((DEC_BATCH, 2, 256, S_LAT), BF16),
            own_shape(256), own_shape(256), own_shape(128), own_shape(128),
        ],
        scratch_shapes=[pltpu.VMEM((D_MODEL, MIX_IN), BF16)],
        input_output_aliases=aliases,
        compiler_params=pltpu.CompilerParams(
            dimension_semantics=("arbitrary",), vmem_limit_bytes=VMEM_LIMIT),
        name="mixproj",
    )(x, mod, norm_pre, w_mix_in, qg, kg, tables, *aliased)


N_UNITS = 12
ATTN_SLOTS = 3


def _attn_kernel(l, qt_ref, k_ref, vt_ref, lq1_ref, lk1_ref, lq2_ref, lk2_ref, subln_ref, *rest):
    n = ATTN_SLOTS
    o_ref, qs_ref = rest[-3 - 3 * n], rest[-2 - 3 * n]
    s_refs = rest[-1 - 3 * n:-1 - 2 * n]
    e_refs = rest[-1 - 2 * n:-1 - n]
    il_refs = rest[-1 - n:-1]
    oacc_ref = rest[-1]
    tq = qt_ref.shape[1]
    half = k_ref.shape[1] // 2
    zero = jnp.zeros((), BF16)

    feat = lax.broadcasted_iota(jnp.int32, (256, 1), 0)
    qb = qt_ref[0:256, :]
    for j in range(8):
        qs_ref[j] = jnp.where(feat // DK_DIFF == j, qb, zero)
    for hd in range(4):
        g = hd // 2
        qs_ref[8 + hd] = jnp.zeros((256, tq), BF16)
        qs_ref[8 + hd, HD_GQA * g:HD_GQA * (g + 1), :] = (
            qt_ref[256 + HD_GQA * hd:256 + HD_GQA * (hd + 1), :])

    def scores(u):
        plane = 0 if u < 8 else 1
        s_ref = s_refs[u % n]
        s_ref[0:half, :] = _dot(k_ref[plane, 0:half, :], qs_ref[u])
        s_ref[half:, :] = _dot(k_ref[plane, half:, :], qs_ref[u])

    def softmax(u):
        s = s_refs[u % n][...]
        e = jnp.exp2(s - jnp.max(s, axis=0, keepdims=True))
        il_refs[u % n][...] = 1.0 / jnp.sum(e, axis=0, keepdims=True)
        e_refs[u % n][...] = e.astype(BF16)

    def values(u):
        plane = 0 if u < 8 else 1
        vrow = DV_DIFF * (u // 2) if u < 8 else HD_GQA * ((u - 8) // 2)
        e_ref = e_refs[u % n]
        o = (_dot(vt_ref[plane, vrow:vrow + DV_DIFF, 0:half], e_ref[0:half, :])
             + _dot(vt_ref[plane, vrow:vrow + DV_DIFF, half:], e_ref[half:, :]))
        oacc_ref[u] = o * il_refs[u % n][...]

    for t in range(N_UNITS + 2):
        if t < N_UNITS:
            scores(t)
        if 1 <= t <= N_UNITS:
            softmax(t - 1)
        if t >= 2:
            values(t - 2)

    lam_init = 0.8 - 0.6 * math.exp(-0.3 * l)
    row = slice(l, l + 1)
    lam = (jnp.exp(jnp.sum(lq1_ref[row, :] * lk1_ref[row, :], axis=-1, keepdims=True))
           - jnp.exp(jnp.sum(lq2_ref[row, :] * lk2_ref[row, :], axis=-1, keepdims=True))
           + lam_init)
    heads = []
    for hd in range(4):
        o = oacc_ref[2 * hd] - lam * oacc_ref[2 * hd + 1]
        nrm = lax.rsqrt(jnp.mean(o * o, axis=0, keepdims=True) + EPS)
        heads.append(o * nrm * (subln_ref[:, l:l + 1] * (1.0 - lam_init)))
    heads += [oacc_ref[8 + hd] for hd in range(4)]
    o_ref[...] = jnp.concatenate(heads, axis=0).T.astype(BF16)


def _attn_param_specs():
    vec = _const_spec((DEPTH, DK_DIFF))
    return [vec, vec, vec, vec, _const_spec((DV_DIFF, DEPTH))]


def _attn_scratch(tq, s_len):
    n = ATTN_SLOTS
    return ([pltpu.VMEM((N_UNITS, 256, tq), BF16)]
            + [pltpu.VMEM((s_len, tq), F32)] * n
            + [pltpu.VMEM((s_len, tq), BF16)] * n
            + [pltpu.VMEM((1, tq), F32)] * n
            + [pltpu.VMEM((N_UNITS, DV_DIFF, tq), F32)])


def _attn_ctx(qt, k_ctx, vt_ctx, aparams, l):
    return pl.pallas_call(
        functools.partial(_attn_kernel, l),
        grid=(BATCH,),
        in_specs=[pl.BlockSpec((512, SEQ), lambda b: (0, b)),
                  pl.BlockSpec((2, SEQ, 256), lambda b: (0, b, 0)),
                  pl.BlockSpec((2, 256, SEQ), lambda b: (0, 0, b))] + _attn_param_specs(),
        out_specs=pl.BlockSpec((SEQ, 512), lambda b: (b, 0)),
        out_shape=jax.ShapeDtypeStruct((N_TOK, 512), BF16),
        scratch_shapes=_attn_scratch(SEQ, SEQ),
        compiler_params=pltpu.CompilerParams(
            dimension_semantics=("arbitrary",), vmem_limit_bytes=VMEM_LIMIT),
        name="attn_ctx",
    )(qt, k_ctx, vt_ctx, *aparams)


def _attn_lat(qt, k_lat, vt_lat, att, aparams, l):
    tq = TQ_LAT
    nq = DEC_SEQ // tq
    tile = lambda b, i: N_CTX // tq + b * nq + i
    return pl.pallas_call(
        functools.partial(_attn_kernel, l),
        grid=(DEC_BATCH, nq),
        in_specs=[pl.BlockSpec((512, tq), lambda b, i: (0, tile(b, i))),
                  pl.BlockSpec((None, 2, S_LAT, 256), lambda b, i: (b, 0, 0, 0)),
                  pl.BlockSpec((None, 2, 256, S_LAT), lambda b, i: (b, 0, 0, 0))]
        + _attn_param_specs() + [pl.BlockSpec(memory_space=pl.ANY)],
        out_specs=pl.BlockSpec((tq, 512), lambda b, i: (tile(b, i), 0)),
        out_shape=jax.ShapeDtypeStruct((N_TOK, 512), BF16),
        scratch_shapes=_attn_scratch(tq, S_LAT),
        input_output_aliases={8: 0},
        compiler_params=pltpu.CompilerParams(
            dimension_semantics=("arbitrary", "arbitrary"), vmem_limit_bytes=VMEM_LIMIT),
        name="attn_lat",
    )(qt, k_lat, vt_lat, *aparams, att)


def _conv_kernel(l, seq_len, cin_ref, wsc_ref, wdw_ref, bdw_ref, lng_ref, lnb_ref, pw_ref, *rest):
    o_ref, pad_ref = rest[-2], rest[-1]
    t_tile = cin_ref.shape[0]
    n_seq = t_tile // seq_len
    stride = seq_len + CONV_HALO
    chunks_per_seq = seq_len // R_CONV
    rows = R_CONV

    for q in range(n_seq + 1):
        pad_ref[q * stride:q * stride + CONV_HALO, :] = jnp.zeros((CONV_HALO, 512), F32)
    for q in range(n_seq):
        pad_ref[CONV_HALO + q * stride:CONV_HALO + q * stride + seq_len, :] = (
            cin_ref[q * seq_len:(q + 1) * seq_len, 256:768])

    pw = pw_ref[...].astype(BF16)
    bdw = bdw_ref[l:l + 1, :]
    lng = lng_ref[l:l + 1, :]
    lnb = lnb_ref[l:l + 1, :]

    def body(c, carry):
        q = c // chunks_per_seq
        r = c % chunks_per_seq
        base = pl.multiple_of(q * stride + r * rows, 8)
        row0 = pl.multiple_of(c * rows, 8)
        win = pad_ref[pl.ds(base, rows + 2 * CONV_HALO), :]
        win_s = win[:, 0:256]
        win_u = win[:, 256:512]

        s_acc = None
        for k in range(SCONV_K):
            off = CONV_HALO + k - SCONV_K // 2
            term = wsc_ref[pl.ds(k, 1), :] * win_s[off:off + rows]
            s_acc = term if s_acc is None else s_acc + term
        a_out = cin_ref[pl.ds(row0, rows), 0:256] * s_acc

        acc = None
        for res in range(8):
            part = None
            for a in range(-2, 2):
                k = 8 * a + res + CCM_K // 2
                if 0 <= k < CCM_K:
                    lo = CONV_HALO + 8 * a
                    term = wdw_ref[pl.ds(k, 1), :] * win_u[lo:lo + rows + 8]
                    part = term if part is None else part + term
            part = part[res:res + rows]
            acc = part if acc is None else acc + part
        u = acc + bdw
        mu = jnp.mean(u, axis=-1, keepdims=True)
        var = jnp.mean(jnp.square(u - mu), axis=-1, keepdims=True)
        y = (u - mu) * lax.rsqrt(var + EPS) * lng + lnb
        y = y * jax.nn.sigmoid(y)
        c_out = _dot(y.astype(BF16), pw)
        o_ref[pl.ds(row0, rows), 0:256] = a_out.astype(BF16)
        o_ref[pl.ds(row0, rows), 256:512] = c_out.astype(BF16)
        return carry

    lax.fori_loop(0, t_tile // rows, body, 0, unroll=2)


def _conv(cin, prev, seq_len, tile0, n_tiles, cparams, l):
    t = T_CONV
    n_seq = t // seq_len
    vec = _const_spec((DEPTH, 256))
    in_specs = [pl.BlockSpec((t, 768), lambda i: (tile0 + i, 0)),
                _layer_spec((SCONV_K, 256), l), _layer_spec((CCM_K, 256), l), vec, vec, vec,
                _layer_spec((256, 256), l)]
    args = [cin, *cparams]
    aliases = {}
    if prev is not None:
        in_specs.append(pl.BlockSpec(memory_space=pl.ANY))
        args.append(prev)
        aliases = {len(args) - 1: 0}
    return pl.pallas_call(
        functools.partial(_conv_kernel, l, seq_len),
        grid=(n_tiles,),
        in_specs=in_specs,
        out_specs=pl.BlockSpec((t, 512), lambda i: (tile0 + i, 0)),
        out_shape=jax.ShapeDtypeStruct((N_TOK, 512), BF16),
        scratch_shapes=[pltpu.VMEM((n_seq * (seq_len + CONV_HALO) + CONV_HALO, 512), F32)],
        input_output_aliases=aliases,
        compiler_params=pltpu.CompilerParams(
            dimension_semantics=("arbitrary",), vmem_limit_bytes=VMEM_LIMIT),
        name=f"conv{seq_len}",
    )(*args)


def _outproj_kernel(ac_ref, att_ref, x_ref, mod_ref, gpost_ref, w_ref, o_ref, wbf_ref):
    i = pl.program_id(0)

    @pl.when(i == 0)
    def _():
        wbf_ref[...] = w_ref[...].astype(BF16)

    y = (_dot(ac_ref[:, 0:256], wbf_ref[0:256, :]) + _dot(att_ref[:, 0:256], wbf_ref[256:512, :])
         + _dot(ac_ref[:, 256:512], wbf_ref[512:768, :])
         + _dot(att_ref[:, 256:512], wbf_ref[768:1024, :]))
    gate = _mod_row(mod_ref, 5, _cond_of_tile(i, TM_OUT))
    o_ref[...] = x_ref[...] + gate * _rms(y, gpost_ref[pl.ds(1, 1), :])


def _outproj(ac, att, x, mod, norm_post, w_mix_out, l):
    tm = TM_OUT
    row = lambda w: pl.BlockSpec((tm, w), lambda i: (i, 0))
    return pl.pallas_call(
        _outproj_kernel,
        grid=(N_TOK // tm,),
        in_specs=[row(512), row(512), row(D_MODEL),
                  _layer_spec(_MOD_SHAPE, l),
                  _layer_spec((3, D_MODEL), l),
                  _layer_spec((D_MODEL, D_MODEL), l, pipeline_mode=pl.Buffered(1))],
        out_specs=row(D_MODEL),
        out_shape=jax.ShapeDtypeStruct((N_TOK, D_MODEL), F32),
        scratch_shapes=[pltpu.VMEM((D_MODEL, D_MODEL), BF16)],
        compiler_params=pltpu.CompilerParams(
            dimension_semantics=("arbitrary",), vmem_limit_bytes=VMEM_LIMIT),
        name="outproj",
    )(ac, att, x, mod, norm_post, w_mix_out)


def _kv_lat_base(cache_diff_k, cache_diff_v, cache_gqa_k, cache_gqa_v, l):
    pad128 = lambda a: jnp.pad(a.reshape(DEC_BATCH, PAST_LEN, 128), ((0, 0), (0, 0), (0, 128)))
    keys = jnp.stack([cache_diff_k[:, l].reshape(DEC_BATCH, PAST_LEN, 256),
                      pad128(cache_gqa_k[:, l])], axis=1).astype(BF16)
    vals = jnp.stack([cache_diff_v[:, l].reshape(DEC_BATCH, PAST_LEN, 256),
                      pad128(cache_gqa_v[:, l])], axis=1).astype(BF16)
    vals_t = jnp.swapaxes(vals, 2, 3)
    return (jnp.pad(keys, ((0, 0), (0, 0), (DEC_SEQ, 0), (0, 0))),
            jnp.pad(vals_t, ((0, 0), (0, 0), (0, 0), (DEC_SEQ, 0))))


def kernel(x_prompt, x_sample, cache_diff_k, cache_diff_v, cache_gqa_k, cache_gqa_v, c, c_ctx, w_ada, b_ada, norm_pre, norm_post, ffn1_gate, ffn1_up, ffn1_down, ffn2_gate, ffn2_up, ffn2_down, w_mix_in, w_mix_out, sconv_w, diff_lq1, diff_lk1, diff_lq2, diff_lk2, diff_subln, ccm_dw_w, ccm_dw_b, ccm_ln_g, ccm_ln_b, ccm_pw, gqa_qnorm, gqa_knorm):
    conds = jnp.concatenate(
        [c_ctx[None, :], c, jnp.zeros((N_COND - 1 - DEC_BATCH, D_MODEL), F32)], axis=0)
    mod = _ada(conds, w_ada, b_ada)
    tables = jnp.asarray(_rope_tables())
    qg = jnp.tile(gqa_qnorm, (1, 4))
    kg = jnp.tile(gqa_knorm, (1, 2))
    aparams = (diff_lq1, diff_lk1, diff_lq2, diff_lk2, diff_subln.T)
    cparams = (sconv_w, ccm_dw_w, ccm_dw_b, ccm_ln_g, ccm_ln_b, ccm_pw)

    xs = (x_prompt.reshape(N_CTX, D_MODEL), x_sample.reshape(N_LAT, D_MODEL))
    own = None
    for l in range(DEPTH):
        (x,) = _ffn(xs, mod, norm_pre, norm_post, ffn1_gate, ffn1_up, ffn1_down, l, 0)

        base = _kv_lat_base(cache_diff_k, cache_diff_v, cache_gqa_k, cache_gqa_v, l)
        cin, qt, k_ctx, k_lat, vt_ctx, vt_lat, *own = _mixproj(
            x, mod, norm_pre, w_mix_in, qg, kg, tables, base, own, l)

        att = _attn_ctx(qt, k_ctx, vt_ctx, aparams, l)
        att = _attn_lat(qt, k_lat, vt_lat, att, aparams, l)

        ac = _conv(cin, None, SEQ, 0, N_CTX // T_CONV, cparams, l)
        ac = _conv(cin, ac, DEC_SEQ, N_CTX // T_CONV, N_LAT // T_CONV, cparams, l)

        x = _outproj(ac, att, x, mod, norm_post, w_mix_out, l)
        xs = _ffn((x,), mod, norm_pre, norm_post, ffn2_gate, ffn2_up, ffn2_down, l, 2,
                  split_out=(l == DEPTH - 1))

    y_prompt, y_sample = xs
    odk, odv, ogk, ogv = own
    return (y_prompt.reshape(BATCH, SEQ, D_MODEL),
            y_sample.reshape(DEC_BATCH, DEC_SEQ, D_MODEL),
            odk.reshape(BATCH, DEPTH, SEQ, 4, 2, DK_DIFF),
            odv.reshape(BATCH, DEPTH, SEQ, 4, DV_DIFF),
            ogk.reshape(BATCH, DEPTH, SEQ, 2, HD_GQA),
            ogv.reshape(BATCH, DEPTH, SEQ, 2, HD_GQA))
```

```python
import functools
import math

import numpy as np

import jax
import jax.numpy as jnp
from jax import lax
from jax.experimental import pallas as pl
from jax.experimental.pallas import tpu as pltpu

F32 = jnp.float32
BF16 = jnp.bfloat16

D_MODEL = 1024
BATCH = 16
SEQ = 256
DEPTH = 2
DEC_BATCH = 2
DEC_SEQ = 2048
PAST_LEN = 256
GRID_W = 64
GRID_H = DEC_SEQ // GRID_W
ROPE_THETA = 10000.0
EPS = 1e-6
N_MOD = 9
D_FF = 2816
DK_DIFF = 32
DV_DIFF = 64
HD_GQA = 64
CCM_K = 31
SCONV_K = 3
MIX_IN = 2560

N_CTX = BATCH * SEQ
N_LAT = DEC_BATCH * DEC_SEQ
N_TOK = N_CTX + N_LAT
N_COND = 8
S_LAT = DEC_SEQ + PAST_LEN

LOG2E = math.log2(math.e)

VMEM_LIMIT = 56 * 1024 * 1024

TM_FFN = 512
FFN_SUB = 256
FFN_CHUNK = 256
FFN_STAGE_SLOTS = 3
TM_MIX = 1024
TQ_LAT = 256
T_CONV = 2048
R_CONV = 64
CONV_HALO = 16


def _cond_of_tile(i, tm):
    n_ctx_tiles = N_CTX // tm
    per_b = DEC_SEQ // tm
    return jnp.where(i < n_ctx_tiles, 0, 1 + (i - n_ctx_tiles) // per_b)


def _mod_row(mod_ref, k, cond):
    return mod_ref[k, pl.ds(cond, 1), :]


def _dot(a, b):
    return jnp.dot(a, b, preferred_element_type=F32)


def _dot_nt(a, b):
    return lax.dot_general(a, b, (((1,), (1,)), ((), ())), preferred_element_type=F32)


def _rms(x, g):
    ms = jnp.mean(x * x, axis=-1, keepdims=True)
    return x * lax.rsqrt(ms + EPS) * g


def _group_mean(xx, width, group):
    r = lax.broadcasted_iota(jnp.int32, (width, width), 0) // group
    c = lax.broadcasted_iota(jnp.int32, (width, width), 1) // group
    gmat = jnp.where(r == c, 1.0 / group, 0.0).astype(BF16)
    hi = xx.astype(BF16)
    lo = (xx - hi.astype(F32)).astype(BF16)
    return _dot(hi, gmat) + _dot(lo, gmat)


def _const_spec(shape):
    return pl.BlockSpec(shape, lambda *_: (0,) * len(shape))


def _layer_spec(shape, l, **kw):
    return pl.BlockSpec((None,) + shape, lambda *_: (l,) + (0,) * len(shape), **kw)


_MOD_SHAPE = (N_MOD, N_COND, D_MODEL)


def _ada_kernel(c_ref, w_ref, b_ref, o_ref):
    cnd = c_ref[...]
    s = (cnd * jax.nn.sigmoid(cnd)).astype(BF16)
    o_ref[...] = _dot(s, w_ref[...].astype(BF16)) + b_ref[...]


def _ada(conds, w_ada, b_ada):
    return pl.pallas_call(
        _ada_kernel,
        grid=(DEPTH, N_MOD),
        in_specs=[
            pl.BlockSpec((N_COND, D_MODEL), lambda l, n: (0, 0)),
            pl.BlockSpec((None, D_MODEL, D_MODEL), lambda l, n: (l, 0, n)),
            pl.BlockSpec((None, None, 1, D_MODEL), lambda l, n: (l, n, 0, 0)),
        ],
        out_specs=pl.BlockSpec((None, None, N_COND, D_MODEL), lambda l, n: (l, n, 0, 0)),
        out_shape=jax.ShapeDtypeStruct((DEPTH,) + _MOD_SHAPE, F32),
        compiler_params=pltpu.CompilerParams(
            dimension_semantics=("arbitrary", "arbitrary"), vmem_limit_bytes=VMEM_LIMIT),
        name="ada",
    )(conds, w_ada, b_ada.reshape(DEPTH, N_MOD, 1, D_MODEL))


class _WeightStager:
    def __init__(self, jobs, stage_ref, sem_ref):
        self.jobs, self.stage_ref, self.depth = jobs, stage_ref, stage_ref.shape[0]
        self.copies = [
            pltpu.make_async_copy(src, stage_ref.at[k % self.depth], sem_ref.at[k % self.depth])
            for k, (src, _) in enumerate(jobs)]

    def prime(self):
        for k in range(min(self.depth - 1, len(self.jobs))):
            self.copies[k].start()

    def finish(self, k):
        if k + self.depth - 1 < len(self.jobs):
            self.copies[k + self.depth - 1].start()
        self.copies[k].wait()
        self.jobs[k][1](self.stage_ref[k % self.depth].astype(BF16))


def _ffn_kernel(l, sub, tm, n_x, n_o, mix, *refs):
    refs = list(refs)
    x_refs = [refs.pop(0) for _ in range(n_x)]
    ac_ref, att_ref = (refs.pop(0), refs.pop(0)) if mix else (None, None)
    mod_ref, gpre_ref, gpost_ref, wg_hbm, wu_hbm, wd_hbm = [refs.pop(0) for _ in range(6)]
    wo_hbm = refs.pop(0) if mix else None
    o_refs = [refs.pop(0) for _ in range(n_o)]
    wg_ref, wu_ref, wd_ref = [refs.pop(0) for _ in range(3)]
    wo_ref = refs.pop(0) if mix else None
    i = pl.program_id(0)
    is_ctx = i < N_CTX // tm
    cond = _cond_of_tile(i, tm)
    c = FFN_CHUNK
    n_chunks = D_FF // c
    gate = _mod_row(mod_ref, 3 * sub + 2, cond)
    gpost = gpost_ref[pl.ds(sub, 1), :]

    def prep():
        if n_x == 1:
            x = x_refs[0][...]
        else:
            x = jnp.where(is_ctx, x_refs[0][...], x_refs[1][...])
        if mix:
            ym = (_dot(ac_ref[:, 0:256], wo_ref[0:256, :]) + _dot(att_ref[:, 0:256], wo_ref[256:512, :])
                  + _dot(ac_ref[:, 256:512], wo_ref[512:768, :])
                  + _dot(att_ref[:, 256:512], wo_ref[768:1024, :]))
            x = x + _mod_row(mod_ref, 5, cond) * _rms(ym, gpost_ref[pl.ds(1, 1), :])
        shift = _mod_row(mod_ref, 3 * sub, cond)
        scale = _mod_row(mod_ref, 3 * sub + 1, cond)
        h = (_rms(x, gpre_ref[pl.ds(sub, 1), :]) * (1.0 + scale) + shift).astype(BF16)
        return x, h

    def swiglu(hh, cols):
        g = _dot(hh, wg_ref[:, cols])
        u = _dot(hh, wu_ref[:, cols])
        a = ((g * jax.nn.sigmoid(g)) * u).astype(BF16)
        return _dot(a, wd_ref[cols, :])

    def emit(y):
        if n_o == 1:
            o_refs[0][...] = y
        else:
            @pl.when(is_ctx)
            def _():
                o_refs[0][...] = y

            @pl.when(jnp.logical_not(is_ctx))
            def _():
                o_refs[1][...] = y

    @pl.when(i == 0)
    def _():
        def col_store(dst, j):
            def store(v):
                dst[:, j * c:(j + 1) * c] = v
            return store

        def row_store(j):
            def store(v):
                wd_ref[j * c:(j + 1) * c, :] = v
            return store

        up_jobs = [(src.at[l, :, pl.ds(j * c, c)], col_store(dst, j))
                   for j in range(n_chunks) for src, dst in ((wg_hbm, wg_ref), (wu_hbm, wu_ref))]
        down_jobs = [(wd_hbm.at[l, pl.ds(j * c, c), :], row_store(j)) for j in range(n_chunks)]

        def run(stage_up, stage_down, sem_up, sem_down):
            up = _WeightStager(up_jobs, stage_up, sem_up)
            up.prime()
            if mix:
                def wo_store(j):
                    def store(v):
                        wo_ref[j * c:(j + 1) * c, :] = v
                    return store

                wo = _WeightStager(
                    [(wo_hbm.at[l, pl.ds(j * c, c), :], wo_store(j)) for j in range(D_MODEL // c)],
                    stage_down, sem_down)
                wo.prime()
                for j in range(D_MODEL // c):
                    wo.finish(j)
            down = _WeightStager(down_jobs, stage_down, sem_down)
            down.prime()
            x, h = prep()
            y = None
            for j in range(n_chunks):
                up.finish(2 * j)
                up.finish(2 * j + 1)
                down.finish(j)
                part = swiglu(h, slice(j * c, (j + 1) * c))
                y = part if y is None else y + part
            emit(x + 0.5 * (gate * _rms(y, gpost)))

        pl.run_scoped(run,
                      pltpu.VMEM((2 * FFN_STAGE_SLOTS, D_MODEL, c), F32),
                      pltpu.VMEM((FFN_STAGE_SLOTS, c, D_MODEL), F32),
                      pltpu.SemaphoreType.DMA((2 * FFN_STAGE_SLOTS,)),
                      pltpu.SemaphoreType.DMA((FFN_STAGE_SLOTS,)))

    @pl.when(i != 0)
    def _():
        x, h = prep()
        outs = []
        for s in range(tm // FFN_SUB):
            rows = slice(s * FFN_SUB, (s + 1) * FFN_SUB)
            y = swiglu(h[rows], slice(None))
            outs.append(x[rows] + 0.5 * (gate * _rms(y, gpost)))
        emit(jnp.concatenate(outs, axis=0))


def _ffn(xs, mod, norm_pre, norm_post, wg, wu, wd, l, sub, split_out=False, mix=None):
    tm = TM_FFN
    n_ctx_tiles = N_CTX // tm
    ctx_spec = pl.BlockSpec((tm, D_MODEL), lambda i: (jnp.minimum(i, n_ctx_tiles - 1), 0))
    lat_spec = pl.BlockSpec((tm, D_MODEL), lambda i: (jnp.maximum(i - n_ctx_tiles, 0), 0))
    full_spec = pl.BlockSpec((tm, D_MODEL), lambda i: (i, 0))
    x_specs = [full_spec] if len(xs) == 1 else [ctx_spec, lat_spec]
    if split_out:
        out_specs = [ctx_spec, lat_spec]
        out_shape = [jax.ShapeDtypeStruct((N_CTX, D_MODEL), F32),
                     jax.ShapeDtypeStruct((N_LAT, D_MODEL), F32)]
    else:
        out_specs = [full_spec]
        out_shape = [jax.ShapeDtypeStruct((N_TOK, D_MODEL), F32)]
    hbm = pl.BlockSpec(memory_space=pl.ANY)
    args = list(xs)
    in_specs = list(x_specs)
    scratch = [pltpu.VMEM((D_MODEL, D_FF), BF16), pltpu.VMEM((D_MODEL, D_FF), BF16),
               pltpu.VMEM((D_FF, D_MODEL), BF16)]
    if mix is not None:
        args += [mix[0], mix[1]]
        in_specs += [pl.BlockSpec((tm, 512), lambda i: (i, 0))] * 2
    args += [mod, norm_pre, norm_post, wg, wu, wd]
    in_specs += [_layer_spec(_MOD_SHAPE, l), _layer_spec((3, D_MODEL), l),
                 _layer_spec((3, D_MODEL), l), hbm, hbm, hbm]
    if mix is not None:
        args.append(mix[2])
        in_specs.append(hbm)
        scratch.append(pltpu.VMEM((D_MODEL, D_MODEL), BF16))
    return pl.pallas_call(
        functools.partial(_ffn_kernel, l, sub, tm, len(xs), len(out_specs), mix is not None),
        grid=(N_TOK // tm,),
        in_specs=in_specs,
        out_specs=out_specs,
        out_shape=out_shape,
        scratch_shapes=scratch,
        compiler_params=pltpu.CompilerParams(
            dimension_semantics=("arbitrary",), vmem_limit_bytes=VMEM_LIMIT),
        name=f"ffn{sub}",
    )(*args)


def _swap_blocks(x, blk):
    w = x.shape[-1]
    lane = lax.broadcasted_iota(jnp.int32, x.shape, x.ndim - 1)
    first = (lane % (2 * blk)) < blk
    return jnp.where(first, pltpu.roll(x, w - blk, x.ndim - 1), pltpu.roll(x, blk, x.ndim - 1))


def _rope_tables():
    out = []
    for rot_dim in (DK_DIFF, HD_GQA):
        n_freq = rot_dim // 4
        inv = ROPE_THETA ** (-np.arange(n_freq, dtype=np.float64) / n_freq)
        lane = np.arange(256) % rot_dim
        pos = lane % (rot_dim // 2)
        freq = inv[pos % n_freq]
        sign = np.where(pos < n_freq, -1.0, 1.0)
        coord = np.concatenate([np.arange(GRID_H), np.arange(GRID_W)]).astype(np.float64)
        ang = coord[:, None] * freq[None, :]
        out.append(np.cos(ang))
        out.append(np.sin(ang) * sign[None, :])
    return np.stack(out).astype(np.float32)


def _tile_rope(tab_ref, which, rot_dim, first_grid_row, n_grid_rows, identity):
    col_part = tab_ref[which, GRID_H:GRID_H + GRID_W, :]
    lane = lax.broadcasted_iota(jnp.int32, (1, 256), 1)
    on_row_axis = (lane % rot_dim) < rot_dim // 2
    tab = jnp.concatenate(
        [jnp.where(on_row_axis, tab_ref[which, pl.ds(first_grid_row + r, 1), :], col_part)
         for r in range(n_grid_rows)], axis=0)
    return jnp.where(identity, 1.0 - (which % 2), tab)


def _mixproj_kernel(l, has_prev, x_ref, mod_ref, gpre_ref, w_ref, qg_ref, kg_ref, tab_ref, *rest):
    n_alias = 6 if has_prev else 2
    (cin_ref, qt_ref, kc_ref, kl_ref, vtc_ref, vtl_ref, odk_ref, odv_ref, ogk_ref, ogv_ref,
     wbf_ref) = rest[n_alias:]
    tm = TM_MIX
    i = pl.program_id(0)
    n_ctx_tiles = N_CTX // tm
    is_ctx = i < n_ctx_tiles
    cond = _cond_of_tile(i, tm)

    @pl.when(i == 0)
    def _():
        wbf_ref[...] = w_ref[...].astype(BF16)

    shift = _mod_row(mod_ref, 3, cond)
    scale = _mod_row(mod_ref, 4, cond)
    h = (_rms(x_ref[...], gpre_ref[pl.ds(1, 1), :]) * (1.0 + scale) + shift).astype(BF16)

    a = _dot(h, wbf_ref[:, 0:768])
    cin_ref[:, 0:256] = a[:, 0:256]
    cin_ref[:, 256:512] = a[:, 256:512] * a[:, 512:768]
    c = _dot(h, wbf_ref[:, 1536:2048])
    cin_ref[:, 512:768] = c[:, 0:256] * jax.nn.sigmoid(c[:, 256:512])

    grid_rows = tm // GRID_W
    first_row = (jnp.maximum(i - n_ctx_tiles, 0) % (DEC_SEQ // tm)) * grid_rows
    cd = _tile_rope(tab_ref, 0, DK_DIFF, first_row, grid_rows, is_ctx)
    sd = _tile_rope(tab_ref, 1, DK_DIFF, first_row, grid_rows, is_ctx)
    cg = _tile_rope(tab_ref, 2, HD_GQA, first_row, grid_rows, is_ctx)
    sg = _tile_rope(tab_ref, 3, HD_GQA, first_row, grid_rows, is_ctx)

    b = _dot(h, wbf_ref[:, 768:1536])
    bq, bk, bv = b[:, 0:256], b[:, 256:512], b[:, 512:768]
    bq_r = (bq * cd + _swap_blocks(bq, DK_DIFF // 4) * sd) * (DK_DIFF ** -0.5 * LOG2E)
    bk_r = bk * cd + _swap_blocks(bk, DK_DIFF // 4) * sd
    qt_ref[0:256, :] = bq_r.T.astype(BF16)

    d = _dot(h, wbf_ref[:, 2048:2560])
    dq, dk, dv = d[:, 0:256], d[:, 256:384], d[:, 384:512]
    dq_n = dq * lax.rsqrt(_group_mean(dq * dq, 256, HD_GQA) + EPS) * qg_ref[l:l + 1, :]
    dk_n = dk * lax.rsqrt(_group_mean(dk * dk, 128, HD_GQA) + EPS) * kg_ref[l:l + 1, :]
    dq_r = (dq_n * cg + _swap_blocks(dq_n, HD_GQA // 4) * sg) * (HD_GQA ** -0.5 * LOG2E)
    dk_r = dk_n * cg[:, 0:128] + _swap_blocks(dk_n, HD_GQA // 4) * sg[:, 0:128]
    qt_ref[256:512, :] = dq_r.T.astype(BF16)

    bk_b = bk_r.astype(BF16)
    dk_b = jnp.concatenate([dk_r, jnp.zeros_like(dk_r)], axis=1).astype(BF16)
    bv_t = bv.T.astype(BF16)
    dv_t = jnp.concatenate([dv, jnp.zeros_like(dv)], axis=1).T.astype(BF16)

    @pl.when(is_ctx)
    def _():
        kc_ref[0] = bk_b
        kc_ref[1] = dk_b
        vtc_ref[0] = bv_t
        vtc_ref[1] = dv_t
        seqs = tm // SEQ
        odk_ref[...] = bk.reshape(seqs, SEQ, 256)
        odv_ref[...] = bv.reshape(seqs, SEQ, 256)
        ogk_ref[...] = dk_n.reshape(seqs, SEQ, 128)
        ogv_ref[...] = dv.reshape(seqs, SEQ, 128)

    @pl.when(jnp.logical_not(is_ctx))
    def _():
        kl_ref[0] = bk_b
        kl_ref[1] = dk_b
        vtl_ref[0] = bv_t
        vtl_ref[1] = dv_t


def _mixproj(x, mod, norm_pre, w_mix_in, qg, kg, tables, lat_base, prev_own, l):
    tm = TM_MIX
    n_ctx_tiles = N_CTX // tm
    per_b = DEC_SEQ // tm
    seqs = tm // SEQ

    def lat_tile(i):
        t = jnp.maximum(i - n_ctx_tiles, 0)
        return t // per_b, t % per_b

    ctx_row = lambda i: jnp.minimum(i, n_ctx_tiles - 1)
    row = lambda w: pl.BlockSpec((tm, w), lambda i: (i, 0))
    own_spec = lambda w: pl.BlockSpec((seqs, None, SEQ, w), lambda i: (ctx_row(i), l, 0, 0))
    own_shape = lambda w: jax.ShapeDtypeStruct((BATCH, DEPTH, SEQ, w), F32)
    any_spec = pl.BlockSpec(memory_space=pl.ANY)
    aliased = list(lat_base) + (list(prev_own) if prev_own is not None else [])
    n_in = 7
    aliases = {n_in: 3, n_in + 1: 5}
    if prev_own is not None:
        aliases.update({n_in + 2 + k: 6 + k for k in range(4)})
    return pl.pallas_call(
        functools.partial(_mixproj_kernel, l, prev_own is not None),
        grid=(N_TOK // tm,),
        in_specs=[
            row(D_MODEL),
            _layer_spec(_MOD_SHAPE, l),
            _layer_spec((3, D_MODEL), l),
            _layer_spec((D_MODEL, MIX_IN), l, pipeline_mode=pl.Buffered(1)),
            _const_spec((DEPTH, 256)),
            _const_spec((DEPTH, 128)),
            _const_spec(tables.shape),
        ] + [any_spec] * len(aliased),
        out_specs=[row(768),
                   pl.BlockSpec((512, tm), lambda i: (0, i)),
                   pl.BlockSpec((2, tm, 256), lambda i: (0, ctx_row(i), 0)),
                   pl.BlockSpec((None, 2, tm, 256), lambda i: (lat_tile(i)[0], 0, lat_tile(i)[1], 0)),
                   pl.BlockSpec((2, 256, tm), lambda i: (0, 0, ctx_row(i))),
                   pl.BlockSpec((None, 2, 256, tm), lambda i: (lat_tile(i)[0], 0, 0, lat_tile(i)[1])),
                   own_spec(256), own_spec(256), own_spec(128), own_spec(128)],
        out_shape=[
            jax.ShapeDtypeStruct((N_TOK, 768), F32),
            jax.ShapeDtypeStruct((512, N_TOK), BF16),
            jax.ShapeDtypeStruct((2, N_CTX, 256), BF16),
            jax.ShapeDtypeStruct((DEC_BATCH, 2, S_LAT, 256), BF16),
            jax.ShapeDtypeStruct((2, 256, N_CTX), BF16),
            jax.ShapeDtypeStruct((DEC_BATCH, 2, 256, S_LAT), BF16),
            own_shape(256), own_shape(256), own_shape(128), own_shape(128),
        ],
        scratch_shapes=[pltpu.VMEM((D_MODEL, MIX_IN), BF16)],
        input_output_aliases=aliases,
        compiler_params=pltpu.CompilerParams(
            dimension_semantics=("arbitrary",), vmem_limit_bytes=VMEM_LIMIT),
        name="mixproj",
    )(x, mod, norm_pre, w_mix_in, qg, kg, tables, *aliased)


N_UNITS = 12
ATTN_SLOTS = 3


def _attn_kernel(l, qt_ref, k_ref, vt_ref, lq1_ref, lk1_ref, lq2_ref, lk2_ref, subln_ref, *rest):
    n = ATTN_SLOTS
    o_ref, qs_ref = rest[-3 - 3 * n], rest[-2 - 3 * n]
    s_refs = rest[-1 - 3 * n:-1 - 2 * n]
    e_refs = rest[-1 - 2 * n:-1 - n]
    il_refs = rest[-1 - n:-1]
    oacc_ref = rest[-1]
    tq = qt_ref.shape[1]
    half = k_ref.shape[1] // 2
    zero = jnp.zeros((), BF16)

    feat = lax.broadcasted_iota(jnp.int32, (256, 1), 0)
    qb = qt_ref[0:256, :]
    for j in range(8):
        qs_ref[j] = jnp.where(feat // DK_DIFF == j, qb, zero)
    for hd in range(4):
        g = hd // 2
        qs_ref[8 + hd] = jnp.zeros((256, tq), BF16)
        qs_ref[8 + hd, HD_GQA * g:HD_GQA * (g + 1), :] = (
            qt_ref[256 + HD_GQA * hd:256 + HD_GQA * (hd + 1), :])

    def scores(u):
        plane = 0 if u < 8 else 1
        s_ref = s_refs[u % n]
        s_ref[0:half, :] = _dot(k_ref[plane, 0:half, :], qs_ref[u])
        s_ref[half:, :] = _dot(k_ref[plane, half:, :], qs_ref[u])

    def softmax(u):
        s = s_refs[u % n][...]
        e = jnp.exp2(s - jnp.max(s, axis=0, keepdims=True))
        il_refs[u % n][...] = 1.0 / jnp.sum(e, axis=0, keepdims=True)
        e_refs[u % n][...] = e.astype(BF16)

    def values(u):
        plane = 0 if u < 8 else 1
        vrow = DV_DIFF * (u // 2) if u < 8 else HD_GQA * ((u - 8) // 2)
        e_ref = e_refs[u % n]
        o = (_dot(vt_ref[plane, vrow:vrow + DV_DIFF, 0:half], e_ref[0:half, :])
             + _dot(vt_ref[plane, vrow:vrow + DV_DIFF, half:], e_ref[half:, :]))
        oacc_ref[u] = o * il_refs[u % n][...]

    for t in range(N_UNITS + 2):
        if t < N_UNITS:
            scores(t)
        if 1 <= t <= N_UNITS:
            softmax(t - 1)
        if t >= 2:
            values(t - 2)

    lam_init = 0.8 - 0.6 * math.exp(-0.3 * l)
    row = slice(l, l + 1)
    lam = (jnp.exp(jnp.sum(lq1_ref[row, :] * lk1_ref[row, :], axis=-1, keepdims=True))
           - jnp.exp(jnp.sum(lq2_ref[row, :] * lk2_ref[row, :], axis=-1, keepdims=True))
           + lam_init)
    heads = []
    for hd in range(4):
        o = oacc_ref[2 * hd] - lam * oacc_ref[2 * hd + 1]
        nrm = lax.rsqrt(jnp.mean(o * o, axis=0, keepdims=True) + EPS)
        heads.append(o * nrm * (subln_ref[:, l:l + 1] * (1.0 - lam_init)))
    heads += [oacc_ref[8 + hd] for hd in range(4)]
    o_ref[...] = jnp.concatenate(heads, axis=0).T.astype(BF16)


def _attn_param_specs():
    vec = _const_spec((DEPTH, DK_DIFF))
    return [vec, vec, vec, vec, _const_spec((DV_DIFF, DEPTH))]


def _attn_scratch(tq, s_len):
    n = ATTN_SLOTS
    return ([pltpu.VMEM((N_UNITS, 256, tq), BF16)]
            + [pltpu.VMEM((s_len, tq), F32)] * n
            + [pltpu.VMEM((s_len, tq), BF16)] * n
            + [pltpu.VMEM((1, tq), F32)] * n
            + [pltpu.VMEM((N_UNITS, DV_DIFF, tq), F32)])


def _attn_ctx(qt, k_ctx, vt_ctx, aparams, l):
    return pl.pallas_call(
        functools.partial(_attn_kernel, l),
        grid=(BATCH,),
        in_specs=[pl.BlockSpec((512, SEQ), lambda b: (0, b)),
                  pl.BlockSpec((2, SEQ, 256), lambda b: (0, b, 0)),
                  pl.BlockSpec((2, 256, SEQ), lambda b: (0, 0, b))] + _attn_param_specs(),
        out_specs=pl.BlockSpec((SEQ, 512), lambda b: (b, 0)),
        out_shape=jax.ShapeDtypeStruct((N_TOK, 512), BF16),
        scratch_shapes=_attn_scratch(SEQ, SEQ),
        compiler_params=pltpu.CompilerParams(
            dimension_semantics=("arbitrary",), vmem_limit_bytes=VMEM_LIMIT),
        name="attn_ctx",
    )(qt, k_ctx, vt_ctx, *aparams)


def _attn_lat(qt, k_lat, vt_lat, att, aparams, l):
    tq = TQ_LAT
    nq = DEC_SEQ // tq
    tile = lambda b, i: N_CTX // tq + b * nq + i
    return pl.pallas_call(
        functools.partial(_attn_kernel, l),
        grid=(DEC_BATCH, nq),
        in_specs=[pl.BlockSpec((512, tq), lambda b, i: (0, tile(b, i))),
                  pl.BlockSpec((None, 2, S_LAT, 256), lambda b, i: (b, 0, 0, 0)),
                  pl.BlockSpec((None, 2, 256, S_LAT), lambda b, i: (b, 0, 0, 0))]
        + _attn_param_specs() + [pl.BlockSpec(memory_space=pl.ANY)],
        out_specs=pl.BlockSpec((tq, 512), lambda b, i: (tile(b, i), 0)),
        out_shape=jax.ShapeDtypeStruct((N_TOK, 512), BF16),
        scratch_shapes=_attn_scratch(tq, S_LAT),
        input_output_aliases={8: 0},
        compiler_params=pltpu.CompilerParams(
            dimension_semantics=("arbitrary", "arbitrary"), vmem_limit_bytes=VMEM_LIMIT),
        name="attn_lat",
    )(qt, k_lat, vt_lat, *aparams, att)


def _conv_kernel(l, seq_len, cin_ref, wsc_ref, wdw_ref, bdw_ref, lng_ref, lnb_ref, pw_ref, *rest):
    o_ref, pad_ref = rest[-2], rest[-1]
    t_tile = cin_ref.shape[0]
    n_seq = t_tile // seq_len
    stride = seq_len + CONV_HALO
    chunks_per_seq = seq_len // R_CONV
    rows = R_CONV

    for q in range(n_seq + 1):
        pad_ref[q * stride:q * stride + CONV_HALO, :] = jnp.zeros((CONV_HALO, 512), F32)
    for q in range(n_seq):
        pad_ref[CONV_HALO + q * stride:CONV_HALO + q * stride + seq_len, :] = (
            cin_ref[q * seq_len:(q + 1) * seq_len, 256:768])

    pw = pw_ref[...].astype(BF16)
    bdw = bdw_ref[l:l + 1, :]
    lng = lng_ref[l:l + 1, :]
    lnb = lnb_ref[l:l + 1, :]

    def body(c, carry):
        q = c // chunks_per_seq
        r = c % chunks_per_seq
        base = pl.multiple_of(q * stride + r * rows, 8)
        row0 = pl.multiple_of(c * rows, 8)
        win = pad_ref[pl.ds(base, rows + 2 * CONV_HALO), :]
        win_s = win[:, 0:256]
        win_u = win[:, 256:512]

        s_acc = None
        for k in range(SCONV_K):
            off = CONV_HALO + k - SCONV_K // 2
            term = wsc_ref[pl.ds(k, 1), :] * win_s[off:off + rows]
            s_acc = term if s_acc is None else s_acc + term
        a_out = cin_ref[pl.ds(row0, rows), 0:256] * s_acc

        acc = None
        for res in range(8):
            part = None
            for a in range(-2, 2):
                k = 8 * a + res + CCM_K // 2
                if 0 <= k < CCM_K:
                    lo = CONV_HALO + 8 * a
                    term = wdw_ref[pl.ds(k, 1), :] * win_u[lo:lo + rows + 8]
                    part = term if part is None else part + term
            part = part[res:res + rows]
            acc = part if acc is None else acc + part
        u = acc + bdw
        mu = jnp.mean(u, axis=-1, keepdims=True)
        var = jnp.mean(jnp.square(u - mu), axis=-1, keepdims=True)
        y = (u - mu) * lax.rsqrt(var + EPS) * lng + lnb
        y = y * jax.nn.sigmoid(y)
        c_out = _dot(y.astype(BF16), pw)
        o_ref[pl.ds(row0, rows), 0:256] = a_out.astype(BF16)
        o_ref[pl.ds(row0, rows), 256:512] = c_out.astype(BF16)
        return carry

    lax.fori_loop(0, t_tile // rows, body, 0, unroll=2)


def _conv(cin, prev, seq_len, tile0, n_tiles, cparams, l):
    t = T_CONV
    n_seq = t // seq_len
    vec = _const_spec((DEPTH, 256))
    in_specs = [pl.BlockSpec((t, 768), lambda i: (tile0 + i, 0)),
                _layer_spec((SCONV_K, 256), l), _layer_spec((CCM_K, 256), l), vec, vec, vec,
                _layer_spec((256, 256), l)]
    args = [cin, *cparams]
    aliases = {}
    if prev is not None:
        in_specs.append(pl.BlockSpec(memory_space=pl.ANY))
        args.append(prev)
        aliases = {len(args) - 1: 0}
    return pl.pallas_call(
        functools.partial(_conv_kernel, l, seq_len),
        grid=(n_tiles,),
        in_specs=in_specs,
        out_specs=pl.BlockSpec((t, 512), lambda i: (tile0 + i, 0)),
        out_shape=jax.ShapeDtypeStruct((N_TOK, 512), BF16),
        scratch_shapes=[pltpu.VMEM((n_seq * (seq_len + CONV_HALO) + CONV_HALO, 512), F32)],
        input_output_aliases=aliases,
        compiler_params=pltpu.CompilerParams(
            dimension_semantics=("arbitrary",), vmem_limit_bytes=VMEM_LIMIT),
        name=f"conv{seq_len}",
    )(*args)


def _kv_lat_base(cache_diff_k, cache_diff_v, cache_gqa_k, cache_gqa_v, l):
    pad128 = lambda a: jnp.pad(a.reshape(DEC_BATCH, PAST_LEN, 128), ((0, 0), (0, 0), (0, 128)))
    keys = jnp.stack([cache_diff_k[:, l].reshape(DEC_BATCH, PAST_LEN, 256),
                      pad128(cache_gqa_k[:, l])], axis=1).astype(BF16)
    vals = jnp.stack([cache_diff_v[:, l].reshape(DEC_BATCH, PAST_LEN, 256),
                      pad128(cache_gqa_v[:, l])], axis=1).astype(BF16)
    vals_t = jnp.swapaxes(vals, 2, 3)
    return (jnp.pad(keys, ((0, 0), (0, 0), (DEC_SEQ, 0), (0, 0))),
            jnp.pad(vals_t, ((0, 0), (0, 0), (0, 0), (DEC_SEQ, 0))))


def kernel(x_prompt, x_sample, cache_diff_k, cache_diff_v, cache_gqa_k, cache_gqa_v, c, c_ctx, w_ada, b_ada, norm_pre, norm_post, ffn1_gate, ffn1_up, ffn1_down, ffn2_gate, ffn2_up, ffn2_down, w_mix_in, w_mix_out, sconv_w, diff_lq1, diff_lk1, diff_lq2, diff_lk2, diff_subln, ccm_dw_w, ccm_dw_b, ccm_ln_g, ccm_ln_b, ccm_pw, gqa_qnorm, gqa_knorm):
    conds = jnp.concatenate(
        [c_ctx[None, :], c, jnp.zeros((N_COND - 1 - DEC_BATCH, D_MODEL), F32)], axis=0)
    mod = _ada(conds, w_ada, b_ada)
    tables = jnp.asarray(_rope_tables())
    qg = jnp.tile(gqa_qnorm, (1, 4))
    kg = jnp.tile(gqa_knorm, (1, 2))
    aparams = (diff_lq1, diff_lk1, diff_lq2, diff_lk2, diff_subln.T)
    cparams = (sconv_w, ccm_dw_w, ccm_dw_b, ccm_ln_g, ccm_ln_b, ccm_pw)

    xs = (x_prompt.reshape(N_CTX, D_MODEL), x_sample.reshape(N_LAT, D_MODEL))
    own = None
    for l in range(DEPTH):
        (x,) = _ffn(xs, mod, norm_pre, norm_post, ffn1_gate, ffn1_up, ffn1_down, l, 0)

        base = _kv_lat_base(cache_diff_k, cache_diff_v, cache_gqa_k, cache_gqa_v, l)
        cin, qt, k_ctx, k_lat, vt_ctx, vt_lat, *own = _mixproj(
            x, mod, norm_pre, w_mix_in, qg, kg, tables, base, own, l)

        att = _attn_ctx(qt, k_ctx, vt_ctx, aparams, l)
        att = _attn_lat(qt, k_lat, vt_lat, att, aparams, l)

        ac = _conv(cin, None, SEQ, 0, N_CTX // T_CONV, cparams, l)
        ac = _conv(cin, ac, DEC_SEQ, N_CTX // T_CONV, N_LAT // T_CONV, cparams, l)

        xs = _ffn((x,), mod, norm_pre, norm_post, ffn2_gate, ffn2_up, ffn2_down, l, 2,
                  split_out=(l == DEPTH - 1), mix=(ac, att, w_mix_out))

    y_prompt, y_sample = xs
    odk, odv, ogk, ogv = own
    return (y_prompt.reshape(BATCH, SEQ, D_MODEL),
            y_sample.reshape(DEC_BATCH, DEC_SEQ, D_MODEL),
            odk.reshape(BATCH, DEPTH, SEQ, 4, 2, DK_DIFF),
            odv.reshape(BATCH, DEPTH, SEQ, 4, DV_DIFF),
            ogk.reshape(BATCH, DEPTH, SEQ, 2, HD_GQA),
            ogv.reshape(BATCH, DEPTH, SEQ, 2, HD_GQA))
```

```python
import functools
import math

import numpy as np

import jax
import jax.numpy as jnp
from jax import lax
from jax.experimental import pallas as pl
from jax.experimental.pallas import tpu as pltpu

F32 = jnp.float32
BF16 = jnp.bfloat16

D_MODEL = 1024
BATCH = 16
SEQ = 256
DEPTH = 2
DEC_BATCH = 2
DEC_SEQ = 2048
PAST_LEN = 256
GRID_W = 64
GRID_H = DEC_SEQ // GRID_W
ROPE_THETA = 10000.0
EPS = 1e-6
N_MOD = 9
D_FF = 2816
DK_DIFF = 32
DV_DIFF = 64
HD_GQA = 64
CCM_K = 31
SCONV_K = 3
MIX_IN = 2560

N_CTX = BATCH * SEQ
N_LAT = DEC_BATCH * DEC_SEQ
N_TOK = N_CTX + N_LAT
N_COND = 8
S_LAT = DEC_SEQ + PAST_LEN

LOG2E = math.log2(math.e)

VMEM_LIMIT = 56 * 1024 * 1024

TM_FFN = 512
FFN_SUB = 256
FFN_CHUNK = 256
FFN_STAGE_SLOTS = 3
TM_MIX = 1024
TQ_LAT = 256
CTX_SEQS_PER_STEP = 2
T_CONV = 2048
R_CONV = 64
CONV_HALO = 16


def _cond_of_tile(i, tm):
    n_ctx_tiles = N_CTX // tm
    per_b = DEC_SEQ // tm
    return jnp.where(i < n_ctx_tiles, 0, 1 + (i - n_ctx_tiles) // per_b)


def _mod_row(mod_ref, k, cond):
    return mod_ref[k, pl.ds(cond, 1), :]


def _dot(a, b):
    return jnp.dot(a, b, preferred_element_type=F32)


def _dot_nt(a, b):
    return lax.dot_general(a, b, (((1,), (1,)), ((), ())), preferred_element_type=F32)


def _rms(x, g):
    ms = jnp.mean(x * x, axis=-1, keepdims=True)
    return x * lax.rsqrt(ms + EPS) * g


def _group_mean(xx, width, group):
    r = lax.broadcasted_iota(jnp.int32, (width, width), 0) // group
    c = lax.broadcasted_iota(jnp.int32, (width, width), 1) // group
    gmat = jnp.where(r == c, 1.0 / group, 0.0).astype(BF16)
    hi = xx.astype(BF16)
    lo = (xx - hi.astype(F32)).astype(BF16)
    return _dot(hi, gmat) + _dot(lo, gmat)


def _const_spec(shape):
    return pl.BlockSpec(shape, lambda *_: (0,) * len(shape))


def _layer_spec(shape, l, **kw):
    return pl.BlockSpec((None,) + shape, lambda *_: (l,) + (0,) * len(shape), **kw)


_MOD_SHAPE = (N_MOD, N_COND, D_MODEL)


def _ada_kernel(c_ref, w_ref, b_ref, o_ref):
    cnd = c_ref[...]
    s = (cnd * jax.nn.sigmoid(cnd)).astype(BF16)
    o_ref[...] = _dot(s, w_ref[...].astype(BF16)) + b_ref[...]


def _ada(conds, w_ada, b_ada):
    return pl.pallas_call(
        _ada_kernel,
        grid=(DEPTH, N_MOD),
        in_specs=[
            pl.BlockSpec((N_COND, D_MODEL), lambda l, n: (0, 0)),
            pl.BlockSpec((None, D_MODEL, D_MODEL), lambda l, n: (l, 0, n)),
            pl.BlockSpec((None, None, 1, D_MODEL), lambda l, n: (l, n, 0, 0)),
        ],
        out_specs=pl.BlockSpec((None, None, N_COND, D_MODEL), lambda l, n: (l, n, 0, 0)),
        out_shape=jax.ShapeDtypeStruct((DEPTH,) + _MOD_SHAPE, F32),
        compiler_params=pltpu.CompilerParams(
            dimension_semantics=("arbitrary", "arbitrary"), vmem_limit_bytes=VMEM_LIMIT),
        name="ada",
    )(conds, w_ada, b_ada.reshape(DEPTH, N_MOD, 1, D_MODEL))


class _WeightStager:
    def __init__(self, jobs, stage_ref, sem_ref):
        self.jobs, self.stage_ref, self.depth = jobs, stage_ref, stage_ref.shape[0]
        self.copies = [
            pltpu.make_async_copy(src, stage_ref.at[k % self.depth], sem_ref.at[k % self.depth])
            for k, (src, _) in enumerate(jobs)]

    def prime(self):
        for k in range(min(self.depth - 1, len(self.jobs))):
            self.copies[k].start()

    def finish(self, k):
        if k + self.depth - 1 < len(self.jobs):
            self.copies[k + self.depth - 1].start()
        self.copies[k].wait()
        self.jobs[k][1](self.stage_ref[k % self.depth].astype(BF16))


def _ffn_kernel(l, sub, tm, n_x, n_o, mix, *refs):
    refs = list(refs)
    x_refs = [refs.pop(0) for _ in range(n_x)]
    ac_ref, att_ref = (refs.pop(0), refs.pop(0)) if mix else (None, None)
    mod_ref, gpre_ref, gpost_ref, wg_hbm, wu_hbm, wd_hbm = [refs.pop(0) for _ in range(6)]
    wo_hbm = refs.pop(0) if mix else None
    o_refs = [refs.pop(0) for _ in range(n_o)]
    wg_ref, wu_ref, wd_ref = [refs.pop(0) for _ in range(3)]
    wo_ref = refs.pop(0) if mix else None
    i = pl.program_id(0)
    is_ctx = i < N_CTX // tm
    cond = _cond_of_tile(i, tm)
    c = FFN_CHUNK
    n_chunks = D_FF // c
    gate = _mod_row(mod_ref, 3 * sub + 2, cond)
    gpost = gpost_ref[pl.ds(sub, 1), :]

    def prep():
        if n_x == 1:
            x = x_refs[0][...]
        else:
            x = jnp.where(is_ctx, x_refs[0][...], x_refs[1][...])
        if mix:
            ym = (_dot(ac_ref[:, 0:256], wo_ref[0:256, :]) + _dot(att_ref[:, 0:256], wo_ref[256:512, :])
                  + _dot(ac_ref[:, 256:512], wo_ref[512:768, :])
                  + _dot(att_ref[:, 256:512], wo_ref[768:1024, :]))
            x = x + _mod_row(mod_ref, 5, cond) * _rms(ym, gpost_ref[pl.ds(1, 1), :])
        shift = _mod_row(mod_ref, 3 * sub, cond)
        scale = _mod_row(mod_ref, 3 * sub + 1, cond)
        h = (_rms(x, gpre_ref[pl.ds(sub, 1), :]) * (1.0 + scale) + shift).astype(BF16)
        return x, h

    def swiglu(hh, cols):
        g = _dot(hh, wg_ref[:, cols])
        u = _dot(hh, wu_ref[:, cols])
        a = ((g * jax.nn.sigmoid(g)) * u).astype(BF16)
        return _dot(a, wd_ref[cols, :])

    def emit(y):
        if n_o == 1:
            o_refs[0][...] = y
        else:
            @pl.when(is_ctx)
            def _():
                o_refs[0][...] = y

            @pl.when(jnp.logical_not(is_ctx))
            def _():
                o_refs[1][...] = y

    @pl.when(i == 0)
    def _():
        def col_store(dst, j):
            def store(v):
                dst[:, j * c:(j + 1) * c] = v
            return store

        def row_store(j):
            def store(v):
                wd_ref[j * c:(j + 1) * c, :] = v
            return store

        up_jobs = [(src.at[l, :, pl.ds(j * c, c)], col_store(dst, j))
                   for j in range(n_chunks) for src, dst in ((wg_hbm, wg_ref), (wu_hbm, wu_ref))]
        down_jobs = [(wd_hbm.at[l, pl.ds(j * c, c), :], row_store(j)) for j in range(n_chunks)]

        def run(stage_up, stage_down, sem_up, sem_down):
            up = _WeightStager(up_jobs, stage_up, sem_up)
            up.prime()
            if mix:
                def wo_store(j):
                    def store(v):
                        wo_ref[j * c:(j + 1) * c, :] = v
                    return store

                wo = _WeightStager(
                    [(wo_hbm.at[l, pl.ds(j * c, c), :], wo_store(j)) for j in range(D_MODEL // c)],
                    stage_down, sem_down)
                wo.prime()
                for j in range(D_MODEL // c):
                    wo.finish(j)
            down = _WeightStager(down_jobs, stage_down, sem_down)
            down.prime()
            x, h = prep()
            y = None
            for j in range(n_chunks):
                up.finish(2 * j)
                up.finish(2 * j + 1)
                down.finish(j)
                part = swiglu(h, slice(j * c, (j + 1) * c))
                y = part if y is None else y + part
            emit(x + 0.5 * (gate * _rms(y, gpost)))

        pl.run_scoped(run,
                      pltpu.VMEM((2 * FFN_STAGE_SLOTS, D_MODEL, c), F32),
                      pltpu.VMEM((FFN_STAGE_SLOTS, c, D_MODEL), F32),
                      pltpu.SemaphoreType.DMA((2 * FFN_STAGE_SLOTS,)),
                      pltpu.SemaphoreType.DMA((FFN_STAGE_SLOTS,)))

    @pl.when(i != 0)
    def _():
        x, h = prep()
        outs = []
        for s in range(tm // FFN_SUB):
            rows = slice(s * FFN_SUB, (s + 1) * FFN_SUB)
            y = swiglu(h[rows], slice(None))
            outs.append(x[rows] + 0.5 * (gate * _rms(y, gpost)))
        emit(jnp.concatenate(outs, axis=0))


def _ffn(xs, mod, norm_pre, norm_post, wg, wu, wd, l, sub, split_out=False, mix=None):
    tm = TM_FFN
    n_ctx_tiles = N_CTX // tm
    ctx_spec = pl.BlockSpec((tm, D_MODEL), lambda i: (jnp.minimum(i, n_ctx_tiles - 1), 0))
    lat_spec = pl.BlockSpec((tm, D_MODEL), lambda i: (jnp.maximum(i - n_ctx_tiles, 0), 0))
    full_spec = pl.BlockSpec((tm, D_MODEL), lambda i: (i, 0))
    x_specs = [full_spec] if len(xs) == 1 else [ctx_spec, lat_spec]
    if split_out:
        out_specs = [ctx_spec, lat_spec]
        out_shape = [jax.ShapeDtypeStruct((N_CTX, D_MODEL), F32),
                     jax.ShapeDtypeStruct((N_LAT, D_MODEL), F32)]
    else:
        out_specs = [full_spec]
        out_shape = [jax.ShapeDtypeStruct((N_TOK, D_MODEL), F32)]
    hbm = pl.BlockSpec(memory_space=pl.ANY)
    args = list(xs)
    in_specs = list(x_specs)
    scratch = [pltpu.VMEM((D_MODEL, D_FF), BF16), pltpu.VMEM((D_MODEL, D_FF), BF16),
               pltpu.VMEM((D_FF, D_MODEL), BF16)]
    if mix is not None:
        args += [mix[0], mix[1]]
        in_specs += [pl.BlockSpec((tm, 512), lambda i: (i, 0))] * 2
    args += [mod, norm_pre, norm_post, wg, wu, wd]
    in_specs += [_layer_spec(_MOD_SHAPE, l), _layer_spec((3, D_MODEL), l),
                 _layer_spec((3, D_MODEL), l), hbm, hbm, hbm]
    if mix is not None:
        args.append(mix[2])
        in_specs.append(hbm)
        scratch.append(pltpu.VMEM((D_MODEL, D_MODEL), BF16))
    return pl.pallas_call(
        functools.partial(_ffn_kernel, l, sub, tm, len(xs), len(out_specs), mix is not None),
        grid=(N_TOK // tm,),
        in_specs=in_specs,
        out_specs=out_specs,
        out_shape=out_shape,
        scratch_shapes=scratch,
        compiler_params=pltpu.CompilerParams(
            dimension_semantics=("arbitrary",), vmem_limit_bytes=VMEM_LIMIT),
        name=f"ffn{sub}",
    )(*args)


def _swap_blocks(x, blk):
    w = x.shape[-1]
    lane = lax.broadcasted_iota(jnp.int32, x.shape, x.ndim - 1)
    first = (lane % (2 * blk)) < blk
    return jnp.where(first, pltpu.roll(x, w - blk, x.ndim - 1), pltpu.roll(x, blk, x.ndim - 1))


def _rope_tables():
    out = []
    for rot_dim in (DK_DIFF, HD_GQA):
        n_freq = rot_dim // 4
        inv = ROPE_THETA ** (-np.arange(n_freq, dtype=np.float64) / n_freq)
        lane = np.arange(256) % rot_dim
        pos = lane % (rot_dim // 2)
        freq = inv[pos % n_freq]
        sign = np.where(pos < n_freq, -1.0, 1.0)
        coord = np.concatenate([np.arange(GRID_H), np.arange(GRID_W)]).astype(np.float64)
        ang = coord[:, None] * freq[None, :]
        out.append(np.cos(ang))
        out.append(np.sin(ang) * sign[None, :])
    return np.stack(out).astype(np.float32)


def _tile_rope(tab_ref, which, rot_dim, first_grid_row, n_grid_rows, identity):
    col_part = tab_ref[which, GRID_H:GRID_H + GRID_W, :]
    lane = lax.broadcasted_iota(jnp.int32, (1, 256), 1)
    on_row_axis = (lane % rot_dim) < rot_dim // 2
    tab = jnp.concatenate(
        [jnp.where(on_row_axis, tab_ref[which, pl.ds(first_grid_row + r, 1), :], col_part)
         for r in range(n_grid_rows)], axis=0)
    return jnp.where(identity, 1.0 - (which % 2), tab)


def _mixproj_kernel(l, has_prev, x_ref, mod_ref, gpre_ref, w_ref, qg_ref, kg_ref, tab_ref, *rest):
    n_alias = 6 if has_prev else 2
    (cin_ref, qt_ref, kc_ref, kl_ref, vtc_ref, vtl_ref, odk_ref, odv_ref, ogk_ref, ogv_ref,
     wbf_ref) = rest[n_alias:]
    tm = TM_MIX
    i = pl.program_id(0)
    n_ctx_tiles = N_CTX // tm
    is_ctx = i < n_ctx_tiles
    cond = _cond_of_tile(i, tm)

    @pl.when(i == 0)
    def _():
        wbf_ref[...] = w_ref[...].astype(BF16)

    shift = _mod_row(mod_ref, 3, cond)
    scale = _mod_row(mod_ref, 4, cond)
    h = (_rms(x_ref[...], gpre_ref[pl.ds(1, 1), :]) * (1.0 + scale) + shift).astype(BF16)

    a = _dot(h, wbf_ref[:, 0:768])
    cin_ref[:, 0:256] = a[:, 0:256]
    cin_ref[:, 256:512] = a[:, 256:512] * a[:, 512:768]
    c = _dot(h, wbf_ref[:, 1536:2048])
    cin_ref[:, 512:768] = c[:, 0:256] * jax.nn.sigmoid(c[:, 256:512])

    grid_rows = tm // GRID_W
    first_row = (jnp.maximum(i - n_ctx_tiles, 0) % (DEC_SEQ // tm)) * grid_rows
    cd = _tile_rope(tab_ref, 0, DK_DIFF, first_row, grid_rows, is_ctx)
    sd = _tile_rope(tab_ref, 1, DK_DIFF, first_row, grid_rows, is_ctx)
    cg = _tile_rope(tab_ref, 2, HD_GQA, first_row, grid_rows, is_ctx)
    sg = _tile_rope(tab_ref, 3, HD_GQA, first_row, grid_rows, is_ctx)

    b = _dot(h, wbf_ref[:, 768:1536])
    bq, bk, bv = b[:, 0:256], b[:, 256:512], b[:, 512:768]
    bq_r = (bq * cd + _swap_blocks(bq, DK_DIFF // 4) * sd) * (DK_DIFF ** -0.5 * LOG2E)
    bk_r = bk * cd + _swap_blocks(bk, DK_DIFF // 4) * sd
    qt_ref[0:256, :] = bq_r.T.astype(BF16)

    d = _dot(h, wbf_ref[:, 2048:2560])
    dq, dk, dv = d[:, 0:256], d[:, 256:384], d[:, 384:512]
    dq_n = dq * lax.rsqrt(_group_mean(dq * dq, 256, HD_GQA) + EPS) * qg_ref[l:l + 1, :]
    dk_n = dk * lax.rsqrt(_group_mean(dk * dk, 128, HD_GQA) + EPS) * kg_ref[l:l + 1, :]
    dq_r = (dq_n * cg + _swap_blocks(dq_n, HD_GQA // 4) * sg) * (HD_GQA ** -0.5 * LOG2E)
    dk_r = dk_n * cg[:, 0:128] + _swap_blocks(dk_n, HD_GQA // 4) * sg[:, 0:128]
    qt_ref[256:512, :] = dq_r.T.astype(BF16)

    bk_b = bk_r.astype(BF16)
    dk_b = jnp.concatenate([dk_r, jnp.zeros_like(dk_r)], axis=1).astype(BF16)
    bv_t = bv.T.astype(BF16)
    dv_t = jnp.concatenate([dv, jnp.zeros_like(dv)], axis=1).T.astype(BF16)

    @pl.when(is_ctx)
    def _():
        kc_ref[0] = bk_b
        kc_ref[1] = dk_b
        vtc_ref[0] = bv_t
        vtc_ref[1] = dv_t
        seqs = tm // SEQ
        odk_ref[...] = bk.reshape(seqs, SEQ, 256)
        odv_ref[...] = bv.reshape(seqs, SEQ, 256)
        ogk_ref[...] = dk_n.reshape(seqs, SEQ, 128)
        ogv_ref[...] = dv.reshape(seqs, SEQ, 128)

    @pl.when(jnp.logical_not(is_ctx))
    def _():
        kl_ref[0] = bk_b
        kl_ref[1] = dk_b
        vtl_ref[0] = bv_t
        vtl_ref[1] = dv_t


def _mixproj(x, mod, norm_pre, w_mix_in, qg, kg, tables, lat_base, prev_own, l):
    tm = TM_MIX
    n_ctx_tiles = N_CTX // tm
    per_b = DEC_SEQ // tm
    seqs = tm // SEQ

    def lat_tile(i):
        t = jnp.maximum(i - n_ctx_tiles, 0)
        return t // per_b, t % per_b

    ctx_row = lambda i: jnp.minimum(i, n_ctx_tiles - 1)
    row = lambda w: pl.BlockSpec((tm, w), lambda i: (i, 0))
    own_spec = lambda w: pl.BlockSpec((seqs, None, SEQ, w), lambda i: (ctx_row(i), l, 0, 0))
    own_shape = lambda w: jax.ShapeDtypeStruct((BATCH, DEPTH, SEQ, w), F32)
    any_spec = pl.BlockSpec(memory_space=pl.ANY)
    aliased = list(lat_base) + (list(prev_own) if prev_own is not None else [])
    n_in = 7
    aliases = {n_in: 3, n_in + 1: 5}
    if prev_own is not None:
        aliases.update({n_in + 2 + k: 6 + k for k in range(4)})
    return pl.pallas_call(
        functools.partial(_mixproj_kernel, l, prev_own is not None),
        grid=(N_TOK // tm,),
        in_specs=[
            row(D_MODEL),
            _layer_spec(_MOD_SHAPE, l),
            _layer_spec((3, D_MODEL), l),
            _layer_spec((D_MODEL, MIX_IN), l, pipeline_mode=pl.Buffered(1)),
            _const_spec((DEPTH, 256)),
            _const_spec((DEPTH, 128)),
            _const_spec(tables.shape),
        ] + [any_spec] * len(aliased),
        out_specs=[row(768),
                   pl.BlockSpec((512, tm), lambda i: (0, i)),
                   pl.BlockSpec((2, tm, 256), lambda i: (0, ctx_row(i), 0)),
                   pl.BlockSpec((None, 2, tm, 256), lambda i: (lat_tile(i)[0], 0, lat_tile(i)[1], 0)),
                   pl.BlockSpec((2, 256, tm), lambda i: (0, 0, ctx_row(i))),
                   pl.BlockSpec((None, 2, 256, tm), lambda i: (lat_tile(i)[0], 0, 0, lat_tile(i)[1])),
                   own_spec(256), own_spec(256), own_spec(128), own_spec(128)],
        out_shape=[
            jax.ShapeDtypeStruct((N_TOK, 768), F32),
            jax.ShapeDtypeStruct((512, N_TOK), BF16),
            jax.ShapeDtypeStruct((2, N_CTX, 256), BF16),
            jax.ShapeDtypeStruct((DEC_BATCH, 2, S_LAT, 256), BF16),
            jax.ShapeDtypeStruct((2, 256, N_CTX), BF16),
            jax.ShapeDtypeStruct((DEC_BATCH, 2, 256, S_LAT), BF16),
            own_shape(256), own_shape(256), own_shape(128), own_shape(128),
        ],
        scratch_shapes=[pltpu.VMEM((D_MODEL, MIX_IN), BF16)],
        input_output_aliases=aliases,
        compiler_params=pltpu.CompilerParams(
            dimension_semantics=("arbitrary",), vmem_limit_bytes=VMEM_LIMIT),
        name="mixproj",
    )(x, mod, norm_pre, w_mix_in, qg, kg, tables, *aliased)


N_UNITS = 12
ATTN_SLOTS = 3


def _attn_kernel(l, qt_ref, k_ref, vt_ref, lq1_ref, lk1_ref, lq2_ref, lk2_ref, subln_ref, *rest):
    n = ATTN_SLOTS
    o_ref, qs_ref = rest[-3 - 3 * n], rest[-2 - 3 * n]
    s_refs = rest[-1 - 3 * n:-1 - 2 * n]
    e_refs = rest[-1 - 2 * n:-1 - n]
    il_refs = rest[-1 - n:-1]
    oacc_ref = rest[-1]
    half = k_ref.shape[1] // 2

    for u in range(N_UNITS):
        qs_ref[u] = _unit_queries(qt_ref, u)

    def scores(u):
        plane = 0 if u < 8 else 1
        s_ref = s_refs[u % n]
        s_ref[0:half, :] = _dot(k_ref[plane, 0:half, :], qs_ref[u])
        s_ref[half:, :] = _dot(k_ref[plane, half:, :], qs_ref[u])

    def softmax(u):
        s = s_refs[u % n][...]
        e = jnp.exp2(s - jnp.max(s, axis=0, keepdims=True))
        il_refs[u % n][...] = 1.0 / jnp.sum(e, axis=0, keepdims=True)
        e_refs[u % n][...] = e.astype(BF16)

    def values(u):
        plane = 0 if u < 8 else 1
        vrow = DV_DIFF * (u // 2) if u < 8 else HD_GQA * ((u - 8) // 2)
        e_ref = e_refs[u % n]
        o = (_dot(vt_ref[plane, vrow:vrow + DV_DIFF, 0:half], e_ref[0:half, :])
             + _dot(vt_ref[plane, vrow:vrow + DV_DIFF, half:], e_ref[half:, :]))
        oacc_ref[u] = o * il_refs[u % n][...]

    for t in range(N_UNITS + 2):
        if t < N_UNITS:
            scores(t)
        if 1 <= t <= N_UNITS:
            softmax(t - 1)
        if t >= 2:
            values(t - 2)

    _attn_finish(l, [oacc_ref[u] for u in range(N_UNITS)],
                 lq1_ref, lk1_ref, lq2_ref, lk2_ref, subln_ref, o_ref)


def _unit_queries(qt_ref, u, cols=slice(None)):
    if u < 8:
        feat = lax.broadcasted_iota(jnp.int32, (256, 1), 0)
        return jnp.where(feat // DK_DIFF == u, qt_ref[0:256, cols], jnp.zeros((), BF16))
    hd = u - 8
    g = hd // 2
    rows = qt_ref[256 + HD_GQA * hd:256 + HD_GQA * (hd + 1), cols]
    tq = rows.shape[1]
    pieces = ([jnp.zeros((HD_GQA * g, tq), BF16)] if g else []) + [rows]
    pieces.append(jnp.zeros((256 - HD_GQA * (g + 1), tq), BF16))
    return jnp.concatenate(pieces, axis=0)


def _attn_finish(l, outs, lq1_ref, lk1_ref, lq2_ref, lk2_ref, subln_ref, o_ref, rows=slice(None)):
    lam_init = 0.8 - 0.6 * math.exp(-0.3 * l)
    row = slice(l, l + 1)
    lam = (jnp.exp(jnp.sum(lq1_ref[row, :] * lk1_ref[row, :], axis=-1, keepdims=True))
           - jnp.exp(jnp.sum(lq2_ref[row, :] * lk2_ref[row, :], axis=-1, keepdims=True))
           + lam_init)
    heads = []
    for hd in range(4):
        o = outs[2 * hd] - lam * outs[2 * hd + 1]
        nrm = lax.rsqrt(jnp.mean(o * o, axis=0, keepdims=True) + EPS)
        heads.append(o * nrm * (subln_ref[:, l:l + 1] * (1.0 - lam_init)))
    heads += outs[8:12]
    o_ref[rows, :] = jnp.concatenate(heads, axis=0).T.astype(BF16)


def _attn_ctx_kernel(l, qt_ref, k_ref, vt_ref, lq1_ref, lk1_ref, lq2_ref, lk2_ref, subln_ref,
                     o_ref):
    for q in range(qt_ref.shape[1] // SEQ):
        tok = slice(q * SEQ, (q + 1) * SEQ)
        w_diff = jnp.concatenate([_unit_queries(qt_ref, u, tok) for u in range(8)], axis=1)
        w_gqa = jnp.concatenate([_unit_queries(qt_ref, u, tok) for u in range(8, N_UNITS)], axis=1)
        s = jnp.concatenate([_dot(k_ref[0, tok, :], w_diff), _dot(k_ref[1, tok, :], w_gqa)],
                            axis=1)
        e = jnp.exp2(s - jnp.max(s, axis=0, keepdims=True))
        il = 1.0 / jnp.sum(e, axis=0, keepdims=True)
        e = e.astype(BF16)
        outs = []
        for u in range(N_UNITS):
            plane = 0 if u < 8 else 1
            vrow = DV_DIFF * (u // 2) if u < 8 else HD_GQA * ((u - 8) // 2)
            cols = slice(u * SEQ, (u + 1) * SEQ)
            outs.append(_dot(vt_ref[plane, vrow:vrow + DV_DIFF, tok], e[:, cols]) * il[:, cols])
        _attn_finish(l, outs, lq1_ref, lk1_ref, lq2_ref, lk2_ref, subln_ref, o_ref, tok)


def _attn_param_specs():
    vec = _const_spec((DEPTH, DK_DIFF))
    return [vec, vec, vec, vec, _const_spec((DV_DIFF, DEPTH))]


def _attn_scratch(tq, s_len):
    n = ATTN_SLOTS
    return ([pltpu.VMEM((N_UNITS, 256, tq), BF16)]
            + [pltpu.VMEM((s_len, tq), F32)] * n
            + [pltpu.VMEM((s_len, tq), BF16)] * n
            + [pltpu.VMEM((1, tq), F32)] * n
            + [pltpu.VMEM((N_UNITS, DV_DIFF, tq), F32)])


def _attn_ctx(qt, k_ctx, vt_ctx, aparams, l):
    t = CTX_SEQS_PER_STEP * SEQ
    return pl.pallas_call(
        functools.partial(_attn_ctx_kernel, l),
        grid=(N_CTX // t,),
        in_specs=[pl.BlockSpec((512, t), lambda b: (0, b)),
                  pl.BlockSpec((2, t, 256), lambda b: (0, b, 0)),
                  pl.BlockSpec((2, 256, t), lambda b: (0, 0, b))] + _attn_param_specs(),
        out_specs=pl.BlockSpec((t, 512), lambda b: (b, 0)),
        out_shape=jax.ShapeDtypeStruct((N_TOK, 512), BF16),
        compiler_params=pltpu.CompilerParams(
            dimension_semantics=("arbitrary",), vmem_limit_bytes=VMEM_LIMIT),
        name="attn_ctx",
    )(qt, k_ctx, vt_ctx, *aparams)


def _attn_lat(qt, k_lat, vt_lat, att, aparams, l):
    tq = TQ_LAT
    nq = DEC_SEQ // tq
    tile = lambda b, i: N_CTX // tq + b * nq + i
    return pl.pallas_call(
        functools.partial(_attn_kernel, l),
        grid=(DEC_BATCH, nq),
        in_specs=[pl.BlockSpec((512, tq), lambda b, i: (0, tile(b, i))),
                  pl.BlockSpec((None, 2, S_LAT, 256), lambda b, i: (b, 0, 0, 0)),
                  pl.BlockSpec((None, 2, 256, S_LAT), lambda b, i: (b, 0, 0, 0))]
        + _attn_param_specs() + [pl.BlockSpec(memory_space=pl.ANY)],
        out_specs=pl.BlockSpec((tq, 512), lambda b, i: (tile(b, i), 0)),
        out_shape=jax.ShapeDtypeStruct((N_TOK, 512), BF16),
        scratch_shapes=_attn_scratch(tq, S_LAT),
        input_output_aliases={8: 0},
        compiler_params=pltpu.CompilerParams(
            dimension_semantics=("arbitrary", "arbitrary"), vmem_limit_bytes=VMEM_LIMIT),
        name="attn_lat",
    )(qt, k_lat, vt_lat, *aparams, att)


def _conv_kernel(l, seq_len, cin_ref, wsc_ref, wdw_ref, bdw_ref, lng_ref, lnb_ref, pw_ref, *rest):
    o_ref, pad_ref = rest[-2], rest[-1]
    t_tile = cin_ref.shape[0]
    n_seq = t_tile // seq_len
    stride = seq_len + CONV_HALO
    chunks_per_seq = seq_len // R_CONV
    rows = R_CONV

    for q in range(n_seq + 1):
        pad_ref[q * stride:q * stride + CONV_HALO, :] = jnp.zeros((CONV_HALO, 512), F32)
    for q in range(n_seq):
        pad_ref[CONV_HALO + q * stride:CONV_HALO + q * stride + seq_len, :] = (
            cin_ref[q * seq_len:(q + 1) * seq_len, 256:768])

    pw = pw_ref[...].astype(BF16)
    bdw = bdw_ref[l:l + 1, :]
    lng = lng_ref[l:l + 1, :]
    lnb = lnb_ref[l:l + 1, :]

    def body(c, carry):
        q = c // chunks_per_seq
        r = c % chunks_per_seq
        base = pl.multiple_of(q * stride + r * rows, 8)
        row0 = pl.multiple_of(c * rows, 8)
        win = pad_ref[pl.ds(base, rows + 2 * CONV_HALO), :]
        win_s = win[:, 0:256]
        win_u = win[:, 256:512]

        s_acc = None
        for k in range(SCONV_K):
            off = CONV_HALO + k - SCONV_K // 2
            term = wsc_ref[pl.ds(k, 1), :] * win_s[off:off + rows]
            s_acc = term if s_acc is None else s_acc + term
        a_out = cin_ref[pl.ds(row0, rows), 0:256] * s_acc

        acc = None
        for res in range(8):
            part = None
            for a in range(-2, 2):
                k = 8 * a + res + CCM_K // 2
                if 0 <= k < CCM_K:
                    lo = CONV_HALO + 8 * a
                    term = wdw_ref[pl.ds(k, 1), :] * win_u[lo:lo + rows + 8]
                    part = term if part is None else part + term
            part = part[res:res + rows]
            acc = part if acc is None else acc + part
        u = acc + bdw
        mu = jnp.mean(u, axis=-1, keepdims=True)
        var = jnp.mean(jnp.square(u - mu), axis=-1, keepdims=True)
        y = (u - mu) * lax.rsqrt(var + EPS) * lng + lnb
        y = y * jax.nn.sigmoid(y)
        c_out = _dot(y.astype(BF16), pw)
        o_ref[pl.ds(row0, rows), 0:256] = a_out.astype(BF16)
        o_ref[pl.ds(row0, rows), 256:512] = c_out.astype(BF16)
        return carry

    lax.fori_loop(0, t_tile // rows, body, 0, unroll=4)


def _conv(cin, prev, seq_len, tile0, n_tiles, cparams, l):
    t = T_CONV
    n_seq = t // seq_len
    vec = _const_spec((DEPTH, 256))
    in_specs = [pl.BlockSpec((t, 768), lambda i: (tile0 + i, 0)),
                _layer_spec((SCONV_K, 256), l), _layer_spec((CCM_K, 256), l), vec, vec, vec,
                _layer_spec((256, 256), l)]
    args = [cin, *cparams]
    aliases = {}
    if prev is not None:
        in_specs.append(pl.BlockSpec(memory_space=pl.ANY))
        args.append(prev)
        aliases = {len(args) - 1: 0}
    return pl.pallas_call(
        functools.partial(_conv_kernel, l, seq_len),
        grid=(n_tiles,),
        in_specs=in_specs,
        out_specs=pl.BlockSpec((t, 512), lambda i: (tile0 + i, 0)),
        out_shape=jax.ShapeDtypeStruct((N_TOK, 512), BF16),
        scratch_shapes=[pltpu.VMEM((n_seq * (seq_len + CONV_HALO) + CONV_HALO, 512), F32)],
        input_output_aliases=aliases,
        compiler_params=pltpu.CompilerParams(
            dimension_semantics=("arbitrary",), vmem_limit_bytes=VMEM_LIMIT),
        name=f"conv{seq_len}",
    )(*args)


def _kv_lat_base(cache_diff_k, cache_diff_v, cache_gqa_k, cache_gqa_v, l):
    pad128 = lambda a: jnp.pad(a.reshape(DEC_BATCH, PAST_LEN, 128), ((0, 0), (0, 0), (0, 128)))
    keys = jnp.stack([cache_diff_k[:, l].reshape(DEC_BATCH, PAST_LEN, 256),
                      pad128(cache_gqa_k[:, l])], axis=1).astype(BF16)
    vals = jnp.stack([cache_diff_v[:, l].reshape(DEC_BATCH, PAST_LEN, 256),
                      pad128(cache_gqa_v[:, l])], axis=1).astype(BF16)
    vals_t = jnp.swapaxes(vals, 2, 3)
    return (jnp.pad(keys, ((0, 0), (0, 0), (DEC_SEQ, 0), (0, 0))),
            jnp.pad(vals_t, ((0, 0), (0, 0), (0, 0), (DEC_SEQ, 0))))


def kernel(x_prompt, x_sample, cache_diff_k, cache_diff_v, cache_gqa_k, cache_gqa_v, c, c_ctx, w_ada, b_ada, norm_pre, norm_post, ffn1_gate, ffn1_up, ffn1_down, ffn2_gate, ffn2_up, ffn2_down, w_mix_in, w_mix_out, sconv_w, diff_lq1, diff_lk1, diff_lq2, diff_lk2, diff_subln, ccm_dw_w, ccm_dw_b, ccm_ln_g, ccm_ln_b, ccm_pw, gqa_qnorm, gqa_knorm):
    conds = jnp.concatenate(
        [c_ctx[None, :], c, jnp.zeros((N_COND - 1 - DEC_BATCH, D_MODEL), F32)], axis=0)
    mod = _ada(conds, w_ada, b_ada)
    tables = jnp.asarray(_rope_tables())
    qg = jnp.tile(gqa_qnorm, (1, 4))
    kg = jnp.tile(gqa_knorm, (1, 2))
    aparams = (diff_lq1, diff_lk1, diff_lq2, diff_lk2, diff_subln.T)
    cparams = (sconv_w, ccm_dw_w, ccm_dw_b, ccm_ln_g, ccm_ln_b, ccm_pw)

    xs = (x_prompt.reshape(N_CTX, D_MODEL), x_sample.reshape(N_LAT, D_MODEL))
    own = None
    for l in range(DEPTH):
        (x,) = _ffn(xs, mod, norm_pre, norm_post, ffn1_gate, ffn1_up, ffn1_down, l, 0)

        base = _kv_lat_base(cache_diff_k, cache_diff_v, cache_gqa_k, cache_gqa_v, l)
        cin, qt, k_ctx, k_lat, vt_ctx, vt_lat, *own = _mixproj(
            x, mod, norm_pre, w_mix_in, qg, kg, tables, base, own, l)

        att = _attn_ctx(qt, k_ctx, vt_ctx, aparams, l)
        att = _attn_lat(qt, k_lat, vt_lat, att, aparams, l)

        ac = _conv(cin, None, SEQ, 0, N_CTX // T_CONV, cparams, l)
        ac = _conv(cin, ac, DEC_SEQ, N_CTX // T_CONV, N_LAT // T_CONV, cparams, l)

        xs = _ffn((x,), mod, norm_pre, norm_post, ffn2_gate, ffn2_up, ffn2_down, l, 2,
                  split_out=(l == DEPTH - 1), mix=(ac, att, w_mix_out))

    y_prompt, y_sample = xs
    odk, odv, ogk, ogv = own
    return (y_prompt.reshape(BATCH, SEQ, D_MODEL),
            y_sample.reshape(DEC_BATCH, DEC_SEQ, D_MODEL),
            odk.reshape(BATCH, DEPTH, SEQ, 4, 2, DK_DIFF),
            odv.reshape(BATCH, DEPTH, SEQ, 4, DV_DIFF),
            ogk.reshape(BATCH, DEPTH, SEQ, 2, HD_GQA),
            ogv.reshape(BATCH, DEPTH, SEQ, 2, HD_GQA))
```

```python
import functools
import math

import numpy as np

import jax
import jax.numpy as jnp
from jax import lax
from jax.experimental import pallas as pl
from jax.experimental.pallas import tpu as pltpu

F32 = jnp.float32
BF16 = jnp.bfloat16

D_MODEL = 1024
BATCH = 16
SEQ = 256
DEPTH = 2
DEC_BATCH = 2
DEC_SEQ = 2048
PAST_LEN = 256
GRID_W = 64
GRID_H = DEC_SEQ // GRID_W
ROPE_THETA = 10000.0
EPS = 1e-6
N_MOD = 9
D_FF = 2816
DK_DIFF = 32
DV_DIFF = 64
HD_GQA = 64
CCM_K = 31
SCONV_K = 3
MIX_IN = 2560

N_CTX = BATCH * SEQ
N_LAT = DEC_BATCH * DEC_SEQ
N_TOK = N_CTX + N_LAT
N_COND = 8
S_LAT = DEC_SEQ + PAST_LEN

LOG2E = math.log2(math.e)

VMEM_LIMIT = 56 * 1024 * 1024

TM_FFN = 512
FFN_SUB = 256
FFN_CHUNK = 256
FFN_STAGE_SLOTS = 3
TM_MIX = 1024
TQ_LAT = 256
CTX_SEQS_PER_STEP = 4
T_CONV = 2048
R_CONV = 64
CONV_HALO = 16


def _cond_of_tile(i, tm):
    n_ctx_tiles = N_CTX // tm
    per_b = DEC_SEQ // tm
    return jnp.where(i < n_ctx_tiles, 0, 1 + (i - n_ctx_tiles) // per_b)


def _mod_row(mod_ref, k, cond):
    return mod_ref[k, pl.ds(cond, 1), :]


def _dot(a, b):
    return jnp.dot(a, b, preferred_element_type=F32)


def _dot_nt(a, b):
    return lax.dot_general(a, b, (((1,), (1,)), ((), ())), preferred_element_type=F32)


def _rms(x, g):
    ms = jnp.mean(x * x, axis=-1, keepdims=True)
    return x * lax.rsqrt(ms + EPS) * g


def _group_mean(xx, width, group):
    r = lax.broadcasted_iota(jnp.int32, (width, width), 0) // group
    c = lax.broadcasted_iota(jnp.int32, (width, width), 1) // group
    gmat = jnp.where(r == c, 1.0 / group, 0.0).astype(BF16)
    hi = xx.astype(BF16)
    lo = (xx - hi.astype(F32)).astype(BF16)
    return _dot(hi, gmat) + _dot(lo, gmat)


def _const_spec(shape):
    return pl.BlockSpec(shape, lambda *_: (0,) * len(shape))


def _layer_spec(shape, l, **kw):
    return pl.BlockSpec((None,) + shape, lambda *_: (l,) + (0,) * len(shape), **kw)


_MOD_SHAPE = (N_MOD, N_COND, D_MODEL)


def _ada_kernel(c_ref, w_ref, b_ref, o_ref):
    cnd = c_ref[...]
    s = (cnd * jax.nn.sigmoid(cnd)).astype(BF16)
    o_ref[...] = _dot(s, w_ref[...].astype(BF16)) + b_ref[...]


def _ada(conds, w_ada, b_ada):
    return pl.pallas_call(
        _ada_kernel,
        grid=(DEPTH, N_MOD),
        in_specs=[
            pl.BlockSpec((N_COND, D_MODEL), lambda l, n: (0, 0)),
            pl.BlockSpec((None, D_MODEL, D_MODEL), lambda l, n: (l, 0, n)),
            pl.BlockSpec((None, None, 1, D_MODEL), lambda l, n: (l, n, 0, 0)),
        ],
        out_specs=pl.BlockSpec((None, None, N_COND, D_MODEL), lambda l, n: (l, n, 0, 0)),
        out_shape=jax.ShapeDtypeStruct((DEPTH,) + _MOD_SHAPE, F32),
        compiler_params=pltpu.CompilerParams(
            dimension_semantics=("arbitrary", "arbitrary"), vmem_limit_bytes=VMEM_LIMIT),
        name="ada",
    )(conds, w_ada, b_ada.reshape(DEPTH, N_MOD, 1, D_MODEL))


class _WeightStager:
    def __init__(self, jobs, stage_ref, sem_ref):
        self.jobs, self.stage_ref, self.depth = jobs, stage_ref, stage_ref.shape[0]
        self.copies = [
            pltpu.make_async_copy(src, stage_ref.at[k % self.depth], sem_ref.at[k % self.depth])
            for k, (src, _) in enumerate(jobs)]

    def prime(self):
        for k in range(min(self.depth - 1, len(self.jobs))):
            self.copies[k].start()

    def finish(self, k):
        if k + self.depth - 1 < len(self.jobs):
            self.copies[k + self.depth - 1].start()
        self.copies[k].wait()
        self.jobs[k][1](self.stage_ref[k % self.depth].astype(BF16))


def _ffn_kernel(l, sub, tm, n_x, n_o, mix, *refs):
    refs = list(refs)
    x_refs = [refs.pop(0) for _ in range(n_x)]
    ac_ref, att_ref = (refs.pop(0), refs.pop(0)) if mix else (None, None)
    mod_ref, gpre_ref, gpost_ref, wg_hbm, wu_hbm, wd_hbm = [refs.pop(0) for _ in range(6)]
    wo_hbm = refs.pop(0) if mix else None
    o_refs = [refs.pop(0) for _ in range(n_o)]
    wg_ref, wu_ref, wd_ref = [refs.pop(0) for _ in range(3)]
    wo_ref = refs.pop(0) if mix else None
    i = pl.program_id(0)
    is_ctx = i < N_CTX // tm
    cond = _cond_of_tile(i, tm)
    c = FFN_CHUNK
    n_chunks = D_FF // c
    gate = _mod_row(mod_ref, 3 * sub + 2, cond)
    gpost = gpost_ref[pl.ds(sub, 1), :]

    def prep():
        if n_x == 1:
            x = x_refs[0][...]
        else:
            x = jnp.where(is_ctx, x_refs[0][...], x_refs[1][...])
        if mix:
            cat = jnp.concatenate([ac_ref[:, 0:256], att_ref[:, 0:256],
                                   ac_ref[:, 256:512], att_ref[:, 256:512]], axis=1)
            ym = _dot(cat, wo_ref[...])
            x = x + _mod_row(mod_ref, 5, cond) * _rms(ym, gpost_ref[pl.ds(1, 1), :])
        shift = _mod_row(mod_ref, 3 * sub, cond)
        scale = _mod_row(mod_ref, 3 * sub + 1, cond)
        h = (_rms(x, gpre_ref[pl.ds(sub, 1), :]) * (1.0 + scale) + shift).astype(BF16)
        return x, h

    def swiglu(hh, cols):
        g = _dot(hh, wg_ref[:, cols])
        u = _dot(hh, wu_ref[:, cols])
        a = ((g * jax.nn.sigmoid(g)) * u).astype(BF16)
        return _dot(a, wd_ref[cols, :])

    def emit(y):
        if n_o == 1:
            o_refs[0][...] = y
        else:
            @pl.when(is_ctx)
            def _():
                o_refs[0][...] = y

            @pl.when(jnp.logical_not(is_ctx))
            def _():
                o_refs[1][...] = y

    @pl.when(i == 0)
    def _():
        def col_store(dst, j):
            def store(v):
                dst[:, j * c:(j + 1) * c] = v
            return store

        def row_store(j):
            def store(v):
                wd_ref[j * c:(j + 1) * c, :] = v
            return store

        up_jobs = [(src.at[l, :, pl.ds(j * c, c)], col_store(dst, j))
                   for j in range(n_chunks) for src, dst in ((wg_hbm, wg_ref), (wu_hbm, wu_ref))]
        down_jobs = [(wd_hbm.at[l, pl.ds(j * c, c), :], row_store(j)) for j in range(n_chunks)]

        def run(stage_up, stage_down, sem_up, sem_down):
            up = _WeightStager(up_jobs, stage_up, sem_up)
            up.prime()
            if mix:
                def wo_store(j):
                    def store(v):
                        wo_ref[j * c:(j + 1) * c, :] = v
                    return store

                wo = _WeightStager(
                    [(wo_hbm.at[l, pl.ds(j * c, c), :], wo_store(j)) for j in range(D_MODEL // c)],
                    stage_down, sem_down)
                wo.prime()
                for j in range(D_MODEL // c):
                    wo.finish(j)
            down = _WeightStager(down_jobs, stage_down, sem_down)
            down.prime()
            x, h = prep()
            y = None
            for j in range(n_chunks):
                up.finish(2 * j)
                up.finish(2 * j + 1)
                down.finish(j)
                part = swiglu(h, slice(j * c, (j + 1) * c))
                y = part if y is None else y + part
            emit(x + 0.5 * (gate * _rms(y, gpost)))

        pl.run_scoped(run,
                      pltpu.VMEM((2 * FFN_STAGE_SLOTS, D_MODEL, c), F32),
                      pltpu.VMEM((FFN_STAGE_SLOTS, c, D_MODEL), F32),
                      pltpu.SemaphoreType.DMA((2 * FFN_STAGE_SLOTS,)),
                      pltpu.SemaphoreType.DMA((FFN_STAGE_SLOTS,)))

    @pl.when(i != 0)
    def _():
        x, h = prep()
        outs = []
        for s in range(tm // FFN_SUB):
            rows = slice(s * FFN_SUB, (s + 1) * FFN_SUB)
            y = swiglu(h[rows], slice(None))
            outs.append(x[rows] + 0.5 * (gate * _rms(y, gpost)))
        emit(jnp.concatenate(outs, axis=0))


def _ffn(xs, mod, norm_pre, norm_post, wg, wu, wd, l, sub, split_out=False, mix=None):
    tm = TM_FFN
    n_ctx_tiles = N_CTX // tm
    ctx_spec = pl.BlockSpec((tm, D_MODEL), lambda i: (jnp.minimum(i, n_ctx_tiles - 1), 0))
    lat_spec = pl.BlockSpec((tm, D_MODEL), lambda i: (jnp.maximum(i - n_ctx_tiles, 0), 0))
    full_spec = pl.BlockSpec((tm, D_MODEL), lambda i: (i, 0))
    x_specs = [full_spec] if len(xs) == 1 else [ctx_spec, lat_spec]
    if split_out:
        out_specs = [ctx_spec, lat_spec]
        out_shape = [jax.ShapeDtypeStruct((N_CTX, D_MODEL), F32),
                     jax.ShapeDtypeStruct((N_LAT, D_MODEL), F32)]
    else:
        out_specs = [full_spec]
        out_shape = [jax.ShapeDtypeStruct((N_TOK, D_MODEL), F32)]
    hbm = pl.BlockSpec(memory_space=pl.ANY)
    args = list(xs)
    in_specs = list(x_specs)
    scratch = [pltpu.VMEM((D_MODEL, D_FF), BF16), pltpu.VMEM((D_MODEL, D_FF), BF16),
               pltpu.VMEM((D_FF, D_MODEL), BF16)]
    if mix is not None:
        args += [mix[0], mix[1]]
        in_specs += [pl.BlockSpec((tm, 512), lambda i: (i, 0))] * 2
    args += [mod, norm_pre, norm_post, wg, wu, wd]
    in_specs += [_layer_spec(_MOD_SHAPE, l), _layer_spec((3, D_MODEL), l),
                 _layer_spec((3, D_MODEL), l), hbm, hbm, hbm]
    if mix is not None:
        args.append(mix[2])
        in_specs.append(hbm)
        scratch.append(pltpu.VMEM((D_MODEL, D_MODEL), BF16))
    return pl.pallas_call(
        functools.partial(_ffn_kernel, l, sub, tm, len(xs), len(out_specs), mix is not None),
        grid=(N_TOK // tm,),
        in_specs=in_specs,
        out_specs=out_specs,
        out_shape=out_shape,
        scratch_shapes=scratch,
        compiler_params=pltpu.CompilerParams(
            dimension_semantics=("arbitrary",), vmem_limit_bytes=VMEM_LIMIT),
        name=f"ffn{sub}",
    )(*args)


def _swap_blocks(x, blk):
    w = x.shape[-1]
    lane = lax.broadcasted_iota(jnp.int32, x.shape, x.ndim - 1)
    first = (lane % (2 * blk)) < blk
    return jnp.where(first, pltpu.roll(x, w - blk, x.ndim - 1), pltpu.roll(x, blk, x.ndim - 1))


def _rope_tables():
    out = []
    for rot_dim in (DK_DIFF, HD_GQA):
        n_freq = rot_dim // 4
        inv = ROPE_THETA ** (-np.arange(n_freq, dtype=np.float64) / n_freq)
        lane = np.arange(256) % rot_dim
        pos = lane % (rot_dim // 2)
        freq = inv[pos % n_freq]
        sign = np.where(pos < n_freq, -1.0, 1.0)
        coord = np.concatenate([np.arange(GRID_H), np.arange(GRID_W)]).astype(np.float64)
        ang = coord[:, None] * freq[None, :]
        out.append(np.cos(ang))
        out.append(np.sin(ang) * sign[None, :])
    return np.stack(out).astype(np.float32)


def _tile_rope(tab_ref, which, rot_dim, first_grid_row, n_grid_rows, identity):
    col_part = tab_ref[which, GRID_H:GRID_H + GRID_W, :]
    lane = lax.broadcasted_iota(jnp.int32, (1, 256), 1)
    on_row_axis = (lane % rot_dim) < rot_dim // 2
    tab = jnp.concatenate(
        [jnp.where(on_row_axis, tab_ref[which, pl.ds(first_grid_row + r, 1), :], col_part)
         for r in range(n_grid_rows)], axis=0)
    return jnp.where(identity, 1.0 - (which % 2), tab)


def _mixproj_kernel(l, has_prev, x_ref, mod_ref, gpre_ref, w_ref, qg_ref, kg_ref, tab_ref, *rest):
    n_alias = 4 if has_prev else 0
    (cin_ref, qt_ref, k_ref, vt_ref, odk_ref, odv_ref, ogk_ref, ogv_ref, wbf_ref) = rest[n_alias:]
    tm = TM_MIX
    i = pl.program_id(0)
    n_ctx_tiles = N_CTX // tm
    is_ctx = i < n_ctx_tiles
    cond = _cond_of_tile(i, tm)

    @pl.when(i == 0)
    def _():
        wbf_ref[...] = w_ref[...].astype(BF16)

    shift = _mod_row(mod_ref, 3, cond)
    scale = _mod_row(mod_ref, 4, cond)
    h = (_rms(x_ref[...], gpre_ref[pl.ds(1, 1), :]) * (1.0 + scale) + shift).astype(BF16)

    a = _dot(h, wbf_ref[:, 0:768])
    cin_ref[:, 0:256] = a[:, 0:256]
    cin_ref[:, 256:512] = a[:, 256:512] * a[:, 512:768]
    c = _dot(h, wbf_ref[:, 1536:2048])
    cin_ref[:, 512:768] = c[:, 0:256] * jax.nn.sigmoid(c[:, 256:512])

    grid_rows = tm // GRID_W
    first_row = (jnp.maximum(i - n_ctx_tiles, 0) % (DEC_SEQ // tm)) * grid_rows
    cd = _tile_rope(tab_ref, 0, DK_DIFF, first_row, grid_rows, is_ctx)
    sd = _tile_rope(tab_ref, 1, DK_DIFF, first_row, grid_rows, is_ctx)
    cg = _tile_rope(tab_ref, 2, HD_GQA, first_row, grid_rows, is_ctx)
    sg = _tile_rope(tab_ref, 3, HD_GQA, first_row, grid_rows, is_ctx)

    b = _dot(h, wbf_ref[:, 768:1536])
    bq, bk, bv = b[:, 0:256], b[:, 256:512], b[:, 512:768]
    bq_r = (bq * cd + _swap_blocks(bq, DK_DIFF // 4) * sd) * (DK_DIFF ** -0.5 * LOG2E)
    bk_r = bk * cd + _swap_blocks(bk, DK_DIFF // 4) * sd
    qt_ref[0:256, :] = bq_r.T.astype(BF16)

    d = _dot(h, wbf_ref[:, 2048:2560])
    dq, dk, dv = d[:, 0:256], d[:, 256:384], d[:, 384:512]
    dq_n = dq * lax.rsqrt(_group_mean(dq * dq, 256, HD_GQA) + EPS) * qg_ref[l:l + 1, :]
    dk_n = dk * lax.rsqrt(_group_mean(dk * dk, 128, HD_GQA) + EPS) * kg_ref[l:l + 1, :]
    dq_r = (dq_n * cg + _swap_blocks(dq_n, HD_GQA // 4) * sg) * (HD_GQA ** -0.5 * LOG2E)
    dk_r = dk_n * cg[:, 0:128] + _swap_blocks(dk_n, HD_GQA // 4) * sg[:, 0:128]
    qt_ref[256:512, :] = dq_r.T.astype(BF16)

    k_ref[0] = bk_r.astype(BF16)
    k_ref[1] = jnp.concatenate([dk_r, jnp.zeros_like(dk_r)], axis=1).astype(BF16)
    vt_ref[0] = bv.T.astype(BF16)
    vt_ref[1] = jnp.concatenate([dv, jnp.zeros_like(dv)], axis=1).T.astype(BF16)

    @pl.when(is_ctx)
    def _():
        seqs = tm // SEQ
        odk_ref[...] = bk.reshape(seqs, SEQ, 256)
        odv_ref[...] = bv.reshape(seqs, SEQ, 256)
        ogk_ref[...] = dk_n.reshape(seqs, SEQ, 128)
        ogv_ref[...] = dv.reshape(seqs, SEQ, 128)


def _mixproj(x, mod, norm_pre, w_mix_in, qg, kg, tables, prev_own, l):
    tm = TM_MIX
    n_ctx_tiles = N_CTX // tm
    seqs = tm // SEQ

    ctx_row = lambda i: jnp.minimum(i, n_ctx_tiles - 1)
    row = lambda w: pl.BlockSpec((tm, w), lambda i: (i, 0))
    own_spec = lambda w: pl.BlockSpec((seqs, None, SEQ, w), lambda i: (ctx_row(i), l, 0, 0))
    own_shape = lambda w: jax.ShapeDtypeStruct((BATCH, DEPTH, SEQ, w), F32)
    any_spec = pl.BlockSpec(memory_space=pl.ANY)
    aliased = list(prev_own) if prev_own is not None else []
    n_in = 7
    aliases = {n_in + k: 4 + k for k in range(len(aliased))}
    return pl.pallas_call(
        functools.partial(_mixproj_kernel, l, prev_own is not None),
        grid=(N_TOK // tm,),
        in_specs=[
            row(D_MODEL),
            _layer_spec(_MOD_SHAPE, l),
            _layer_spec((3, D_MODEL), l),
            _layer_spec((D_MODEL, MIX_IN), l, pipeline_mode=pl.Buffered(1)),
            _const_spec((DEPTH, 256)),
            _const_spec((DEPTH, 128)),
            _const_spec(tables.shape),
        ] + [any_spec] * len(aliased),
        out_specs=[row(768),
                   pl.BlockSpec((512, tm), lambda i: (0, i)),
                   pl.BlockSpec((2, tm, 256), lambda i: (0, i, 0)),
                   pl.BlockSpec((2, 256, tm), lambda i: (0, 0, i)),
                   own_spec(256), own_spec(256), own_spec(128), own_spec(128)],
        out_shape=[
            jax.ShapeDtypeStruct((N_TOK, 768), F32),
            jax.ShapeDtypeStruct((512, N_TOK), BF16),
            jax.ShapeDtypeStruct((2, N_TOK, 256), BF16),
            jax.ShapeDtypeStruct((2, 256, N_TOK), BF16),
            own_shape(256), own_shape(256), own_shape(128), own_shape(128),
        ],
        scratch_shapes=[pltpu.VMEM((D_MODEL, MIX_IN), BF16)],
        input_output_aliases=aliases,
        compiler_params=pltpu.CompilerParams(
            dimension_semantics=("arbitrary",), vmem_limit_bytes=VMEM_LIMIT),
        name="mixproj",
    )(x, mod, norm_pre, w_mix_in, qg, kg, tables, *aliased)


N_UNITS = 12
ATTN_SLOTS = 3


def _attn_kernel(l, qt_ref, k_ref, vt_ref, kc_ref, vtc_ref, lq1_ref, lk1_ref, lq2_ref, lk2_ref,
                 subln_ref, *rest):
    n = ATTN_SLOTS
    o_ref, qs_ref = rest[-3 - 3 * n], rest[-2 - 3 * n]
    s_refs = rest[-1 - 3 * n:-1 - 2 * n]
    e_refs = rest[-1 - 2 * n:-1 - n]
    il_refs = rest[-1 - n:-1]
    oacc_ref = rest[-1]
    own = k_ref.shape[1]
    half = own // 2

    for u in range(N_UNITS):
        qs_ref[u] = _unit_queries(qt_ref, u)

    def scores(u):
        plane = 0 if u < 8 else 1
        s_ref = s_refs[u % n]
        s_ref[0:half, :] = _dot(k_ref[plane, 0:half, :], qs_ref[u])
        s_ref[half:own, :] = _dot(k_ref[plane, half:, :], qs_ref[u])
        s_ref[own:, :] = _dot(kc_ref[plane], qs_ref[u])

    def softmax(u):
        s = s_refs[u % n][...]
        e = jnp.exp2(s - jnp.max(s, axis=0, keepdims=True))
        il_refs[u % n][...] = 1.0 / jnp.sum(e, axis=0, keepdims=True)
        e_refs[u % n][...] = e.astype(BF16)

    def values(u):
        plane = 0 if u < 8 else 1
        vrow = DV_DIFF * (u // 2) if u < 8 else HD_GQA * ((u - 8) // 2)
        e_ref = e_refs[u % n]
        o = (_dot(vt_ref[plane, vrow:vrow + DV_DIFF, 0:half], e_ref[0:half, :])
             + _dot(vt_ref[plane, vrow:vrow + DV_DIFF, half:], e_ref[half:own, :])
             + _dot(vtc_ref[plane, vrow:vrow + DV_DIFF, :], e_ref[own:, :]))
        oacc_ref[u] = o * il_refs[u % n][...]

    for t in range(N_UNITS + 2):
        if t < N_UNITS:
            scores(t)
        if 1 <= t <= N_UNITS:
            softmax(t - 1)
        if t >= 2:
            values(t - 2)

    _attn_finish(l, [oacc_ref[u] for u in range(N_UNITS)],
                 lq1_ref, lk1_ref, lq2_ref, lk2_ref, subln_ref, o_ref)


def _unit_queries(qt_ref, u, cols=slice(None)):
    if u < 8:
        feat = lax.broadcasted_iota(jnp.int32, (256, 1), 0)
        return jnp.where(feat // DK_DIFF == u, qt_ref[0:256, cols], jnp.zeros((), BF16))
    hd = u - 8
    g = hd // 2
    rows = qt_ref[256 + HD_GQA * hd:256 + HD_GQA * (hd + 1), cols]
    tq = rows.shape[1]
    pieces = ([jnp.zeros((HD_GQA * g, tq), BF16)] if g else []) + [rows]
    pieces.append(jnp.zeros((256 - HD_GQA * (g + 1), tq), BF16))
    return jnp.concatenate(pieces, axis=0)


def _attn_finish(l, outs, lq1_ref, lk1_ref, lq2_ref, lk2_ref, subln_ref, o_ref, rows=slice(None)):
    lam_init = 0.8 - 0.6 * math.exp(-0.3 * l)
    row = slice(l, l + 1)
    lam = (jnp.exp(jnp.sum(lq1_ref[row, :] * lk1_ref[row, :], axis=-1, keepdims=True))
           - jnp.exp(jnp.sum(lq2_ref[row, :] * lk2_ref[row, :], axis=-1, keepdims=True))
           + lam_init)
    heads = []
    for hd in range(4):
        o = outs[2 * hd] - lam * outs[2 * hd + 1]
        nrm = lax.rsqrt(jnp.mean(o * o, axis=0, keepdims=True) + EPS)
        heads.append(o * nrm * (subln_ref[:, l:l + 1] * (1.0 - lam_init)))
    heads += outs[8:12]
    o_ref[rows, :] = jnp.concatenate(heads, axis=0).T.astype(BF16)


def _attn_ctx_kernel(l, qt_ref, k_ref, vt_ref, lq1_ref, lk1_ref, lq2_ref, lk2_ref, subln_ref,
                     o_ref):
    for q in range(qt_ref.shape[1] // SEQ):
        tok = slice(q * SEQ, (q + 1) * SEQ)
        w_diff = jnp.concatenate([_unit_queries(qt_ref, u, tok) for u in range(8)], axis=1)
        w_gqa = jnp.concatenate([_unit_queries(qt_ref, u, tok) for u in range(8, N_UNITS)], axis=1)
        s = jnp.concatenate([_dot(k_ref[0, tok, :], w_diff), _dot(k_ref[1, tok, :], w_gqa)],
                            axis=1)
        e = jnp.exp2(s - jnp.max(s, axis=0, keepdims=True))
        il = 1.0 / jnp.sum(e, axis=0, keepdims=True)
        e = e.astype(BF16)
        outs = []
        for u in range(N_UNITS):
            plane = 0 if u < 8 else 1
            vrow = DV_DIFF * (u // 2) if u < 8 else HD_GQA * ((u - 8) // 2)
            cols = slice(u * SEQ, (u + 1) * SEQ)
            outs.append(_dot(vt_ref[plane, vrow:vrow + DV_DIFF, tok], e[:, cols]) * il[:, cols])
        _attn_finish(l, outs, lq1_ref, lk1_ref, lq2_ref, lk2_ref, subln_ref, o_ref, tok)


def _attn_param_specs():
    vec = _const_spec((DEPTH, DK_DIFF))
    return [vec, vec, vec, vec, _const_spec((DV_DIFF, DEPTH))]


def _attn_scratch(tq, s_len):
    n = ATTN_SLOTS
    return ([pltpu.VMEM((N_UNITS, 256, tq), BF16)]
            + [pltpu.VMEM((s_len, tq), F32)] * n
            + [pltpu.VMEM((s_len, tq), BF16)] * n
            + [pltpu.VMEM((1, tq), F32)] * n
            + [pltpu.VMEM((N_UNITS, DV_DIFF, tq), F32)])


def _attn_ctx(qt, k_ctx, vt_ctx, aparams, l):
    t = CTX_SEQS_PER_STEP * SEQ
    return pl.pallas_call(
        functools.partial(_attn_ctx_kernel, l),
        grid=(N_CTX // t,),
        in_specs=[pl.BlockSpec((512, t), lambda b: (0, b)),
                  pl.BlockSpec((2, t, 256), lambda b: (0, b, 0)),
                  pl.BlockSpec((2, 256, t), lambda b: (0, 0, b))] + _attn_param_specs(),
        out_specs=pl.BlockSpec((t, 512), lambda b: (b, 0)),
        out_shape=jax.ShapeDtypeStruct((N_TOK, 512), BF16),
        compiler_params=pltpu.CompilerParams(
            dimension_semantics=("arbitrary",), vmem_limit_bytes=VMEM_LIMIT),
        name="attn_ctx",
    )(qt, k_ctx, vt_ctx, *aparams)


def _attn_lat(qt, k_all, vt_all, cache_k, cache_vt, att, aparams, l):
    tq = TQ_LAT
    nq = DEC_SEQ // tq
    tile = lambda b, i: N_CTX // tq + b * nq + i
    own = lambda b: N_CTX // DEC_SEQ + b
    return pl.pallas_call(
        functools.partial(_attn_kernel, l),
        grid=(DEC_BATCH, nq),
        in_specs=[pl.BlockSpec((512, tq), lambda b, i: (0, tile(b, i))),
                  pl.BlockSpec((2, DEC_SEQ, 256), lambda b, i: (0, own(b), 0)),
                  pl.BlockSpec((2, 256, DEC_SEQ), lambda b, i: (0, 0, own(b))),
                  pl.BlockSpec((None, 2, PAST_LEN, 256), lambda b, i: (b, 0, 0, 0)),
                  pl.BlockSpec((None, 2, 256, PAST_LEN), lambda b, i: (b, 0, 0, 0))]
        + _attn_param_specs() + [pl.BlockSpec(memory_space=pl.ANY)],
        out_specs=pl.BlockSpec((tq, 512), lambda b, i: (tile(b, i), 0)),
        out_shape=jax.ShapeDtypeStruct((N_TOK, 512), BF16),
        scratch_shapes=_attn_scratch(tq, S_LAT),
        input_output_aliases={10: 0},
        compiler_params=pltpu.CompilerParams(
            dimension_semantics=("arbitrary", "arbitrary"), vmem_limit_bytes=VMEM_LIMIT),
        name="attn_lat",
    )(qt, k_all, vt_all, cache_k, cache_vt, *aparams, att)


def _conv_kernel(l, seq_len, cin_ref, wsc_ref, wdw_ref, bdw_ref, lng_ref, lnb_ref, pw_ref, *rest):
    o_ref, pad_ref = rest[-2], rest[-1]
    t_tile = cin_ref.shape[0]
    n_seq = t_tile // seq_len
    stride = seq_len + CONV_HALO
    chunks_per_seq = seq_len // R_CONV
    rows = R_CONV

    for q in range(n_seq + 1):
        pad_ref[q * stride:q * stride + CONV_HALO, :] = jnp.zeros((CONV_HALO, 512), F32)
    for q in range(n_seq):
        pad_ref[CONV_HALO + q * stride:CONV_HALO + q * stride + seq_len, :] = (
            cin_ref[q * seq_len:(q + 1) * seq_len, 256:768])

    pw = pw_ref[...].astype(BF16)
    bdw = bdw_ref[l:l + 1, :]
    lng = lng_ref[l:l + 1, :]
    lnb = lnb_ref[l:l + 1, :]

    def body(c, carry):
        q = c // chunks_per_seq
        r = c % chunks_per_seq
        base = pl.multiple_of(q * stride + r * rows, 8)
        row0 = pl.multiple_of(c * rows, 8)
        win = pad_ref[pl.ds(base, rows + 2 * CONV_HALO), :]
        win_s = win[:, 0:256]
        win_u = win[:, 256:512]

        s_acc = None
        for k in range(SCONV_K):
            off = CONV_HALO + k - SCONV_K // 2
            term = wsc_ref[pl.ds(k, 1), :] * win_s[off:off + rows]
            s_acc = term if s_acc is None else s_acc + term
        a_out = cin_ref[pl.ds(row0, rows), 0:256] * s_acc

        acc = None
        for res in range(8):
            part = None
            for a in range(-2, 2):
                k = 8 * a + res + CCM_K // 2
                if 0 <= k < CCM_K:
                    lo = CONV_HALO + 8 * a
                    term = wdw_ref[pl.ds(k, 1), :] * win_u[lo:lo + rows + 8]
                    part = term if part is None else part + term
            part = part[res:res + rows]
            acc = part if acc is None else acc + part
        u = acc + bdw
        mu = jnp.mean(u, axis=-1, keepdims=True)
        var = jnp.mean(jnp.square(u - mu), axis=-1, keepdims=True)
        y = (u - mu) * lax.rsqrt(var + EPS) * lng + lnb
        y = y * jax.nn.sigmoid(y)
        c_out = _dot(y.astype(BF16), pw)
        o_ref[pl.ds(row0, rows), 0:256] = a_out.astype(BF16)
        o_ref[pl.ds(row0, rows), 256:512] = c_out.astype(BF16)
        return carry

    lax.fori_loop(0, t_tile // rows, body, 0, unroll=4)


def _conv(cin, prev, seq_len, tile0, n_tiles, cparams, l):
    t = T_CONV
    n_seq = t // seq_len
    vec = _const_spec((DEPTH, 256))
    in_specs = [pl.BlockSpec((t, 768), lambda i: (tile0 + i, 0)),
                _layer_spec((SCONV_K, 256), l), _layer_spec((CCM_K, 256), l), vec, vec, vec,
                _layer_spec((256, 256), l)]
    args = [cin, *cparams]
    aliases = {}
    if prev is not None:
        in_specs.append(pl.BlockSpec(memory_space=pl.ANY))
        args.append(prev)
        aliases = {len(args) - 1: 0}
    return pl.pallas_call(
        functools.partial(_conv_kernel, l, seq_len),
        grid=(n_tiles,),
        in_specs=in_specs,
        out_specs=pl.BlockSpec((t, 512), lambda i: (tile0 + i, 0)),
        out_shape=jax.ShapeDtypeStruct((N_TOK, 512), BF16),
        scratch_shapes=[pltpu.VMEM((n_seq * (seq_len + CONV_HALO) + CONV_HALO, 512), F32)],
        input_output_aliases=aliases,
        compiler_params=pltpu.CompilerParams(
            dimension_semantics=("arbitrary",), vmem_limit_bytes=VMEM_LIMIT),
        name=f"conv{seq_len}",
    )(*args)


def _cache_planes(cache_diff_k, cache_diff_v, cache_gqa_k, cache_gqa_v, l):
    pad128 = lambda a: jnp.pad(a.reshape(DEC_BATCH, PAST_LEN, 128), ((0, 0), (0, 0), (0, 128)))
    keys = jnp.stack([cache_diff_k[:, l].reshape(DEC_BATCH, PAST_LEN, 256),
                      pad128(cache_gqa_k[:, l])], axis=1).astype(BF16)
    vals = jnp.stack([cache_diff_v[:, l].reshape(DEC_BATCH, PAST_LEN, 256),
                      pad128(cache_gqa_v[:, l])], axis=1).astype(BF16)
    return keys, jnp.swapaxes(vals, 2, 3)


def kernel(x_prompt, x_sample, cache_diff_k, cache_diff_v, cache_gqa_k, cache_gqa_v, c, c_ctx, w_ada, b_ada, norm_pre, norm_post, ffn1_gate, ffn1_up, ffn1_down, ffn2_gate, ffn2_up, ffn2_down, w_mix_in, w_mix_out, sconv_w, diff_lq1, diff_lk1, diff_lq2, diff_lk2, diff_subln, ccm_dw_w, ccm_dw_b, ccm_ln_g, ccm_ln_b, ccm_pw, gqa_qnorm, gqa_knorm):
    conds = jnp.concatenate(
        [c_ctx[None, :], c, jnp.zeros((N_COND - 1 - DEC_BATCH, D_MODEL), F32)], axis=0)
    mod = _ada(conds, w_ada, b_ada)
    tables = jnp.asarray(_rope_tables())
    qg = jnp.tile(gqa_qnorm, (1, 4))
    kg = jnp.tile(gqa_knorm, (1, 2))
    aparams = (diff_lq1, diff_lk1, diff_lq2, diff_lk2, diff_subln.T)
    cparams = (sconv_w, ccm_dw_w, ccm_dw_b, ccm_ln_g, ccm_ln_b, ccm_pw)

    xs = (x_prompt.reshape(N_CTX, D_MODEL), x_sample.reshape(N_LAT, D_MODEL))
    own = None
    for l in range(DEPTH):
        (x,) = _ffn(xs, mod, norm_pre, norm_post, ffn1_gate, ffn1_up, ffn1_down, l, 0)

        cin, qt, k_all, vt_all, *own = _mixproj(
            x, mod, norm_pre, w_mix_in, qg, kg, tables, own, l)

        cache_k, cache_vt = _cache_planes(cache_diff_k, cache_diff_v, cache_gqa_k, cache_gqa_v, l)
        att = _attn_ctx(qt, k_all, vt_all, aparams, l)
        att = _attn_lat(qt, k_all, vt_all, cache_k, cache_vt, att, aparams, l)

        ac = _conv(cin, None, SEQ, 0, N_CTX // T_CONV, cparams, l)
        ac = _conv(cin, ac, DEC_SEQ, N_CTX // T_CONV, N_LAT // T_CONV, cparams, l)

        xs = _ffn((x,), mod, norm_pre, norm_post, ffn2_gate, ffn2_up, ffn2_down, l, 2,
                  split_out=(l == DEPTH - 1), mix=(ac, att, w_mix_out))

    y_prompt, y_sample = xs
    odk, odv, ogk, ogv = own
    return (y_prompt.reshape(BATCH, SEQ, D_MODEL),
            y_sample.reshape(DEC_BATCH, DEC_SEQ, D_MODEL),
            odk.reshape(BATCH, DEPTH, SEQ, 4, 2, DK_DIFF),
            odv.reshape(BATCH, DEPTH, SEQ, 4, DV_DIFF),
            ogk.reshape(BATCH, DEPTH, SEQ, 2, HD_GQA),
            ogv.reshape(BATCH, DEPTH, SEQ, 2, HD_GQA))
```

```python
import functools
import math

import numpy as np

import jax
import jax.numpy as jnp
from jax import lax
from jax.experimental import pallas as pl
from jax.experimental.pallas import tpu as pltpu

F32 = jnp.float32
BF16 = jnp.bfloat16

D_MODEL = 1024
BATCH = 16
SEQ = 256
DEPTH = 2
DEC_BATCH = 2
DEC_SEQ = 2048
PAST_LEN = 256
GRID_W = 64
GRID_H = DEC_SEQ // GRID_W
ROPE_THETA = 10000.0
EPS = 1e-6
N_MOD = 9
D_FF = 2816
DK_DIFF = 32
DV_DIFF = 64
HD_GQA = 64
CCM_K = 31
SCONV_K = 3
MIX_IN = 2560

N_CTX = BATCH * SEQ
N_LAT = DEC_BATCH * DEC_SEQ
N_TOK = N_CTX + N_LAT
N_COND = 8
S_LAT = DEC_SEQ + PAST_LEN

LOG2E = math.log2(math.e)

VMEM_LIMIT = 56 * 1024 * 1024

TM_FFN = 512
FFN_SUB = 256
FFN_CHUNK = 256
FFN_STAGE_SLOTS = 3
TM_MIX = 1024
ADA_VECS = 3
TQ_LAT = 256
CTX_SEQS_PER_STEP = 4
T_CONV = 2048
R_CONV = 64
CONV_HALO = 16


def _cond_of_tile(i, tm):
    n_ctx_tiles = N_CTX // tm
    per_b = DEC_SEQ // tm
    return jnp.where(i < n_ctx_tiles, 0, 1 + (i - n_ctx_tiles) // per_b)


def _mod_row(mod_ref, k, cond):
    return mod_ref[k, pl.ds(cond, 1), :]


def _dot(a, b):
    return jnp.dot(a, b, preferred_element_type=F32)


def _dot_nt(a, b):
    return lax.dot_general(a, b, (((1,), (1,)), ((), ())), preferred_element_type=F32)


def _rms(x, g):
    ms = jnp.mean(x * x, axis=-1, keepdims=True)
    return x * lax.rsqrt(ms + EPS) * g


def _group_mean(xx, width, group):
    r = lax.broadcasted_iota(jnp.int32, (width, width), 0) // group
    c = lax.broadcasted_iota(jnp.int32, (width, width), 1) // group
    gmat = jnp.where(r == c, 1.0 / group, 0.0).astype(BF16)
    hi = xx.astype(BF16)
    lo = (xx - hi.astype(F32)).astype(BF16)
    return _dot(hi, gmat) + _dot(lo, gmat)


def _const_spec(shape):
    return pl.BlockSpec(shape, lambda *_: (0,) * len(shape))


def _layer_spec(shape, l, **kw):
    return pl.BlockSpec((None,) + shape, lambda *_: (l,) + (0,) * len(shape), **kw)


_MOD_SHAPE = (N_MOD, N_COND, D_MODEL)


def _ada_kernel(c_ref, w_ref, b_ref, o_ref):
    cnd = c_ref[...]
    s = (cnd * jax.nn.sigmoid(cnd)).astype(BF16)
    for k in range(ADA_VECS):
        cols = slice(k * D_MODEL, (k + 1) * D_MODEL)
        o_ref[k] = _dot(s, w_ref[:, cols].astype(BF16)) + b_ref[k]


def _ada(conds, w_ada, b_ada):
    return pl.pallas_call(
        _ada_kernel,
        grid=(DEPTH, N_MOD // ADA_VECS),
        in_specs=[
            pl.BlockSpec((N_COND, D_MODEL), lambda l, n: (0, 0)),
            pl.BlockSpec((None, D_MODEL, ADA_VECS * D_MODEL), lambda l, n: (l, 0, n)),
            pl.BlockSpec((None, ADA_VECS, 1, D_MODEL), lambda l, n: (l, n, 0, 0)),
        ],
        out_specs=pl.BlockSpec((None, ADA_VECS, N_COND, D_MODEL), lambda l, n: (l, n, 0, 0)),
        out_shape=jax.ShapeDtypeStruct((DEPTH,) + _MOD_SHAPE, F32),
        compiler_params=pltpu.CompilerParams(
            dimension_semantics=("arbitrary", "arbitrary"), vmem_limit_bytes=VMEM_LIMIT),
        name="ada",
    )(conds, w_ada, b_ada.reshape(DEPTH, N_MOD, 1, D_MODEL))


class _WeightStager:
    def __init__(self, jobs, stage_ref, sem_ref):
        self.jobs, self.stage_ref, self.depth = jobs, stage_ref, stage_ref.shape[0]
        self.copies = [
            pltpu.make_async_copy(src, stage_ref.at[k % self.depth], sem_ref.at[k % self.depth])
            for k, (src, _) in enumerate(jobs)]

    def prime(self):
        for k in range(min(self.depth - 1, len(self.jobs))):
            self.copies[k].start()

    def finish(self, k):
        if k + self.depth - 1 < len(self.jobs):
            self.copies[k + self.depth - 1].start()
        self.copies[k].wait()
        self.jobs[k][1](self.stage_ref[k % self.depth].astype(BF16))


def _ffn_kernel(l, sub, tm, n_x, n_o, mix, *refs):
    refs = list(refs)
    x_refs = [refs.pop(0) for _ in range(n_x)]
    ac_ref, att_ref = (refs.pop(0), refs.pop(0)) if mix else (None, None)
    mod_ref, gpre_ref, gpost_ref, wg_hbm, wu_hbm, wd_hbm = [refs.pop(0) for _ in range(6)]
    wo_hbm = refs.pop(0) if mix else None
    o_refs = [refs.pop(0) for _ in range(n_o)]
    wg_ref, wu_ref, wd_ref = [refs.pop(0) for _ in range(3)]
    wo_ref = refs.pop(0) if mix else None
    i = pl.program_id(0)
    is_ctx = i < N_CTX // tm
    cond = _cond_of_tile(i, tm)
    c = FFN_CHUNK
    n_chunks = D_FF // c
    gate = _mod_row(mod_ref, 3 * sub + 2, cond)
    gpost = gpost_ref[pl.ds(sub, 1), :]

    def prep():
        if n_x == 1:
            x = x_refs[0][...]
        else:
            x = jnp.where(is_ctx, x_refs[0][...], x_refs[1][...])
        if mix:
            cat = jnp.concatenate([ac_ref[:, 0:256], att_ref[:, 0:256],
                                   ac_ref[:, 256:512], att_ref[:, 256:512]], axis=1)
            ym = _dot(cat, wo_ref[...])
            x = x + _mod_row(mod_ref, 5, cond) * _rms(ym, gpost_ref[pl.ds(1, 1), :])
        shift = _mod_row(mod_ref, 3 * sub, cond)
        scale = _mod_row(mod_ref, 3 * sub + 1, cond)
        h = (_rms(x, gpre_ref[pl.ds(sub, 1), :]) * (1.0 + scale) + shift).astype(BF16)
        return x, h

    def swiglu(hh, cols):
        g = _dot(hh, wg_ref[:, cols])
        u = _dot(hh, wu_ref[:, cols])
        a = ((g * jax.nn.sigmoid(g)) * u).astype(BF16)
        return _dot(a, wd_ref[cols, :])

    def emit(y):
        if n_o == 1:
            o_refs[0][...] = y
        else:
            @pl.when(is_ctx)
            def _():
                o_refs[0][...] = y

            @pl.when(jnp.logical_not(is_ctx))
            def _():
                o_refs[1][...] = y

    @pl.when(i == 0)
    def _():
        def col_store(dst, j):
            def store(v):
                dst[:, j * c:(j + 1) * c] = v
            return store

        def row_store(j):
            def store(v):
                wd_ref[j * c:(j + 1) * c, :] = v
            return store

        up_jobs = [(src.at[l, :, pl.ds(j * c, c)], col_store(dst, j))
                   for j in range(n_chunks) for src, dst in ((wg_hbm, wg_ref), (wu_hbm, wu_ref))]
        down_jobs = [(wd_hbm.at[l, pl.ds(j * c, c), :], row_store(j)) for j in range(n_chunks)]

        def run(stage_up, stage_down, sem_up, sem_down):
            up = _WeightStager(up_jobs, stage_up, sem_up)
            up.prime()
            if mix:
                def wo_store(j):
                    def store(v):
                        wo_ref[j * c:(j + 1) * c, :] = v
                    return store

                wo = _WeightStager(
                    [(wo_hbm.at[l, pl.ds(j * c, c), :], wo_store(j)) for j in range(D_MODEL // c)],
                    stage_down, sem_down)
                wo.prime()
                for j in range(D_MODEL // c):
                    wo.finish(j)
            down = _WeightStager(down_jobs, stage_down, sem_down)
            down.prime()
            x, h = prep()
            y = None
            for j in range(n_chunks):
                up.finish(2 * j)
                up.finish(2 * j + 1)
                down.finish(j)
                part = swiglu(h, slice(j * c, (j + 1) * c))
                y = part if y is None else y + part
            emit(x + 0.5 * (gate * _rms(y, gpost)))

        pl.run_scoped(run,
                      pltpu.VMEM((2 * FFN_STAGE_SLOTS, D_MODEL, c), F32),
                      pltpu.VMEM((FFN_STAGE_SLOTS, c, D_MODEL), F32),
                      pltpu.SemaphoreType.DMA((2 * FFN_STAGE_SLOTS,)),
                      pltpu.SemaphoreType.DMA((FFN_STAGE_SLOTS,)))

    @pl.when(i != 0)
    def _():
        x, h = prep()
        outs = []
        for s in range(tm // FFN_SUB):
            rows = slice(s * FFN_SUB, (s + 1) * FFN_SUB)
            y = swiglu(h[rows], slice(None))
            outs.append(x[rows] + 0.5 * (gate * _rms(y, gpost)))
        emit(jnp.concatenate(outs, axis=0))


def _ffn(xs, mod, norm_pre, norm_post, wg, wu, wd, l, sub, split_out=False, mix=None):
    tm = TM_FFN
    n_ctx_tiles = N_CTX // tm
    ctx_spec = pl.BlockSpec((tm, D_MODEL), lambda i: (jnp.minimum(i, n_ctx_tiles - 1), 0))
    lat_spec = pl.BlockSpec((tm, D_MODEL), lambda i: (jnp.maximum(i - n_ctx_tiles, 0), 0))
    full_spec = pl.BlockSpec((tm, D_MODEL), lambda i: (i, 0))
    x_specs = [full_spec] if len(xs) == 1 else [ctx_spec, lat_spec]
    if split_out:
        out_specs = [ctx_spec, lat_spec]
        out_shape = [jax.ShapeDtypeStruct((N_CTX, D_MODEL), F32),
                     jax.ShapeDtypeStruct((N_LAT, D_MODEL), F32)]
    else:
        out_specs = [full_spec]
        out_shape = [jax.ShapeDtypeStruct((N_TOK, D_MODEL), F32)]
    hbm = pl.BlockSpec(memory_space=pl.ANY)
    args = list(xs)
    in_specs = list(x_specs)
    scratch = [pltpu.VMEM((D_MODEL, D_FF), BF16), pltpu.VMEM((D_MODEL, D_FF), BF16),
               pltpu.VMEM((D_FF, D_MODEL), BF16)]
    if mix is not None:
        args += [mix[0], mix[1]]
        in_specs += [pl.BlockSpec((tm, 512), lambda i: (i, 0))] * 2
    args += [mod, norm_pre, norm_post, wg, wu, wd]
    in_specs += [_layer_spec(_MOD_SHAPE, l), _layer_spec((3, D_MODEL), l),
                 _layer_spec((3, D_MODEL), l), hbm, hbm, hbm]
    if mix is not None:
        args.append(mix[2])
        in_specs.append(hbm)
        scratch.append(pltpu.VMEM((D_MODEL, D_MODEL), BF16))
    return pl.pallas_call(
        functools.partial(_ffn_kernel, l, sub, tm, len(xs), len(out_specs), mix is not None),
        grid=(N_TOK // tm,),
        in_specs=in_specs,
        out_specs=out_specs,
        out_shape=out_shape,
        scratch_shapes=scratch,
        compiler_params=pltpu.CompilerParams(
            dimension_semantics=("arbitrary",), vmem_limit_bytes=VMEM_LIMIT),
        name=f"ffn{sub}",
    )(*args)


def _swap_blocks(x, blk):
    w = x.shape[-1]
    lane = lax.broadcasted_iota(jnp.int32, x.shape, x.ndim - 1)
    first = (lane % (2 * blk)) < blk
    return jnp.where(first, pltpu.roll(x, w - blk, x.ndim - 1), pltpu.roll(x, blk, x.ndim - 1))


def _rope_tables():
    out = []
    for rot_dim in (DK_DIFF, HD_GQA):
        n_freq = rot_dim // 4
        inv = ROPE_THETA ** (-np.arange(n_freq, dtype=np.float64) / n_freq)
        lane = np.arange(256) % rot_dim
        pos = lane % (rot_dim // 2)
        freq = inv[pos % n_freq]
        sign = np.where(pos < n_freq, -1.0, 1.0)
        coord = np.concatenate([np.arange(GRID_H), np.arange(GRID_W)]).astype(np.float64)
        ang = coord[:, None] * freq[None, :]
        out.append(np.cos(ang))
        out.append(np.sin(ang) * sign[None, :])
    return np.stack(out).astype(np.float32)


def _tile_rope(tab_ref, which, rot_dim, first_grid_row, n_grid_rows, identity):
    col_part = tab_ref[which, GRID_H:GRID_H + GRID_W, :]
    lane = lax.broadcasted_iota(jnp.int32, (1, 256), 1)
    on_row_axis = (lane % rot_dim) < rot_dim // 2
    tab = jnp.concatenate(
        [jnp.where(on_row_axis, tab_ref[which, pl.ds(first_grid_row + r, 1), :], col_part)
         for r in range(n_grid_rows)], axis=0)
    return jnp.where(identity, 1.0 - (which % 2), tab)


def _mixproj_kernel(l, has_prev, x_ref, mod_ref, gpre_ref, w_ref, qg_ref, kg_ref, tab_ref, *rest):
    n_alias = 4 if has_prev else 0
    (cin_ref, qt_ref, k_ref, vt_ref, odk_ref, odv_ref, ogk_ref, ogv_ref, wbf_ref) = rest[n_alias:]
    tm = TM_MIX
    i = pl.program_id(0)
    n_ctx_tiles = N_CTX // tm
    is_ctx = i < n_ctx_tiles
    cond = _cond_of_tile(i, tm)

    @pl.when(i == 0)
    def _():
        wbf_ref[...] = w_ref[...].astype(BF16)

    shift = _mod_row(mod_ref, 3, cond)
    scale = _mod_row(mod_ref, 4, cond)
    h = (_rms(x_ref[...], gpre_ref[pl.ds(1, 1), :]) * (1.0 + scale) + shift).astype(BF16)

    a = _dot(h, wbf_ref[:, 0:768])
    cin_ref[:, 0:256] = a[:, 0:256]
    cin_ref[:, 256:512] = a[:, 256:512] * a[:, 512:768]
    c = _dot(h, wbf_ref[:, 1536:2048])
    cin_ref[:, 512:768] = c[:, 0:256] * jax.nn.sigmoid(c[:, 256:512])

    grid_rows = tm // GRID_W
    first_row = (jnp.maximum(i - n_ctx_tiles, 0) % (DEC_SEQ // tm)) * grid_rows
    cd = _tile_rope(tab_ref, 0, DK_DIFF, first_row, grid_rows, is_ctx)
    sd = _tile_rope(tab_ref, 1, DK_DIFF, first_row, grid_rows, is_ctx)
    cg = _tile_rope(tab_ref, 2, HD_GQA, first_row, grid_rows, is_ctx)
    sg = _tile_rope(tab_ref, 3, HD_GQA, first_row, grid_rows, is_ctx)

    b = _dot(h, wbf_ref[:, 768:1536])
    bq, bk, bv = b[:, 0:256], b[:, 256:512], b[:, 512:768]
    bq_r = (bq * cd + _swap_blocks(bq, DK_DIFF // 4) * sd) * (DK_DIFF ** -0.5 * LOG2E)
    bk_r = bk * cd + _swap_blocks(bk, DK_DIFF // 4) * sd
    qt_ref[0:256, :] = bq_r.T.astype(BF16)

    d = _dot(h, wbf_ref[:, 2048:2560])
    dq, dk, dv = d[:, 0:256], d[:, 256:384], d[:, 384:512]
    dq_n = dq * lax.rsqrt(_group_mean(dq * dq, 256, HD_GQA) + EPS) * qg_ref[l:l + 1, :]
    dk_n = dk * lax.rsqrt(_group_mean(dk * dk, 128, HD_GQA) + EPS) * kg_ref[l:l + 1, :]
    dq_r = (dq_n * cg + _swap_blocks(dq_n, HD_GQA // 4) * sg) * (HD_GQA ** -0.5 * LOG2E)
    dk_r = dk_n * cg[:, 0:128] + _swap_blocks(dk_n, HD_GQA // 4) * sg[:, 0:128]
    qt_ref[256:512, :] = dq_r.T.astype(BF16)

    k_ref[0] = bk_r.astype(BF16)
    k_ref[1] = jnp.concatenate([dk_r, jnp.zeros_like(dk_r)], axis=1).astype(BF16)
    vt_ref[0] = bv.T.astype(BF16)
    vt_ref[1] = jnp.concatenate([dv, jnp.zeros_like(dv)], axis=1).T.astype(BF16)

    @pl.when(is_ctx)
    def _():
        seqs = tm // SEQ
        odk_ref[...] = bk.reshape(seqs, SEQ, 256)
        odv_ref[...] = bv.reshape(seqs, SEQ, 256)
        ogk_ref[...] = dk_n.reshape(seqs, SEQ, 128)
        ogv_ref[...] = dv.reshape(seqs, SEQ, 128)


def _mixproj(x, mod, norm_pre, w_mix_in, qg, kg, tables, prev_own, l):
    tm = TM_MIX
    n_ctx_tiles = N_CTX // tm
    seqs = tm // SEQ

    ctx_row = lambda i: jnp.minimum(i, n_ctx_tiles - 1)
    row = lambda w: pl.BlockSpec((tm, w), lambda i: (i, 0))
    own_spec = lambda w: pl.BlockSpec((seqs, None, SEQ, w), lambda i: (ctx_row(i), l, 0, 0))
    own_shape = lambda w: jax.ShapeDtypeStruct((BATCH, DEPTH, SEQ, w), F32)
    any_spec = pl.BlockSpec(memory_space=pl.ANY)
    aliased = list(prev_own) if prev_own is not None else []
    n_in = 7
    aliases = {n_in + k: 4 + k for k in range(len(aliased))}
    return pl.pallas_call(
        functools.partial(_mixproj_kernel, l, prev_own is not None),
        grid=(N_TOK // tm,),
        in_specs=[
            row(D_MODEL),
            _layer_spec(_MOD_SHAPE, l),
            _layer_spec((3, D_MODEL), l),
            _layer_spec((D_MODEL, MIX_IN), l, pipeline_mode=pl.Buffered(1)),
            _const_spec((DEPTH, 256)),
            _const_spec((DEPTH, 128)),
            _const_spec(tables.shape),
        ] + [any_spec] * len(aliased),
        out_specs=[row(768),
                   pl.BlockSpec((512, tm), lambda i: (0, i)),
                   pl.BlockSpec((2, tm, 256), lambda i: (0, i, 0)),
                   pl.BlockSpec((2, 256, tm), lambda i: (0, 0, i)),
                   own_spec(256), own_spec(256), own_spec(128), own_spec(128)],
        out_shape=[
            jax.ShapeDtypeStruct((N_TOK, 768), F32),
            jax.ShapeDtypeStruct((512, N_TOK), BF16),
            jax.ShapeDtypeStruct((2, N_TOK, 256), BF16),
            jax.ShapeDtypeStruct((2, 256, N_TOK), BF16),
            own_shape(256), own_shape(256), own_shape(128), own_shape(128),
        ],
        scratch_shapes=[pltpu.VMEM((D_MODEL, MIX_IN), BF16)],
        input_output_aliases=aliases,
        compiler_params=pltpu.CompilerParams(
            dimension_semantics=("arbitrary",), vmem_limit_bytes=VMEM_LIMIT),
        name="mixproj",
    )(x, mod, norm_pre, w_mix_in, qg, kg, tables, *aliased)


N_UNITS = 12
ATTN_SLOTS = 4


def _attn_kernel(l, qt_ref, k_ref, vt_ref, kc_ref, vtc_ref, lq1_ref, lk1_ref, lq2_ref, lk2_ref,
                 subln_ref, *rest):
    n = ATTN_SLOTS
    o_ref, qs_ref = rest[-3 - 3 * n], rest[-2 - 3 * n]
    s_refs = rest[-1 - 3 * n:-1 - 2 * n]
    e_refs = rest[-1 - 2 * n:-1 - n]
    il_refs = rest[-1 - n:-1]
    oacc_ref = rest[-1]
    own = k_ref.shape[1]
    half = own // 2

    for u in range(N_UNITS):
        qs_ref[u] = _unit_queries(qt_ref, u)

    def scores(u):
        plane = 0 if u < 8 else 1
        s_ref = s_refs[u % n]
        s_ref[0:half, :] = _dot(k_ref[plane, 0:half, :], qs_ref[u])
        s_ref[half:own, :] = _dot(k_ref[plane, half:, :], qs_ref[u])
        s_ref[own:, :] = _dot(kc_ref[plane], qs_ref[u])

    def softmax(u):
        s = s_refs[u % n][...]
        e = jnp.exp2(s - jnp.max(s, axis=0, keepdims=True))
        il_refs[u % n][...] = 1.0 / jnp.sum(e, axis=0, keepdims=True)
        e_refs[u % n][...] = e.astype(BF16)

    def values(u):
        plane = 0 if u < 8 else 1
        vrow = DV_DIFF * (u // 2) if u < 8 else HD_GQA * ((u - 8) // 2)
        e_ref = e_refs[u % n]
        o = (_dot(vt_ref[plane, vrow:vrow + DV_DIFF, 0:half], e_ref[0:half, :])
             + _dot(vt_ref[plane, vrow:vrow + DV_DIFF, half:], e_ref[half:own, :])
             + _dot(vtc_ref[plane, vrow:vrow + DV_DIFF, :], e_ref[own:, :]))
        oacc_ref[u] = o * il_refs[u % n][...]

    lag = ATTN_SLOTS // 2
    for t in range(N_UNITS + 2 * lag):
        if t < N_UNITS:
            scores(t)
        if lag <= t < N_UNITS + lag:
            softmax(t - lag)
        if t >= 2 * lag:
            values(t - 2 * lag)

    _attn_finish(l, [oacc_ref[u] for u in range(N_UNITS)],
                 lq1_ref, lk1_ref, lq2_ref, lk2_ref, subln_ref, o_ref)


def _unit_queries(qt_ref, u, cols=slice(None)):
    if u < 8:
        feat = lax.broadcasted_iota(jnp.int32, (256, 1), 0)
        return jnp.where(feat // DK_DIFF == u, qt_ref[0:256, cols], jnp.zeros((), BF16))
    hd = u - 8
    g = hd // 2
    rows = qt_ref[256 + HD_GQA * hd:256 + HD_GQA * (hd + 1), cols]
    tq = rows.shape[1]
    pieces = ([jnp.zeros((HD_GQA * g, tq), BF16)] if g else []) + [rows]
    pieces.append(jnp.zeros((256 - HD_GQA * (g + 1), tq), BF16))
    return jnp.concatenate(pieces, axis=0)


def _attn_finish(l, outs, lq1_ref, lk1_ref, lq2_ref, lk2_ref, subln_ref, o_ref, rows=slice(None)):
    lam_init = 0.8 - 0.6 * math.exp(-0.3 * l)
    row = slice(l, l + 1)
    lam = (jnp.exp(jnp.sum(lq1_ref[row, :] * lk1_ref[row, :], axis=-1, keepdims=True))
           - jnp.exp(jnp.sum(lq2_ref[row, :] * lk2_ref[row, :], axis=-1, keepdims=True))
           + lam_init)
    heads = []
    for hd in range(4):
        o = outs[2 * hd] - lam * outs[2 * hd + 1]
        nrm = lax.rsqrt(jnp.mean(o * o, axis=0, keepdims=True) + EPS)
        heads.append(o * nrm * (subln_ref[:, l:l + 1] * (1.0 - lam_init)))
    heads += outs[8:12]
    o_ref[rows, :] = jnp.concatenate(heads, axis=0).T.astype(BF16)


def _attn_ctx_kernel(l, qt_ref, k_ref, vt_ref, lq1_ref, lk1_ref, lq2_ref, lk2_ref, subln_ref,
                     o_ref):
    for q in range(qt_ref.shape[1] // SEQ):
        tok = slice(q * SEQ, (q + 1) * SEQ)
        w_diff = jnp.concatenate([_unit_queries(qt_ref, u, tok) for u in range(8)], axis=1)
        w_gqa = jnp.concatenate([_unit_queries(qt_ref, u, tok) for u in range(8, N_UNITS)], axis=1)
        s = jnp.concatenate([_dot(k_ref[0, tok, :], w_diff), _dot(k_ref[1, tok, :], w_gqa)],
                            axis=1)
        e = jnp.exp2(s - jnp.max(s, axis=0, keepdims=True))
        il = 1.0 / jnp.sum(e, axis=0, keepdims=True)
        e = e.astype(BF16)
        outs = []
        for u in range(N_UNITS):
            plane = 0 if u < 8 else 1
            vrow = DV_DIFF * (u // 2) if u < 8 else HD_GQA * ((u - 8) // 2)
            cols = slice(u * SEQ, (u + 1) * SEQ)
            outs.append(_dot(vt_ref[plane, vrow:vrow + DV_DIFF, tok], e[:, cols]) * il[:, cols])
        _attn_finish(l, outs, lq1_ref, lk1_ref, lq2_ref, lk2_ref, subln_ref, o_ref, tok)


def _attn_param_specs():
    vec = _const_spec((DEPTH, DK_DIFF))
    return [vec, vec, vec, vec, _const_spec((DV_DIFF, DEPTH))]


def _attn_scratch(tq, s_len):
    n = ATTN_SLOTS
    return ([pltpu.VMEM((N_UNITS, 256, tq), BF16)]
            + [pltpu.VMEM((s_len, tq), F32)] * n
            + [pltpu.VMEM((s_len, tq), BF16)] * n
            + [pltpu.VMEM((1, tq), F32)] * n
            + [pltpu.VMEM((N_UNITS, DV_DIFF, tq), F32)])


def _attn_ctx(qt, k_ctx, vt_ctx, aparams, l):
    t = CTX_SEQS_PER_STEP * SEQ
    return pl.pallas_call(
        functools.partial(_attn_ctx_kernel, l),
        grid=(N_CTX // t,),
        in_specs=[pl.BlockSpec((512, t), lambda b: (0, b)),
                  pl.BlockSpec((2, t, 256), lambda b: (0, b, 0)),
                  pl.BlockSpec((2, 256, t), lambda b: (0, 0, b))] + _attn_param_specs(),
        out_specs=pl.BlockSpec((t, 512), lambda b: (b, 0)),
        out_shape=jax.ShapeDtypeStruct((N_TOK, 512), BF16),
        compiler_params=pltpu.CompilerParams(
            dimension_semantics=("arbitrary",), vmem_limit_bytes=VMEM_LIMIT),
        name="attn_ctx",
    )(qt, k_ctx, vt_ctx, *aparams)


def _attn_lat(qt, k_all, vt_all, cache_k, cache_vt, att, aparams, l):
    tq = TQ_LAT
    nq = DEC_SEQ // tq
    tile = lambda b, i: N_CTX // tq + b * nq + i
    own = lambda b: N_CTX // DEC_SEQ + b
    return pl.pallas_call(
        functools.partial(_attn_kernel, l),
        grid=(DEC_BATCH, nq),
        in_specs=[pl.BlockSpec((512, tq), lambda b, i: (0, tile(b, i))),
                  pl.BlockSpec((2, DEC_SEQ, 256), lambda b, i: (0, own(b), 0)),
                  pl.BlockSpec((2, 256, DEC_SEQ), lambda b, i: (0, 0, own(b))),
                  pl.BlockSpec((None, 2, PAST_LEN, 256), lambda b, i: (b, 0, 0, 0)),
                  pl.BlockSpec((None, 2, 256, PAST_LEN), lambda b, i: (b, 0, 0, 0))]
        + _attn_param_specs() + [pl.BlockSpec(memory_space=pl.ANY)],
        out_specs=pl.BlockSpec((tq, 512), lambda b, i: (tile(b, i), 0)),
        out_shape=jax.ShapeDtypeStruct((N_TOK, 512), BF16),
        scratch_shapes=_attn_scratch(tq, S_LAT),
        input_output_aliases={10: 0},
        compiler_params=pltpu.CompilerParams(
            dimension_semantics=("arbitrary", "arbitrary"), vmem_limit_bytes=VMEM_LIMIT),
        name="attn_lat",
    )(qt, k_all, vt_all, cache_k, cache_vt, *aparams, att)


def _conv_kernel(l, seq_len, cin_ref, wsc_ref, wdw_ref, bdw_ref, lng_ref, lnb_ref, pw_ref, *rest):
    o_ref, pad_ref = rest[-2], rest[-1]
    t_tile = cin_ref.shape[0]
    n_seq = t_tile // seq_len
    stride = seq_len + CONV_HALO
    chunks_per_seq = seq_len // R_CONV
    rows = R_CONV

    for q in range(n_seq + 1):
        pad_ref[q * stride:q * stride + CONV_HALO, :] = jnp.zeros((CONV_HALO, 512), F32)
    for q in range(n_seq):
        pad_ref[CONV_HALO + q * stride:CONV_HALO + q * stride + seq_len, :] = (
            cin_ref[q * seq_len:(q + 1) * seq_len, 256:768])

    pw = pw_ref[...].astype(BF16)
    bdw = bdw_ref[l:l + 1, :]
    lng = lng_ref[l:l + 1, :]
    lnb = lnb_ref[l:l + 1, :]

    def body(c, carry):
        q = c // chunks_per_seq
        r = c % chunks_per_seq
        base = pl.multiple_of(q * stride + r * rows, 8)
        row0 = pl.multiple_of(c * rows, 8)
        win = pad_ref[pl.ds(base, rows + 2 * CONV_HALO), :]
        win_s = win[:, 0:256]
        win_u = win[:, 256:512]

        s_acc = None
        for k in range(SCONV_K):
            off = CONV_HALO + k - SCONV_K // 2
            term = wsc_ref[pl.ds(k, 1), :] * win_s[off:off + rows]
            s_acc = term if s_acc is None else s_acc + term
        a_out = cin_ref[pl.ds(row0, rows), 0:256] * s_acc

        acc = None
        for res in range(8):
            part = None
            for a in range(-2, 2):
                k = 8 * a + res + CCM_K // 2
                if 0 <= k < CCM_K:
                    lo = CONV_HALO + 8 * a
                    term = wdw_ref[pl.ds(k, 1), :] * win_u[lo:lo + rows + 8]
                    part = term if part is None else part + term
            part = part[res:res + rows]
            acc = part if acc is None else acc + part
        u = acc + bdw
        mu = jnp.mean(u, axis=-1, keepdims=True)
        var = jnp.mean(jnp.square(u - mu), axis=-1, keepdims=True)
        y = (u - mu) * lax.rsqrt(var + EPS) * lng + lnb
        y = y * jax.nn.sigmoid(y)
        c_out = _dot(y.astype(BF16), pw)
        o_ref[pl.ds(row0, rows), 0:256] = a_out.astype(BF16)
        o_ref[pl.ds(row0, rows), 256:512] = c_out.astype(BF16)
        return carry

    lax.fori_loop(0, t_tile // rows, body, 0, unroll=4)


def _conv(cin, prev, seq_len, tile0, n_tiles, cparams, l):
    t = T_CONV
    n_seq = t // seq_len
    vec = _const_spec((DEPTH, 256))
    in_specs = [pl.BlockSpec((t, 768), lambda i: (tile0 + i, 0)),
                _layer_spec((SCONV_K, 256), l), _layer_spec((CCM_K, 256), l), vec, vec, vec,
                _layer_spec((256, 256), l)]
    args = [cin, *cparams]
    aliases = {}
    if prev is not None:
        in_specs.append(pl.BlockSpec(memory_space=pl.ANY))
        args.append(prev)
        aliases = {len(args) - 1: 0}
    return pl.pallas_call(
        functools.partial(_conv_kernel, l, seq_len),
        grid=(n_tiles,),
        in_specs=in_specs,
        out_specs=pl.BlockSpec((t, 512), lambda i: (tile0 + i, 0)),
        out_shape=jax.ShapeDtypeStruct((N_TOK, 512), BF16),
        scratch_shapes=[pltpu.VMEM((n_seq * (seq_len + CONV_HALO) + CONV_HALO, 512), F32)],
        input_output_aliases=aliases,
        compiler_params=pltpu.CompilerParams(
            dimension_semantics=("arbitrary",), vmem_limit_bytes=VMEM_LIMIT),
        name=f"conv{seq_len}",
    )(*args)


def _cache_planes(cache_diff_k, cache_diff_v, cache_gqa_k, cache_gqa_v, l):
    pad128 = lambda a: jnp.pad(a.reshape(DEC_BATCH, PAST_LEN, 128), ((0, 0), (0, 0), (0, 128)))
    keys = jnp.stack([cache_diff_k[:, l].reshape(DEC_BATCH, PAST_LEN, 256),
                      pad128(cache_gqa_k[:, l])], axis=1).astype(BF16)
    vals = jnp.stack([cache_diff_v[:, l].reshape(DEC_BATCH, PAST_LEN, 256),
                      pad128(cache_gqa_v[:, l])], axis=1).astype(BF16)
    return keys, jnp.swapaxes(vals, 2, 3)


def kernel(x_prompt, x_sample, cache_diff_k, cache_diff_v, cache_gqa_k, cache_gqa_v, c, c_ctx, w_ada, b_ada, norm_pre, norm_post, ffn1_gate, ffn1_up, ffn1_down, ffn2_gate, ffn2_up, ffn2_down, w_mix_in, w_mix_out, sconv_w, diff_lq1, diff_lk1, diff_lq2, diff_lk2, diff_subln, ccm_dw_w, ccm_dw_b, ccm_ln_g, ccm_ln_b, ccm_pw, gqa_qnorm, gqa_knorm):
    conds = jnp.concatenate(
        [c_ctx[None, :], c, jnp.zeros((N_COND - 1 - DEC_BATCH, D_MODEL), F32)], axis=0)
    mod = _ada(conds, w_ada, b_ada)
    tables = jnp.asarray(_rope_tables())
    qg = jnp.tile(gqa_qnorm, (1, 4))
    kg = jnp.tile(gqa_knorm, (1, 2))
    aparams = (diff_lq1, diff_lk1, diff_lq2, diff_lk2, diff_subln.T)
    cparams = (sconv_w, ccm_dw_w, ccm_dw_b, ccm_ln_g, ccm_ln_b, ccm_pw)

    xs = (x_prompt.reshape(N_CTX, D_MODEL), x_sample.reshape(N_LAT, D_MODEL))
    own = None
    for l in range(DEPTH):
        (x,) = _ffn(xs, mod, norm_pre, norm_post, ffn1_gate, ffn1_up, ffn1_down, l, 0)

        cin, qt, k_all, vt_all, *own = _mixproj(
            x, mod, norm_pre, w_mix_in, qg, kg, tables, own, l)

        cache_k, cache_vt = _cache_planes(cache_diff_k, cache_diff_v, cache_gqa_k, cache_gqa_v, l)
        att = _attn_ctx(qt, k_all, vt_all, aparams, l)
        att = _attn_lat(qt, k_all, vt_all, cache_k, cache_vt, att, aparams, l)

        ac = _conv(cin, None, SEQ, 0, N_CTX // T_CONV, cparams, l)
        ac = _conv(cin, ac, DEC_SEQ, N_CTX // T_CONV, N_LAT // T_CONV, cparams, l)

        xs = _ffn((x,), mod, norm_pre, norm_post, ffn2_gate, ffn2_up, ffn2_down, l, 2,
                  split_out=(l == DEPTH - 1), mix=(ac, att, w_mix_out))

    y_prompt, y_sample = xs
    odk, odv, ogk, ogv = own
    return (y_prompt.reshape(BATCH, SEQ, D_MODEL),
            y_sample.reshape(DEC_BATCH, DEC_SEQ, D_MODEL),
            odk.reshape(BATCH, DEPTH, SEQ, 4, 2, DK_DIFF),
            odv.reshape(BATCH, DEPTH, SEQ, 4, DV_DIFF),
            ogk.reshape(BATCH, DEPTH, SEQ, 2, HD_GQA),
            ogv.reshape(BATCH, DEPTH, SEQ, 2, HD_GQA))
```

```python
import functools
import math

import numpy as np

import jax
import jax.numpy as jnp
from jax import lax
from jax.experimental import pallas as pl
from jax.experimental.pallas import tpu as pltpu

F32 = jnp.float32
BF16 = jnp.bfloat16

D_MODEL = 1024
BATCH = 16
SEQ = 256
DEPTH = 2
DEC_BATCH = 2
DEC_SEQ = 2048
PAST_LEN = 256
GRID_W = 64
GRID_H = DEC_SEQ // GRID_W
ROPE_THETA = 10000.0
EPS = 1e-6
N_MOD = 9
D_FF = 2816
DK_DIFF = 32
DV_DIFF = 64
HD_GQA = 64
CCM_K = 31
SCONV_K = 3
MIX_IN = 2560

N_CTX = BATCH * SEQ
N_LAT = DEC_BATCH * DEC_SEQ
N_TOK = N_CTX + N_LAT
N_COND = 8
S_LAT = DEC_SEQ + PAST_LEN

LOG2E = math.log2(math.e)

VMEM_LIMIT = 56 * 1024 * 1024

TM_FFN = 512
FFN_SUB = 256
FFN_CHUNK = 256
FFN_STAGE_SLOTS = 3
TM_MIX = 1024
MIX_SUB = 256
ADA_VECS = 3
TQ_LAT = 256
CTX_SEQS_PER_STEP = 4
T_CONV = 2048
R_CONV = 64
CONV_HALO = 16


def _cond_of_tile(i, tm):
    n_ctx_tiles = N_CTX // tm
    per_b = DEC_SEQ // tm
    return jnp.where(i < n_ctx_tiles, 0, 1 + (i - n_ctx_tiles) // per_b)


def _mod_row(mod_ref, k, cond):
    return mod_ref[k, pl.ds(cond, 1), :]


def _dot(a, b):
    return jnp.dot(a, b, preferred_element_type=F32)


def _dot_nt(a, b):
    return lax.dot_general(a, b, (((1,), (1,)), ((), ())), preferred_element_type=F32)


def _rms(x, g):
    ms = jnp.mean(x * x, axis=-1, keepdims=True)
    return x * lax.rsqrt(ms + EPS) * g


def _group_mean(xx, width, group):
    r = lax.broadcasted_iota(jnp.int32, (width, width), 0) // group
    c = lax.broadcasted_iota(jnp.int32, (width, width), 1) // group
    gmat = jnp.where(r == c, 1.0 / group, 0.0).astype(BF16)
    hi = xx.astype(BF16)
    lo = (xx - hi.astype(F32)).astype(BF16)
    return _dot(hi, gmat) + _dot(lo, gmat)


def _const_spec(shape):
    return pl.BlockSpec(shape, lambda *_: (0,) * len(shape))


def _layer_spec(shape, l, **kw):
    return pl.BlockSpec((None,) + shape, lambda *_: (l,) + (0,) * len(shape), **kw)


_MOD_SHAPE = (N_MOD, N_COND, D_MODEL)


def _ada_kernel(c_ref, w_ref, b_ref, o_ref):
    cnd = c_ref[...]
    s = (cnd * jax.nn.sigmoid(cnd)).astype(BF16)
    for k in range(ADA_VECS):
        cols = slice(k * D_MODEL, (k + 1) * D_MODEL)
        o_ref[k] = _dot(s, w_ref[:, cols].astype(BF16)) + b_ref[k]


def _ada(conds, w_ada, b_ada):
    return pl.pallas_call(
        _ada_kernel,
        grid=(DEPTH, N_MOD // ADA_VECS),
        in_specs=[
            pl.BlockSpec((N_COND, D_MODEL), lambda l, n: (0, 0)),
            pl.BlockSpec((None, D_MODEL, ADA_VECS * D_MODEL), lambda l, n: (l, 0, n)),
            pl.BlockSpec((None, ADA_VECS, 1, D_MODEL), lambda l, n: (l, n, 0, 0)),
        ],
        out_specs=pl.BlockSpec((None, ADA_VECS, N_COND, D_MODEL), lambda l, n: (l, n, 0, 0)),
        out_shape=jax.ShapeDtypeStruct((DEPTH,) + _MOD_SHAPE, F32),
        compiler_params=pltpu.CompilerParams(
            dimension_semantics=("arbitrary", "arbitrary"), vmem_limit_bytes=VMEM_LIMIT),
        name="ada",
    )(conds, w_ada, b_ada.reshape(DEPTH, N_MOD, 1, D_MODEL))


class _WeightStager:
    def __init__(self, jobs, stage_ref, sem_ref):
        self.jobs, self.stage_ref, self.depth = jobs, stage_ref, stage_ref.shape[0]
        self.copies = [
            pltpu.make_async_copy(src, stage_ref.at[k % self.depth], sem_ref.at[k % self.depth])
            for k, (src, _) in enumerate(jobs)]

    def prime(self):
        for k in range(min(self.depth - 1, len(self.jobs))):
            self.copies[k].start()

    def finish(self, k):
        if k + self.depth - 1 < len(self.jobs):
            self.copies[k + self.depth - 1].start()
        self.copies[k].wait()
        self.jobs[k][1](self.stage_ref[k % self.depth].astype(BF16))


def _ffn_kernel(l, sub, tm, n_x, n_o, mix, *refs):
    refs = list(refs)
    x_refs = [refs.pop(0) for _ in range(n_x)]
    ac_ref, att_ref = (refs.pop(0), refs.pop(0)) if mix else (None, None)
    mod_ref, gpre_ref, gpost_ref, wg_hbm, wu_hbm, wd_hbm = [refs.pop(0) for _ in range(6)]
    wo_hbm = refs.pop(0) if mix else None
    o_refs = [refs.pop(0) for _ in range(n_o)]
    wg_ref, wu_ref, wd_ref = [refs.pop(0) for _ in range(3)]
    wo_ref = refs.pop(0) if mix else None
    i = pl.program_id(0)
    is_ctx = i < N_CTX // tm
    cond = _cond_of_tile(i, tm)
    c = FFN_CHUNK
    n_chunks = D_FF // c
    gate = _mod_row(mod_ref, 3 * sub + 2, cond)
    gpost = gpost_ref[pl.ds(sub, 1), :]

    def prep():
        if n_x == 1:
            x = x_refs[0][...]
        else:
            x = jnp.where(is_ctx, x_refs[0][...], x_refs[1][...])
        if mix:
            cat = jnp.concatenate([ac_ref[:, 0:256], att_ref[:, 0:256],
                                   ac_ref[:, 256:512], att_ref[:, 256:512]], axis=1)
            ym = _dot(cat, wo_ref[...])
            x = x + _mod_row(mod_ref, 5, cond) * _rms(ym, gpost_ref[pl.ds(1, 1), :])
        shift = _mod_row(mod_ref, 3 * sub, cond)
        scale = _mod_row(mod_ref, 3 * sub + 1, cond)
        h = (_rms(x, gpre_ref[pl.ds(sub, 1), :]) * (1.0 + scale) + shift).astype(BF16)
        return x, h

    def swiglu(hh, cols):
        g = _dot(hh, wg_ref[:, cols])
        u = _dot(hh, wu_ref[:, cols])
        a = ((g * jax.nn.sigmoid(g)) * u).astype(BF16)
        return _dot(a, wd_ref[cols, :])

    def emit(y):
        if n_o == 1:
            o_refs[0][...] = y
        else:
            @pl.when(is_ctx)
            def _():
                o_refs[0][...] = y

            @pl.when(jnp.logical_not(is_ctx))
            def _():
                o_refs[1][...] = y

    @pl.when(i == 0)
    def _():
        def col_store(dst, j):
            def store(v):
                dst[:, j * c:(j + 1) * c] = v
            return store

        def row_store(j):
            def store(v):
                wd_ref[j * c:(j + 1) * c, :] = v
            return store

        up_jobs = [(src.at[l, :, pl.ds(j * c, c)], col_store(dst, j))
                   for j in range(n_chunks) for src, dst in ((wg_hbm, wg_ref), (wu_hbm, wu_ref))]
        down_jobs = [(wd_hbm.at[l, pl.ds(j * c, c), :], row_store(j)) for j in range(n_chunks)]

        def run(stage_up, stage_down, sem_up, sem_down):
            up = _WeightStager(up_jobs, stage_up, sem_up)
            up.prime()
            if mix:
                def wo_store(j):
                    def store(v):
                        wo_ref[j * c:(j + 1) * c, :] = v
                    return store

                wo = _WeightStager(
                    [(wo_hbm.at[l, pl.ds(j * c, c), :], wo_store(j)) for j in range(D_MODEL // c)],
                    stage_down, sem_down)
                wo.prime()
                for j in range(D_MODEL // c):
                    wo.finish(j)
            down = _WeightStager(down_jobs, stage_down, sem_down)
            down.prime()
            x, h = prep()
            y = None
            for j in range(n_chunks):
                up.finish(2 * j)
                up.finish(2 * j + 1)
                down.finish(j)
                part = swiglu(h, slice(j * c, (j + 1) * c))
                y = part if y is None else y + part
            emit(x + 0.5 * (gate * _rms(y, gpost)))

        pl.run_scoped(run,
                      pltpu.VMEM((2 * FFN_STAGE_SLOTS, D_MODEL, c), F32),
                      pltpu.VMEM((FFN_STAGE_SLOTS, c, D_MODEL), F32),
                      pltpu.SemaphoreType.DMA((2 * FFN_STAGE_SLOTS,)),
                      pltpu.SemaphoreType.DMA((FFN_STAGE_SLOTS,)))

    @pl.when(i != 0)
    def _():
        x, h = prep()
        outs = []
        for s in range(tm // FFN_SUB):
            rows = slice(s * FFN_SUB, (s + 1) * FFN_SUB)
            y = swiglu(h[rows], slice(None))
            outs.append(x[rows] + 0.5 * (gate * _rms(y, gpost)))
        emit(jnp.concatenate(outs, axis=0))


def _ffn(xs, mod, norm_pre, norm_post, wg, wu, wd, l, sub, split_out=False, mix=None):
    tm = TM_FFN
    n_ctx_tiles = N_CTX // tm
    ctx_spec = pl.BlockSpec((tm, D_MODEL), lambda i: (jnp.minimum(i, n_ctx_tiles - 1), 0))
    lat_spec = pl.BlockSpec((tm, D_MODEL), lambda i: (jnp.maximum(i - n_ctx_tiles, 0), 0))
    full_spec = pl.BlockSpec((tm, D_MODEL), lambda i: (i, 0))
    x_specs = [full_spec] if len(xs) == 1 else [ctx_spec, lat_spec]
    if split_out:
        out_specs = [ctx_spec, lat_spec]
        out_shape = [jax.ShapeDtypeStruct((N_CTX, D_MODEL), F32),
                     jax.ShapeDtypeStruct((N_LAT, D_MODEL), F32)]
    else:
        out_specs = [full_spec]
        out_shape = [jax.ShapeDtypeStruct((N_TOK, D_MODEL), F32)]
    hbm = pl.BlockSpec(memory_space=pl.ANY)
    args = list(xs)
    in_specs = list(x_specs)
    scratch = [pltpu.VMEM((D_MODEL, D_FF), BF16), pltpu.VMEM((D_MODEL, D_FF), BF16),
               pltpu.VMEM((D_FF, D_MODEL), BF16)]
    if mix is not None:
        args += [mix[0], mix[1]]
        in_specs += [pl.BlockSpec((tm, 512), lambda i: (i, 0))] * 2
    args += [mod, norm_pre, norm_post, wg, wu, wd]
    in_specs += [_layer_spec(_MOD_SHAPE, l), _layer_spec((3, D_MODEL), l),
                 _layer_spec((3, D_MODEL), l), hbm, hbm, hbm]
    if mix is not None:
        args.append(mix[2])
        in_specs.append(hbm)
        scratch.append(pltpu.VMEM((D_MODEL, D_MODEL), BF16))
    return pl.pallas_call(
        functools.partial(_ffn_kernel, l, sub, tm, len(xs), len(out_specs), mix is not None),
        grid=(N_TOK // tm,),
        in_specs=in_specs,
        out_specs=out_specs,
        out_shape=out_shape,
        scratch_shapes=scratch,
        compiler_params=pltpu.CompilerParams(
            dimension_semantics=("arbitrary",), vmem_limit_bytes=VMEM_LIMIT),
        name=f"ffn{sub}",
    )(*args)


def _swap_blocks(x, blk):
    w = x.shape[-1]
    lane = lax.broadcasted_iota(jnp.int32, x.shape, x.ndim - 1)
    first = (lane % (2 * blk)) < blk
    return jnp.where(first, pltpu.roll(x, w - blk, x.ndim - 1), pltpu.roll(x, blk, x.ndim - 1))


def _rope_tables():
    out = []
    for rot_dim in (DK_DIFF, HD_GQA):
        n_freq = rot_dim // 4
        inv = ROPE_THETA ** (-np.arange(n_freq, dtype=np.float64) / n_freq)
        lane = np.arange(256) % rot_dim
        pos = lane % (rot_dim // 2)
        freq = inv[pos % n_freq]
        sign = np.where(pos < n_freq, -1.0, 1.0)
        coord = np.concatenate([np.arange(GRID_H), np.arange(GRID_W)]).astype(np.float64)
        ang = coord[:, None] * freq[None, :]
        out.append(np.cos(ang))
        out.append(np.sin(ang) * sign[None, :])
    return np.stack(out).astype(np.float32)


def _tile_rope(tab_ref, which, rot_dim, first_grid_row, n_grid_rows, identity):
    col_part = tab_ref[which, GRID_H:GRID_H + GRID_W, :]
    lane = lax.broadcasted_iota(jnp.int32, (1, 256), 1)
    on_row_axis = (lane % rot_dim) < rot_dim // 2
    tab = jnp.concatenate(
        [jnp.where(on_row_axis, tab_ref[which, pl.ds(first_grid_row + r, 1), :], col_part)
         for r in range(n_grid_rows)], axis=0)
    return jnp.where(identity, 1.0 - (which % 2), tab)


def _mixproj_kernel(l, has_prev, x_ref, mod_ref, gpre_ref, w_ref, qg_ref, kg_ref, tab_ref, *rest):
    n_alias = 4 if has_prev else 0
    (cin_ref, qt_ref, k_ref, vt_ref, odk_ref, odv_ref, ogk_ref, ogv_ref, wbf_ref) = rest[n_alias:]
    tm = TM_MIX
    i = pl.program_id(0)
    n_ctx_tiles = N_CTX // tm
    is_ctx = i < n_ctx_tiles
    cond = _cond_of_tile(i, tm)

    @pl.when(i == 0)
    def _():
        wbf_ref[...] = w_ref[...].astype(BF16)

    shift = _mod_row(mod_ref, 3, cond)
    scale = _mod_row(mod_ref, 4, cond)
    gpre = gpre_ref[pl.ds(1, 1), :]
    sub = MIX_SUB
    grid_rows = sub // GRID_W
    seqs = sub // SEQ
    tile_row = (jnp.maximum(i - n_ctx_tiles, 0) % (DEC_SEQ // tm)) * (tm // GRID_W)
    own = []

    for s in range(tm // sub):
        rows = slice(s * sub, (s + 1) * sub)
        h = (_rms(x_ref[rows, :], gpre) * (1.0 + scale) + shift).astype(BF16)

        a = _dot(h, wbf_ref[:, 0:768])
        cin_ref[rows, 0:256] = a[:, 0:256]
        cin_ref[rows, 256:512] = a[:, 256:512] * a[:, 512:768]
        c = _dot(h, wbf_ref[:, 1536:2048])
        cin_ref[rows, 512:768] = c[:, 0:256] * jax.nn.sigmoid(c[:, 256:512])

        first_row = tile_row + s * grid_rows
        cd = _tile_rope(tab_ref, 0, DK_DIFF, first_row, grid_rows, is_ctx)
        sd = _tile_rope(tab_ref, 1, DK_DIFF, first_row, grid_rows, is_ctx)
        cg = _tile_rope(tab_ref, 2, HD_GQA, first_row, grid_rows, is_ctx)
        sg = _tile_rope(tab_ref, 3, HD_GQA, first_row, grid_rows, is_ctx)

        b = _dot(h, wbf_ref[:, 768:1536])
        bq, bk, bv = b[:, 0:256], b[:, 256:512], b[:, 512:768]
        bq_r = (bq * cd + _swap_blocks(bq, DK_DIFF // 4) * sd) * (DK_DIFF ** -0.5 * LOG2E)
        bk_r = bk * cd + _swap_blocks(bk, DK_DIFF // 4) * sd
        qt_ref[0:256, rows] = bq_r.T.astype(BF16)

        d = _dot(h, wbf_ref[:, 2048:2560])
        dq, dk, dv = d[:, 0:256], d[:, 256:384], d[:, 384:512]
        dq_n = dq * lax.rsqrt(_group_mean(dq * dq, 256, HD_GQA) + EPS) * qg_ref[l:l + 1, :]
        dk_n = dk * lax.rsqrt(_group_mean(dk * dk, 128, HD_GQA) + EPS) * kg_ref[l:l + 1, :]
        dq_r = (dq_n * cg + _swap_blocks(dq_n, HD_GQA // 4) * sg) * (HD_GQA ** -0.5 * LOG2E)
        dk_r = dk_n * cg[:, 0:128] + _swap_blocks(dk_n, HD_GQA // 4) * sg[:, 0:128]
        qt_ref[256:512, rows] = dq_r.T.astype(BF16)

        k_ref[0, rows, :] = bk_r.astype(BF16)
        k_ref[1, rows, :] = jnp.concatenate([dk_r, jnp.zeros_like(dk_r)], axis=1).astype(BF16)
        vt_ref[0, :, rows] = bv.T.astype(BF16)
        vt_ref[1, :, rows] = jnp.concatenate([dv, jnp.zeros_like(dv)], axis=1).T.astype(BF16)
        own.append((bk, bv, dk_n, dv))

    @pl.when(is_ctx)
    def _():
        for s, (bk, bv, dk_n, dv) in enumerate(own):
            seq = slice(s * seqs, (s + 1) * seqs)
            odk_ref[seq] = bk.reshape(seqs, SEQ, 256)
            odv_ref[seq] = bv.reshape(seqs, SEQ, 256)
            ogk_ref[seq] = dk_n.reshape(seqs, SEQ, 128)
            ogv_ref[seq] = dv.reshape(seqs, SEQ, 128)


def _mixproj(x, mod, norm_pre, w_mix_in, qg, kg, tables, prev_own, l):
    tm = TM_MIX
    n_ctx_tiles = N_CTX // tm
    seqs = tm // SEQ

    ctx_row = lambda i: jnp.minimum(i, n_ctx_tiles - 1)
    row = lambda w: pl.BlockSpec((tm, w), lambda i: (i, 0))
    own_spec = lambda w: pl.BlockSpec((seqs, None, SEQ, w), lambda i: (ctx_row(i), l, 0, 0))
    own_shape = lambda w: jax.ShapeDtypeStruct((BATCH, DEPTH, SEQ, w), F32)
    any_spec = pl.BlockSpec(memory_space=pl.ANY)
    aliased = list(prev_own) if prev_own is not None else []
    n_in = 7
    aliases = {n_in + k: 4 + k for k in range(len(aliased))}
    return pl.pallas_call(
        functools.partial(_mixproj_kernel, l, prev_own is not None),
        grid=(N_TOK // tm,),
        in_specs=[
            row(D_MODEL),
            _layer_spec(_MOD_SHAPE, l),
            _layer_spec((3, D_MODEL), l),
            _layer_spec((D_MODEL, MIX_IN), l, pipeline_mode=pl.Buffered(1)),
            _const_spec((DEPTH, 256)),
            _const_spec((DEPTH, 128)),
            _const_spec(tables.shape),
        ] + [any_spec] * len(aliased),
        out_specs=[row(768),
                   pl.BlockSpec((512, tm), lambda i: (0, i)),
                   pl.BlockSpec((2, tm, 256), lambda i: (0, i, 0)),
                   pl.BlockSpec((2, 256, tm), lambda i: (0, 0, i)),
                   own_spec(256), own_spec(256), own_spec(128), own_spec(128)],
        out_shape=[
            jax.ShapeDtypeStruct((N_TOK, 768), F32),
            jax.ShapeDtypeStruct((512, N_TOK), BF16),
            jax.ShapeDtypeStruct((2, N_TOK, 256), BF16),
            jax.ShapeDtypeStruct((2, 256, N_TOK), BF16),
            own_shape(256), own_shape(256), own_shape(128), own_shape(128),
        ],
        scratch_shapes=[pltpu.VMEM((D_MODEL, MIX_IN), BF16)],
        input_output_aliases=aliases,
        compiler_params=pltpu.CompilerParams(
            dimension_semantics=("arbitrary",), vmem_limit_bytes=VMEM_LIMIT),
        name="mixproj",
    )(x, mod, norm_pre, w_mix_in, qg, kg, tables, *aliased)


N_UNITS = 12
ATTN_SLOTS = 4


def _attn_kernel(l, qt_ref, k_ref, vt_ref, kc_ref, vtc_ref, lq1_ref, lk1_ref, lq2_ref, lk2_ref,
                 subln_ref, *rest):
    n = ATTN_SLOTS
    o_ref, qs_ref = rest[-3 - 3 * n], rest[-2 - 3 * n]
    s_refs = rest[-1 - 3 * n:-1 - 2 * n]
    e_refs = rest[-1 - 2 * n:-1 - n]
    il_refs = rest[-1 - n:-1]
    oacc_ref = rest[-1]
    own = k_ref.shape[1]
    half = own // 2

    for u in range(N_UNITS):
        qs_ref[u] = _unit_queries(qt_ref, u)

    def scores(u):
        plane = 0 if u < 8 else 1
        s_ref = s_refs[u % n]
        s_ref[0:half, :] = _dot(k_ref[plane, 0:half, :], qs_ref[u])
        s_ref[half:own, :] = _dot(k_ref[plane, half:, :], qs_ref[u])
        s_ref[own:, :] = _dot(kc_ref[plane], qs_ref[u])

    def softmax(u):
        s = s_refs[u % n][...]
        e = jnp.exp2(s - jnp.max(s, axis=0, keepdims=True))
        il_refs[u % n][...] = 1.0 / jnp.sum(e, axis=0, keepdims=True)
        e_refs[u % n][...] = e.astype(BF16)

    def values(u):
        plane = 0 if u < 8 else 1
        vrow = DV_DIFF * (u // 2) if u < 8 else HD_GQA * ((u - 8) // 2)
        e_ref = e_refs[u % n]
        o = (_dot(vt_ref[plane, vrow:vrow + DV_DIFF, 0:half], e_ref[0:half, :])
             + _dot(vt_ref[plane, vrow:vrow + DV_DIFF, half:], e_ref[half:own, :])
             + _dot(vtc_ref[plane, vrow:vrow + DV_DIFF, :], e_ref[own:, :]))
        oacc_ref[u] = o * il_refs[u % n][...]

    lag = ATTN_SLOTS // 2
    for t in range(N_UNITS + 2 * lag):
        if t < N_UNITS:
            scores(t)
        if lag <= t < N_UNITS + lag:
            softmax(t - lag)
        if t >= 2 * lag:
            values(t - 2 * lag)

    _attn_finish(l, [oacc_ref[u] for u in range(N_UNITS)],
                 lq1_ref, lk1_ref, lq2_ref, lk2_ref, subln_ref, o_ref)


def _unit_queries(qt_ref, u, cols=slice(None)):
    if u < 8:
        feat = lax.broadcasted_iota(jnp.int32, (256, 1), 0)
        return jnp.where(feat // DK_DIFF == u, qt_ref[0:256, cols], jnp.zeros((), BF16))
    hd = u - 8
    g = hd // 2
    rows = qt_ref[256 + HD_GQA * hd:256 + HD_GQA * (hd + 1), cols]
    tq = rows.shape[1]
    pieces = ([jnp.zeros((HD_GQA * g, tq), BF16)] if g else []) + [rows]
    pieces.append(jnp.zeros((256 - HD_GQA * (g + 1), tq), BF16))
    return jnp.concatenate(pieces, axis=0)


def _attn_finish(l, outs, lq1_ref, lk1_ref, lq2_ref, lk2_ref, subln_ref, o_ref, rows=slice(None)):
    lam_init = 0.8 - 0.6 * math.exp(-0.3 * l)
    row = slice(l, l + 1)
    lam = (jnp.exp(jnp.sum(lq1_ref[row, :] * lk1_ref[row, :], axis=-1, keepdims=True))
           - jnp.exp(jnp.sum(lq2_ref[row, :] * lk2_ref[row, :], axis=-1, keepdims=True))
           + lam_init)
    heads = []
    for hd in range(4):
        o = outs[2 * hd] - lam * outs[2 * hd + 1]
        nrm = lax.rsqrt(jnp.mean(o * o, axis=0, keepdims=True) + EPS)
        heads.append(o * nrm * (subln_ref[:, l:l + 1] * (1.0 - lam_init)))
    heads += outs[8:12]
    o_ref[rows, :] = jnp.concatenate(heads, axis=0).T.astype(BF16)


def _attn_ctx_kernel(l, qt_ref, k_ref, vt_ref, lq1_ref, lk1_ref, lq2_ref, lk2_ref, subln_ref,
                     o_ref):
    for q in range(qt_ref.shape[1] // SEQ):
        tok = slice(q * SEQ, (q + 1) * SEQ)
        w_diff = jnp.concatenate([_unit_queries(qt_ref, u, tok) for u in range(8)], axis=1)
        w_gqa = jnp.concatenate([_unit_queries(qt_ref, u, tok) for u in range(8, N_UNITS)], axis=1)
        s = jnp.concatenate([_dot(k_ref[0, tok, :], w_diff), _dot(k_ref[1, tok, :], w_gqa)],
                            axis=1)
        e = jnp.exp2(s - jnp.max(s, axis=0, keepdims=True))
        il = 1.0 / jnp.sum(e, axis=0, keepdims=True)
        e = e.astype(BF16)
        outs = []
        for u in range(N_UNITS):
            plane = 0 if u < 8 else 1
            vrow = DV_DIFF * (u // 2) if u < 8 else HD_GQA * ((u - 8) // 2)
            cols = slice(u * SEQ, (u + 1) * SEQ)
            outs.append(_dot(vt_ref[plane, vrow:vrow + DV_DIFF, tok], e[:, cols]) * il[:, cols])
        _attn_finish(l, outs, lq1_ref, lk1_ref, lq2_ref, lk2_ref, subln_ref, o_ref, tok)


def _attn_param_specs():
    vec = _const_spec((DEPTH, DK_DIFF))
    return [vec, vec, vec, vec, _const_spec((DV_DIFF, DEPTH))]


def _attn_scratch(tq, s_len):
    n = ATTN_SLOTS
    return ([pltpu.VMEM((N_UNITS, 256, tq), BF16)]
            + [pltpu.VMEM((s_len, tq), F32)] * n
            + [pltpu.VMEM((s_len, tq), BF16)] * n
            + [pltpu.VMEM((1, tq), F32)] * n
            + [pltpu.VMEM((N_UNITS, DV_DIFF, tq), F32)])


def _attn_ctx(qt, k_ctx, vt_ctx, aparams, l):
    t = CTX_SEQS_PER_STEP * SEQ
    return pl.pallas_call(
        functools.partial(_attn_ctx_kernel, l),
        grid=(N_CTX // t,),
        in_specs=[pl.BlockSpec((512, t), lambda b: (0, b)),
                  pl.BlockSpec((2, t, 256), lambda b: (0, b, 0)),
                  pl.BlockSpec((2, 256, t), lambda b: (0, 0, b))] + _attn_param_specs(),
        out_specs=pl.BlockSpec((t, 512), lambda b: (b, 0)),
        out_shape=jax.ShapeDtypeStruct((N_TOK, 512), BF16),
        compiler_params=pltpu.CompilerParams(
            dimension_semantics=("arbitrary",), vmem_limit_bytes=VMEM_LIMIT),
        name="attn_ctx",
    )(qt, k_ctx, vt_ctx, *aparams)


def _attn_lat(qt, k_all, vt_all, cache_k, cache_vt, att, aparams, l):
    tq = TQ_LAT
    nq = DEC_SEQ // tq
    tile = lambda b, i: N_CTX // tq + b * nq + i
    own = lambda b: N_CTX // DEC_SEQ + b
    return pl.pallas_call(
        functools.partial(_attn_kernel, l),
        grid=(DEC_BATCH, nq),
        in_specs=[pl.BlockSpec((512, tq), lambda b, i: (0, tile(b, i))),
                  pl.BlockSpec((2, DEC_SEQ, 256), lambda b, i: (0, own(b), 0)),
                  pl.BlockSpec((2, 256, DEC_SEQ), lambda b, i: (0, 0, own(b))),
                  pl.BlockSpec((None, None, 2, PAST_LEN, 256), lambda b, i: (b, l, 0, 0, 0)),
                  pl.BlockSpec((None, None, 2, 256, PAST_LEN), lambda b, i: (b, l, 0, 0, 0))]
        + _attn_param_specs() + [pl.BlockSpec(memory_space=pl.ANY)],
        out_specs=pl.BlockSpec((tq, 512), lambda b, i: (tile(b, i), 0)),
        out_shape=jax.ShapeDtypeStruct((N_TOK, 512), BF16),
        scratch_shapes=_attn_scratch(tq, S_LAT),
        input_output_aliases={10: 0},
        compiler_params=pltpu.CompilerParams(
            dimension_semantics=("arbitrary", "arbitrary"), vmem_limit_bytes=VMEM_LIMIT),
        name="attn_lat",
    )(qt, k_all, vt_all, cache_k, cache_vt, *aparams, att)


def _conv_kernel(l, seq_len, cin_ref, wsc_ref, wdw_ref, bdw_ref, lng_ref, lnb_ref, pw_ref, *rest):
    o_ref, pad_ref = rest[-2], rest[-1]
    t_tile = cin_ref.shape[0]
    n_seq = t_tile // seq_len
    stride = seq_len + CONV_HALO
    chunks_per_seq = seq_len // R_CONV
    rows = R_CONV

    for q in range(n_seq + 1):
        pad_ref[q * stride:q * stride + CONV_HALO, :] = jnp.zeros((CONV_HALO, 512), F32)
    for q in range(n_seq):
        pad_ref[CONV_HALO + q * stride:CONV_HALO + q * stride + seq_len, :] = (
            cin_ref[q * seq_len:(q + 1) * seq_len, 256:768])

    pw = pw_ref[...].astype(BF16)
    bdw = bdw_ref[l:l + 1, :]
    lng = lng_ref[l:l + 1, :]
    lnb = lnb_ref[l:l + 1, :]

    def body(c, carry):
        q = c // chunks_per_seq
        r = c % chunks_per_seq
        base = pl.multiple_of(q * stride + r * rows, 8)
        row0 = pl.multiple_of(c * rows, 8)
        win = pad_ref[pl.ds(base, rows + 2 * CONV_HALO), :]
        win_s = win[:, 0:256]
        win_u = win[:, 256:512]

        s_acc = None
        for k in range(SCONV_K):
            off = CONV_HALO + k - SCONV_K // 2
            term = wsc_ref[pl.ds(k, 1), :] * win_s[off:off + rows]
            s_acc = term if s_acc is None else s_acc + term
        a_out = cin_ref[pl.ds(row0, rows), 0:256] * s_acc

        acc = None
        for res in range(8):
            part = None
            for a in range(-2, 2):
                k = 8 * a + res + CCM_K // 2
                if 0 <= k < CCM_K:
                    lo = CONV_HALO + 8 * a
                    term = wdw_ref[pl.ds(k, 1), :] * win_u[lo:lo + rows + 8]
                    part = term if part is None else part + term
            part = part[res:res + rows]
            acc = part if acc is None else acc + part
        u = acc + bdw
        mu = jnp.mean(u, axis=-1, keepdims=True)
        var = jnp.mean(jnp.square(u - mu), axis=-1, keepdims=True)
        y = (u - mu) * lax.rsqrt(var + EPS) * lng + lnb
        y = y * jax.nn.sigmoid(y)
        c_out = _dot(y.astype(BF16), pw)
        o_ref[pl.ds(row0, rows), 0:256] = a_out.astype(BF16)
        o_ref[pl.ds(row0, rows), 256:512] = c_out.astype(BF16)
        return carry

    lax.fori_loop(0, t_tile // rows, body, 0, unroll=4)


def _conv(cin, prev, seq_len, tile0, n_tiles, cparams, l):
    t = T_CONV
    n_seq = t // seq_len
    vec = _const_spec((DEPTH, 256))
    in_specs = [pl.BlockSpec((t, 768), lambda i: (tile0 + i, 0)),
                _layer_spec((SCONV_K, 256), l), _layer_spec((CCM_K, 256), l), vec, vec, vec,
                _layer_spec((256, 256), l)]
    args = [cin, *cparams]
    aliases = {}
    if prev is not None:
        in_specs.append(pl.BlockSpec(memory_space=pl.ANY))
        args.append(prev)
        aliases = {len(args) - 1: 0}
    return pl.pallas_call(
        functools.partial(_conv_kernel, l, seq_len),
        grid=(n_tiles,),
        in_specs=in_specs,
        out_specs=pl.BlockSpec((t, 512), lambda i: (tile0 + i, 0)),
        out_shape=jax.ShapeDtypeStruct((N_TOK, 512), BF16),
        scratch_shapes=[pltpu.VMEM((n_seq * (seq_len + CONV_HALO) + CONV_HALO, 512), F32)],
        input_output_aliases=aliases,
        compiler_params=pltpu.CompilerParams(
            dimension_semantics=("arbitrary",), vmem_limit_bytes=VMEM_LIMIT),
        name=f"conv{seq_len}",
    )(*args)


def _cache_planes(cache_diff_k, cache_diff_v, cache_gqa_k, cache_gqa_v):
    lead = (DEC_BATCH, DEPTH, PAST_LEN)
    pad128 = lambda a: jnp.pad(a.reshape(lead + (128,)), ((0, 0), (0, 0), (0, 0), (0, 128)))
    keys = jnp.stack([cache_diff_k.reshape(lead + (256,)), pad128(cache_gqa_k)], axis=2)
    vals = jnp.stack([cache_diff_v.reshape(lead + (256,)), pad128(cache_gqa_v)], axis=2)
    return keys.astype(BF16), jnp.swapaxes(vals, 3, 4).astype(BF16)


def kernel(x_prompt, x_sample, cache_diff_k, cache_diff_v, cache_gqa_k, cache_gqa_v, c, c_ctx, w_ada, b_ada, norm_pre, norm_post, ffn1_gate, ffn1_up, ffn1_down, ffn2_gate, ffn2_up, ffn2_down, w_mix_in, w_mix_out, sconv_w, diff_lq1, diff_lk1, diff_lq2, diff_lk2, diff_subln, ccm_dw_w, ccm_dw_b, ccm_ln_g, ccm_ln_b, ccm_pw, gqa_qnorm, gqa_knorm):
    conds = jnp.concatenate(
        [c_ctx[None, :], c, jnp.zeros((N_COND - 1 - DEC_BATCH, D_MODEL), F32)], axis=0)
    mod = _ada(conds, w_ada, b_ada)
    tables = jnp.asarray(_rope_tables())
    qg = jnp.tile(gqa_qnorm, (1, 4))
    kg = jnp.tile(gqa_knorm, (1, 2))
    aparams = (diff_lq1, diff_lk1, diff_lq2, diff_lk2, diff_subln.T)
    cparams = (sconv_w, ccm_dw_w, ccm_dw_b, ccm_ln_g, ccm_ln_b, ccm_pw)
    cache_k, cache_vt = _cache_planes(cache_diff_k, cache_diff_v, cache_gqa_k, cache_gqa_v)

    xs = (x_prompt.reshape(N_CTX, D_MODEL), x_sample.reshape(N_LAT, D_MODEL))
    own = None
    for l in range(DEPTH):
        (x,) = _ffn(xs, mod, norm_pre, norm_post, ffn1_gate, ffn1_up, ffn1_down, l, 0)

        cin, qt, k_all, vt_all, *own = _mixproj(
            x, mod, norm_pre, w_mix_in, qg, kg, tables, own, l)

        att = _attn_ctx(qt, k_all, vt_all, aparams, l)
        att = _attn_lat(qt, k_all, vt_all, cache_k, cache_vt, att, aparams, l)

        ac = _conv(cin, None, SEQ, 0, N_CTX // T_CONV, cparams, l)
        ac = _conv(cin, ac, DEC_SEQ, N_CTX // T_CONV, N_LAT // T_CONV, cparams, l)

        xs = _ffn((x,), mod, norm_pre, norm_post, ffn2_gate, ffn2_up, ffn2_down, l, 2,
                  split_out=(l == DEPTH - 1), mix=(ac, att, w_mix_out))

    y_prompt, y_sample = xs
    odk, odv, ogk, ogv = own
    return (y_prompt.reshape(BATCH, SEQ, D_MODEL),
            y_sample.reshape(DEC_BATCH, DEC_SEQ, D_MODEL),
            odk.reshape(BATCH, DEPTH, SEQ, 4, 2, DK_DIFF),
            odv.reshape(BATCH, DEPTH, SEQ, 4, DV_DIFF),
            ogk.reshape(BATCH, DEPTH, SEQ, 2, HD_GQA),
            ogv.reshape(BATCH, DEPTH, SEQ, 2, HD_GQA))
```

```python
import functools
import math

import numpy as np

import jax
import jax.numpy as jnp
from jax import lax
from jax.experimental import pallas as pl
from jax.experimental.pallas import tpu as pltpu

F32 = jnp.float32
BF16 = jnp.bfloat16

D_MODEL = 1024
BATCH = 16
SEQ = 256
DEPTH = 2
DEC_BATCH = 2
DEC_SEQ = 2048
PAST_LEN = 256
GRID_W = 64
GRID_H = DEC_SEQ // GRID_W
ROPE_THETA = 10000.0
EPS = 1e-6
N_MOD = 9
D_FF = 2816
DK_DIFF = 32
DV_DIFF = 64
HD_GQA = 64
CCM_K = 31
SCONV_K = 3
MIX_IN = 2560

N_CTX = BATCH * SEQ
N_LAT = DEC_BATCH * DEC_SEQ
N_TOK = N_CTX + N_LAT
N_COND = 8
S_LAT = DEC_SEQ + PAST_LEN

LOG2E = math.log2(math.e)

VMEM_LIMIT = 56 * 1024 * 1024

TM_FFN = 512
FFN_SUB = 256
FFN_CHUNK = 256
FFN_STAGE_SLOTS = 3
TM_MIX = 1024
MIX_SUB = 256
ADA_VECS = 3
TQ_LAT = 256
CTX_SEQS_PER_STEP = 4
T_CONV = 2048
R_CONV = 128
CONV_HALO = 16


def _cond_of_tile(i, tm):
    n_ctx_tiles = N_CTX // tm
    per_b = DEC_SEQ // tm
    return jnp.where(i < n_ctx_tiles, 0, 1 + (i - n_ctx_tiles) // per_b)


def _mod_row(mod_ref, k, cond):
    return mod_ref[k, pl.ds(cond, 1), :]


def _dot(a, b):
    return jnp.dot(a, b, preferred_element_type=F32)


def _dot_nt(a, b):
    return lax.dot_general(a, b, (((1,), (1,)), ((), ())), preferred_element_type=F32)


def _rms(x, g):
    ms = jnp.mean(x * x, axis=-1, keepdims=True)
    return x * lax.rsqrt(ms + EPS) * g


def _group_mean(xx, width, group):
    r = lax.broadcasted_iota(jnp.int32, (width, width), 0) // group
    c = lax.broadcasted_iota(jnp.int32, (width, width), 1) // group
    gmat = jnp.where(r == c, 1.0 / group, 0.0).astype(BF16)
    hi = xx.astype(BF16)
    lo = (xx - hi.astype(F32)).astype(BF16)
    return _dot(hi, gmat) + _dot(lo, gmat)


def _const_spec(shape):
    return pl.BlockSpec(shape, lambda *_: (0,) * len(shape))


def _layer_spec(shape, l, **kw):
    return pl.BlockSpec((None,) + shape, lambda *_: (l,) + (0,) * len(shape), **kw)


_MOD_SHAPE = (N_MOD, N_COND, D_MODEL)


def _ada_kernel(c_ref, w_ref, b_ref, o_ref):
    cnd = c_ref[...]
    s = (cnd * jax.nn.sigmoid(cnd)).astype(BF16)
    for k in range(ADA_VECS):
        cols = slice(k * D_MODEL, (k + 1) * D_MODEL)
        o_ref[k] = _dot(s, w_ref[:, cols].astype(BF16)) + b_ref[k]


def _ada(conds, w_ada, b_ada):
    return pl.pallas_call(
        _ada_kernel,
        grid=(DEPTH, N_MOD // ADA_VECS),
        in_specs=[
            pl.BlockSpec((N_COND, D_MODEL), lambda l, n: (0, 0)),
            pl.BlockSpec((None, D_MODEL, ADA_VECS * D_MODEL), lambda l, n: (l, 0, n)),
            pl.BlockSpec((None, ADA_VECS, 1, D_MODEL), lambda l, n: (l, n, 0, 0)),
        ],
        out_specs=pl.BlockSpec((None, ADA_VECS, N_COND, D_MODEL), lambda l, n: (l, n, 0, 0)),
        out_shape=jax.ShapeDtypeStruct((DEPTH,) + _MOD_SHAPE, F32),
        compiler_params=pltpu.CompilerParams(
            dimension_semantics=("arbitrary", "arbitrary"), vmem_limit_bytes=VMEM_LIMIT),
        name="ada",
    )(conds, w_ada, b_ada.reshape(DEPTH, N_MOD, 1, D_MODEL))


class _WeightStager:
    def __init__(self, jobs, stage_ref, sem_ref):
        self.jobs, self.stage_ref, self.depth = jobs, stage_ref, stage_ref.shape[0]
        self.copies = [
            pltpu.make_async_copy(src, stage_ref.at[k % self.depth], sem_ref.at[k % self.depth])
            for k, (src, _) in enumerate(jobs)]

    def prime(self):
        for k in range(min(self.depth - 1, len(self.jobs))):
            self.copies[k].start()

    def finish(self, k):
        if k + self.depth - 1 < len(self.jobs):
            self.copies[k + self.depth - 1].start()
        self.copies[k].wait()
        self.jobs[k][1](self.stage_ref[k % self.depth].astype(BF16))


def _ffn_kernel(l, sub, tm, n_x, n_o, mix, *refs):
    refs = list(refs)
    x_refs = [refs.pop(0) for _ in range(n_x)]
    ac_ref, att_ref = (refs.pop(0), refs.pop(0)) if mix else (None, None)
    mod_ref, gpre_ref, gpost_ref, wg_hbm, wu_hbm, wd_hbm = [refs.pop(0) for _ in range(6)]
    wo_hbm = refs.pop(0) if mix else None
    o_refs = [refs.pop(0) for _ in range(n_o)]
    wg_ref, wu_ref, wd_ref = [refs.pop(0) for _ in range(3)]
    wo_ref = refs.pop(0) if mix else None
    i = pl.program_id(0)
    is_ctx = i < N_CTX // tm
    cond = _cond_of_tile(i, tm)
    c = FFN_CHUNK
    n_chunks = D_FF // c
    gate = _mod_row(mod_ref, 3 * sub + 2, cond)
    gpost = gpost_ref[pl.ds(sub, 1), :]

    def prep():
        if n_x == 1:
            x = x_refs[0][...]
        else:
            x = jnp.where(is_ctx, x_refs[0][...], x_refs[1][...])
        if mix:
            cat = jnp.concatenate([ac_ref[:, 0:256], att_ref[:, 0:256],
                                   ac_ref[:, 256:512], att_ref[:, 256:512]], axis=1)
            ym = _dot(cat, wo_ref[...])
            x = x + _mod_row(mod_ref, 5, cond) * _rms(ym, gpost_ref[pl.ds(1, 1), :])
        shift = _mod_row(mod_ref, 3 * sub, cond)
        scale = _mod_row(mod_ref, 3 * sub + 1, cond)
        h = (_rms(x, gpre_ref[pl.ds(sub, 1), :]) * (1.0 + scale) + shift).astype(BF16)
        return x, h

    def swiglu(hh, cols):
        g = _dot(hh, wg_ref[:, cols])
        u = _dot(hh, wu_ref[:, cols])
        a = ((g * jax.nn.sigmoid(g)) * u).astype(BF16)
        return _dot(a, wd_ref[cols, :])

    def emit(y):
        if n_o == 1:
            o_refs[0][...] = y
        else:
            @pl.when(is_ctx)
            def _():
                o_refs[0][...] = y

            @pl.when(jnp.logical_not(is_ctx))
            def _():
                o_refs[1][...] = y

    @pl.when(i == 0)
    def _():
        def col_store(dst, j):
            def store(v):
                dst[:, j * c:(j + 1) * c] = v
            return store

        def row_store(j):
            def store(v):
                wd_ref[j * c:(j + 1) * c, :] = v
            return store

        up_jobs = [(src.at[l, :, pl.ds(j * c, c)], col_store(dst, j))
                   for j in range(n_chunks) for src, dst in ((wg_hbm, wg_ref), (wu_hbm, wu_ref))]
        down_jobs = [(wd_hbm.at[l, pl.ds(j * c, c), :], row_store(j)) for j in range(n_chunks)]

        def run(stage_up, stage_down, sem_up, sem_down):
            up = _WeightStager(up_jobs, stage_up, sem_up)
            up.prime()
            if mix:
                def wo_store(j):
                    def store(v):
                        wo_ref[j * c:(j + 1) * c, :] = v
                    return store

                wo = _WeightStager(
                    [(wo_hbm.at[l, pl.ds(j * c, c), :], wo_store(j)) for j in range(D_MODEL // c)],
                    stage_down, sem_down)
                wo.prime()
                for j in range(D_MODEL // c):
                    wo.finish(j)
            down = _WeightStager(down_jobs, stage_down, sem_down)
            down.prime()
            x, h = prep()
            y = None
            for j in range(n_chunks):
                up.finish(2 * j)
                up.finish(2 * j + 1)
                down.finish(j)
                part = swiglu(h, slice(j * c, (j + 1) * c))
                y = part if y is None else y + part
            emit(x + 0.5 * (gate * _rms(y, gpost)))

        pl.run_scoped(run,
                      pltpu.VMEM((2 * FFN_STAGE_SLOTS, D_MODEL, c), F32),
                      pltpu.VMEM((FFN_STAGE_SLOTS, c, D_MODEL), F32),
                      pltpu.SemaphoreType.DMA((2 * FFN_STAGE_SLOTS,)),
                      pltpu.SemaphoreType.DMA((FFN_STAGE_SLOTS,)))

    @pl.when(i != 0)
    def _():
        x, h = prep()
        outs = []
        for s in range(tm // FFN_SUB):
            rows = slice(s * FFN_SUB, (s + 1) * FFN_SUB)
            y = swiglu(h[rows], slice(None))
            outs.append(x[rows] + 0.5 * (gate * _rms(y, gpost)))
        emit(jnp.concatenate(outs, axis=0))


def _ffn(xs, mod, norm_pre, norm_post, wg, wu, wd, l, sub, split_out=False, mix=None):
    tm = TM_FFN
    n_ctx_tiles = N_CTX // tm
    ctx_spec = pl.BlockSpec((tm, D_MODEL), lambda i: (jnp.minimum(i, n_ctx_tiles - 1), 0))
    lat_spec = pl.BlockSpec((tm, D_MODEL), lambda i: (jnp.maximum(i - n_ctx_tiles, 0), 0))
    full_spec = pl.BlockSpec((tm, D_MODEL), lambda i: (i, 0))
    x_specs = [full_spec] if len(xs) == 1 else [ctx_spec, lat_spec]
    if split_out:
        out_specs = [ctx_spec, lat_spec]
        out_shape = [jax.ShapeDtypeStruct((N_CTX, D_MODEL), F32),
                     jax.ShapeDtypeStruct((N_LAT, D_MODEL), F32)]
    else:
        out_specs = [full_spec]
        out_shape = [jax.ShapeDtypeStruct((N_TOK, D_MODEL), F32)]
    hbm = pl.BlockSpec(memory_space=pl.ANY)
    args = list(xs)
    in_specs = list(x_specs)
    scratch = [pltpu.VMEM((D_MODEL, D_FF), BF16), pltpu.VMEM((D_MODEL, D_FF), BF16),
               pltpu.VMEM((D_FF, D_MODEL), BF16)]
    if mix is not None:
        args += [mix[0], mix[1]]
        in_specs += [pl.BlockSpec((tm, 512), lambda i: (i, 0))] * 2
    args += [mod, norm_pre, norm_post, wg, wu, wd]
    in_specs += [_layer_spec(_MOD_SHAPE, l), _layer_spec((3, D_MODEL), l),
                 _layer_spec((3, D_MODEL), l), hbm, hbm, hbm]
    if mix is not None:
        args.append(mix[2])
        in_specs.append(hbm)
        scratch.append(pltpu.VMEM((D_MODEL, D_MODEL), BF16))
    return pl.pallas_call(
        functools.partial(_ffn_kernel, l, sub, tm, len(xs), len(out_specs), mix is not None),
        grid=(N_TOK // tm,),
        in_specs=in_specs,
        out_specs=out_specs,
        out_shape=out_shape,
        scratch_shapes=scratch,
        compiler_params=pltpu.CompilerParams(
            dimension_semantics=("arbitrary",), vmem_limit_bytes=VMEM_LIMIT),
        name=f"ffn{sub}",
    )(*args)


def _swap_blocks(x, blk):
    w = x.shape[-1]
    lane = lax.broadcasted_iota(jnp.int32, x.shape, x.ndim - 1)
    first = (lane % (2 * blk)) < blk
    return jnp.where(first, pltpu.roll(x, w - blk, x.ndim - 1), pltpu.roll(x, blk, x.ndim - 1))


def _rope_tables():
    out = []
    for rot_dim in (DK_DIFF, HD_GQA):
        n_freq = rot_dim // 4
        inv = ROPE_THETA ** (-np.arange(n_freq, dtype=np.float64) / n_freq)
        lane = np.arange(256) % rot_dim
        pos = lane % (rot_dim // 2)
        freq = inv[pos % n_freq]
        sign = np.where(pos < n_freq, -1.0, 1.0)
        coord = np.concatenate([np.arange(GRID_H), np.arange(GRID_W)]).astype(np.float64)
        ang = coord[:, None] * freq[None, :]
        out.append(np.cos(ang))
        out.append(np.sin(ang) * sign[None, :])
    return np.stack(out).astype(np.float32)


def _tile_rope(tab_ref, which, rot_dim, first_grid_row, n_grid_rows, identity):
    col_part = tab_ref[which, GRID_H:GRID_H + GRID_W, :]
    lane = lax.broadcasted_iota(jnp.int32, (1, 256), 1)
    on_row_axis = (lane % rot_dim) < rot_dim // 2
    tab = jnp.concatenate(
        [jnp.where(on_row_axis, tab_ref[which, pl.ds(first_grid_row + r, 1), :], col_part)
         for r in range(n_grid_rows)], axis=0)
    return jnp.where(identity, 1.0 - (which % 2), tab)


def _mixproj_kernel(l, has_prev, x_ref, mod_ref, gpre_ref, w_ref, qg_ref, kg_ref, tab_ref, *rest):
    n_alias = 4 if has_prev else 0
    (cin_ref, qt_ref, k_ref, vt_ref, odk_ref, odv_ref, ogk_ref, ogv_ref, wbf_ref) = rest[n_alias:]
    tm = TM_MIX
    i = pl.program_id(0)
    n_ctx_tiles = N_CTX // tm
    is_ctx = i < n_ctx_tiles
    cond = _cond_of_tile(i, tm)

    @pl.when(i == 0)
    def _():
        wbf_ref[...] = w_ref[...].astype(BF16)

    shift = _mod_row(mod_ref, 3, cond)
    scale = _mod_row(mod_ref, 4, cond)
    gpre = gpre_ref[pl.ds(1, 1), :]
    sub = MIX_SUB
    grid_rows = sub // GRID_W
    seqs = sub // SEQ
    tile_row = (jnp.maximum(i - n_ctx_tiles, 0) % (DEC_SEQ // tm)) * (tm // GRID_W)
    own = []

    for s in range(tm // sub):
        rows = slice(s * sub, (s + 1) * sub)
        h = (_rms(x_ref[rows, :], gpre) * (1.0 + scale) + shift).astype(BF16)

        a = _dot(h, wbf_ref[:, 0:768])
        cin_ref[rows, 0:256] = a[:, 0:256]
        cin_ref[rows, 256:512] = a[:, 256:512] * a[:, 512:768]
        c = _dot(h, wbf_ref[:, 1536:2048])
        cin_ref[rows, 512:768] = c[:, 0:256] * jax.nn.sigmoid(c[:, 256:512])

        first_row = tile_row + s * grid_rows
        cd = _tile_rope(tab_ref, 0, DK_DIFF, first_row, grid_rows, is_ctx)
        sd = _tile_rope(tab_ref, 1, DK_DIFF, first_row, grid_rows, is_ctx)
        cg = _tile_rope(tab_ref, 2, HD_GQA, first_row, grid_rows, is_ctx)
        sg = _tile_rope(tab_ref, 3, HD_GQA, first_row, grid_rows, is_ctx)

        b = _dot(h, wbf_ref[:, 768:1536])
        bq, bk, bv = b[:, 0:256], b[:, 256:512], b[:, 512:768]
        bq_r = (bq * cd + _swap_blocks(bq, DK_DIFF // 4) * sd) * (DK_DIFF ** -0.5 * LOG2E)
        bk_r = bk * cd + _swap_blocks(bk, DK_DIFF // 4) * sd
        qt_ref[0:256, rows] = bq_r.T.astype(BF16)

        d = _dot(h, wbf_ref[:, 2048:2560])
        dq, dk, dv = d[:, 0:256], d[:, 256:384], d[:, 384:512]
        dq_n = dq * lax.rsqrt(_group_mean(dq * dq, 256, HD_GQA) + EPS) * qg_ref[l:l + 1, :]
        dk_n = dk * lax.rsqrt(_group_mean(dk * dk, 128, HD_GQA) + EPS) * kg_ref[l:l + 1, :]
        dq_r = (dq_n * cg + _swap_blocks(dq_n, HD_GQA // 4) * sg) * (HD_GQA ** -0.5 * LOG2E)
        dk_r = dk_n * cg[:, 0:128] + _swap_blocks(dk_n, HD_GQA // 4) * sg[:, 0:128]
        qt_ref[256:512, rows] = dq_r.T.astype(BF16)

        k_ref[0, rows, :] = bk_r.astype(BF16)
        k_ref[1, rows, :] = jnp.concatenate([dk_r, jnp.zeros_like(dk_r)], axis=1).astype(BF16)
        vt_ref[0, :, rows] = bv.T.astype(BF16)
        vt_ref[1, :, rows] = jnp.concatenate([dv, jnp.zeros_like(dv)], axis=1).T.astype(BF16)
        own.append((bk, bv, dk_n, dv))

    @pl.when(is_ctx)
    def _():
        for s, (bk, bv, dk_n, dv) in enumerate(own):
            seq = slice(s * seqs, (s + 1) * seqs)
            odk_ref[seq] = bk.reshape(seqs, SEQ, 256)
            odv_ref[seq] = bv.reshape(seqs, SEQ, 256)
            ogk_ref[seq] = dk_n.reshape(seqs, SEQ, 128)
            ogv_ref[seq] = dv.reshape(seqs, SEQ, 128)


def _mixproj(x, mod, norm_pre, w_mix_in, qg, kg, tables, prev_own, l):
    tm = TM_MIX
    n_ctx_tiles = N_CTX // tm
    seqs = tm // SEQ

    ctx_row = lambda i: jnp.minimum(i, n_ctx_tiles - 1)
    row = lambda w: pl.BlockSpec((tm, w), lambda i: (i, 0))
    own_spec = lambda w: pl.BlockSpec((seqs, None, SEQ, w), lambda i: (ctx_row(i), l, 0, 0))
    own_shape = lambda w: jax.ShapeDtypeStruct((BATCH, DEPTH, SEQ, w), F32)
    any_spec = pl.BlockSpec(memory_space=pl.ANY)
    aliased = list(prev_own) if prev_own is not None else []
    n_in = 7
    aliases = {n_in + k: 4 + k for k in range(len(aliased))}
    return pl.pallas_call(
        functools.partial(_mixproj_kernel, l, prev_own is not None),
        grid=(N_TOK // tm,),
        in_specs=[
            row(D_MODEL),
            _layer_spec(_MOD_SHAPE, l),
            _layer_spec((3, D_MODEL), l),
            _layer_spec((D_MODEL, MIX_IN), l, pipeline_mode=pl.Buffered(1)),
            _const_spec((DEPTH, 256)),
            _const_spec((DEPTH, 128)),
            _const_spec(tables.shape),
        ] + [any_spec] * len(aliased),
        out_specs=[row(768),
                   pl.BlockSpec((512, tm), lambda i: (0, i)),
                   pl.BlockSpec((2, tm, 256), lambda i: (0, i, 0)),
                   pl.BlockSpec((2, 256, tm), lambda i: (0, 0, i)),
                   own_spec(256), own_spec(256), own_spec(128), own_spec(128)],
        out_shape=[
            jax.ShapeDtypeStruct((N_TOK, 768), F32),
            jax.ShapeDtypeStruct((512, N_TOK), BF16),
            jax.ShapeDtypeStruct((2, N_TOK, 256), BF16),
            jax.ShapeDtypeStruct((2, 256, N_TOK), BF16),
            own_shape(256), own_shape(256), own_shape(128), own_shape(128),
        ],
        scratch_shapes=[pltpu.VMEM((D_MODEL, MIX_IN), BF16)],
        input_output_aliases=aliases,
        compiler_params=pltpu.CompilerParams(
            dimension_semantics=("arbitrary",), vmem_limit_bytes=VMEM_LIMIT),
        name="mixproj",
    )(x, mod, norm_pre, w_mix_in, qg, kg, tables, *aliased)


N_UNITS = 12
ATTN_SLOTS = 4
ATTN_PIECE = 1024


def _attn_kernel(l, qt_ref, k_ref, vt_ref, kc_ref, vtc_ref, lq1_ref, lk1_ref, lq2_ref, lk2_ref,
                 subln_ref, *rest):
    n = ATTN_SLOTS
    o_ref, qs_ref = rest[-3 - 3 * n], rest[-2 - 3 * n]
    s_refs = rest[-1 - 3 * n:-1 - 2 * n]
    e_refs = rest[-1 - 2 * n:-1 - n]
    il_refs = rest[-1 - n:-1]
    oacc_ref = rest[-1]
    own = k_ref.shape[1]
    half = own // 2

    for u in range(N_UNITS):
        qs_ref[u] = _unit_queries(qt_ref, u)

    def scores(u):
        plane = 0 if u < 8 else 1
        s_ref = s_refs[u % n]
        for lo in range(0, own, ATTN_PIECE):
            s_ref[lo:lo + ATTN_PIECE, :] = _dot(k_ref[plane, lo:lo + ATTN_PIECE, :], qs_ref[u])
        s_ref[own:, :] = _dot(kc_ref[plane], qs_ref[u])

    def softmax(u):
        s = s_refs[u % n][...]
        e = jnp.exp2(s - jnp.max(s, axis=0, keepdims=True))
        il_refs[u % n][...] = 1.0 / jnp.sum(e, axis=0, keepdims=True)
        e_refs[u % n][...] = e.astype(BF16)

    def values(u):
        plane = 0 if u < 8 else 1
        vrow = DV_DIFF * (u // 2) if u < 8 else HD_GQA * ((u - 8) // 2)
        e_ref = e_refs[u % n]
        o = (_dot(vt_ref[plane, vrow:vrow + DV_DIFF, 0:half], e_ref[0:half, :])
             + _dot(vt_ref[plane, vrow:vrow + DV_DIFF, half:], e_ref[half:own, :])
             + _dot(vtc_ref[plane, vrow:vrow + DV_DIFF, :], e_ref[own:, :]))
        oacc_ref[u] = o * il_refs[u % n][...]

    lag = ATTN_SLOTS // 2
    for t in range(N_UNITS + 2 * lag):
        if t < N_UNITS:
            scores(t)
        if lag <= t < N_UNITS + lag:
            softmax(t - lag)
        if t >= 2 * lag:
            values(t - 2 * lag)

    _attn_finish(l, [oacc_ref[u] for u in range(N_UNITS)],
                 lq1_ref, lk1_ref, lq2_ref, lk2_ref, subln_ref, o_ref)


def _unit_queries(qt_ref, u, cols=slice(None)):
    if u < 8:
        feat = lax.broadcasted_iota(jnp.int32, (256, 1), 0)
        return jnp.where(feat // DK_DIFF == u, qt_ref[0:256, cols], jnp.zeros((), BF16))
    hd = u - 8
    g = hd // 2
    rows = qt_ref[256 + HD_GQA * hd:256 + HD_GQA * (hd + 1), cols]
    tq = rows.shape[1]
    pieces = ([jnp.zeros((HD_GQA * g, tq), BF16)] if g else []) + [rows]
    pieces.append(jnp.zeros((256 - HD_GQA * (g + 1), tq), BF16))
    return jnp.concatenate(pieces, axis=0)


def _attn_finish(l, outs, lq1_ref, lk1_ref, lq2_ref, lk2_ref, subln_ref, o_ref, rows=slice(None)):
    lam_init = 0.8 - 0.6 * math.exp(-0.3 * l)
    row = slice(l, l + 1)
    lam = (jnp.exp(jnp.sum(lq1_ref[row, :] * lk1_ref[row, :], axis=-1, keepdims=True))
           - jnp.exp(jnp.sum(lq2_ref[row, :] * lk2_ref[row, :], axis=-1, keepdims=True))
           + lam_init)
    heads = []
    for hd in range(4):
        o = outs[2 * hd] - lam * outs[2 * hd + 1]
        nrm = lax.rsqrt(jnp.mean(o * o, axis=0, keepdims=True) + EPS)
        heads.append(o * nrm * (subln_ref[:, l:l + 1] * (1.0 - lam_init)))
    heads += outs[8:12]
    o_ref[rows, :] = jnp.concatenate(heads, axis=0).T.astype(BF16)


def _attn_ctx_kernel(l, qt_ref, k_ref, vt_ref, lq1_ref, lk1_ref, lq2_ref, lk2_ref, subln_ref,
                     o_ref):
    for q in range(qt_ref.shape[1] // SEQ):
        tok = slice(q * SEQ, (q + 1) * SEQ)
        w_diff = jnp.concatenate([_unit_queries(qt_ref, u, tok) for u in range(8)], axis=1)
        w_gqa = jnp.concatenate([_unit_queries(qt_ref, u, tok) for u in range(8, N_UNITS)], axis=1)
        s = jnp.concatenate([_dot(k_ref[0, tok, :], w_diff), _dot(k_ref[1, tok, :], w_gqa)],
                            axis=1)
        e = jnp.exp2(s - jnp.max(s, axis=0, keepdims=True))
        il = 1.0 / jnp.sum(e, axis=0, keepdims=True)
        e = e.astype(BF16)
        outs = []
        for u in range(N_UNITS):
            plane = 0 if u < 8 else 1
            vrow = DV_DIFF * (u // 2) if u < 8 else HD_GQA * ((u - 8) // 2)
            cols = slice(u * SEQ, (u + 1) * SEQ)
            outs.append(_dot(vt_ref[plane, vrow:vrow + DV_DIFF, tok], e[:, cols]) * il[:, cols])
        _attn_finish(l, outs, lq1_ref, lk1_ref, lq2_ref, lk2_ref, subln_ref, o_ref, tok)


def _attn_param_specs():
    vec = _const_spec((DEPTH, DK_DIFF))
    return [vec, vec, vec, vec, _const_spec((DV_DIFF, DEPTH))]


def _attn_scratch(tq, s_len):
    n = ATTN_SLOTS
    return ([pltpu.VMEM((N_UNITS, 256, tq), BF16)]
            + [pltpu.VMEM((s_len, tq), F32)] * n
            + [pltpu.VMEM((s_len, tq), BF16)] * n
            + [pltpu.VMEM((1, tq), F32)] * n
            + [pltpu.VMEM((N_UNITS, DV_DIFF, tq), F32)])


def _attn_ctx(qt, k_ctx, vt_ctx, aparams, l):
    t = CTX_SEQS_PER_STEP * SEQ
    return pl.pallas_call(
        functools.partial(_attn_ctx_kernel, l),
        grid=(N_CTX // t,),
        in_specs=[pl.BlockSpec((512, t), lambda b: (0, b)),
                  pl.BlockSpec((2, t, 256), lambda b: (0, b, 0)),
                  pl.BlockSpec((2, 256, t), lambda b: (0, 0, b))] + _attn_param_specs(),
        out_specs=pl.BlockSpec((t, 512), lambda b: (b, 0)),
        out_shape=jax.ShapeDtypeStruct((N_TOK, 512), BF16),
        compiler_params=pltpu.CompilerParams(
            dimension_semantics=("arbitrary",), vmem_limit_bytes=VMEM_LIMIT),
        name="attn_ctx",
    )(qt, k_ctx, vt_ctx, *aparams)


def _attn_lat(qt, k_all, vt_all, cache_k, cache_vt, att, aparams, l):
    tq = TQ_LAT
    nq = DEC_SEQ // tq
    tile = lambda b, i: N_CTX // tq + b * nq + i
    own = lambda b: N_CTX // DEC_SEQ + b
    return pl.pallas_call(
        functools.partial(_attn_kernel, l),
        grid=(DEC_BATCH, nq),
        in_specs=[pl.BlockSpec((512, tq), lambda b, i: (0, tile(b, i))),
                  pl.BlockSpec((2, DEC_SEQ, 256), lambda b, i: (0, own(b), 0)),
                  pl.BlockSpec((2, 256, DEC_SEQ), lambda b, i: (0, 0, own(b))),
                  pl.BlockSpec((None, None, 2, PAST_LEN, 256), lambda b, i: (b, l, 0, 0, 0)),
                  pl.BlockSpec((None, None, 2, 256, PAST_LEN), lambda b, i: (b, l, 0, 0, 0))]
        + _attn_param_specs() + [pl.BlockSpec(memory_space=pl.ANY)],
        out_specs=pl.BlockSpec((tq, 512), lambda b, i: (tile(b, i), 0)),
        out_shape=jax.ShapeDtypeStruct((N_TOK, 512), BF16),
        scratch_shapes=_attn_scratch(tq, S_LAT),
        input_output_aliases={10: 0},
        compiler_params=pltpu.CompilerParams(
            dimension_semantics=("arbitrary", "arbitrary"), vmem_limit_bytes=VMEM_LIMIT),
        name="attn_lat",
    )(qt, k_all, vt_all, cache_k, cache_vt, *aparams, att)


def _conv_kernel(l, *refs):
    is_ctx = pl.program_id(0) < N_CTX // T_CONV

    @pl.when(is_ctx)
    def _():
        _conv_tile(l, SEQ, *refs)

    @pl.when(jnp.logical_not(is_ctx))
    def _():
        _conv_tile(l, DEC_SEQ, *refs)


def _conv_tile(l, seq_len, cin_ref, wsc_ref, wdw_ref, bdw_ref, lng_ref, lnb_ref, pw_ref, o_ref,
               pad_ref):
    t_tile = cin_ref.shape[0]
    n_seq = t_tile // seq_len
    stride = seq_len + CONV_HALO
    chunks_per_seq = seq_len // R_CONV
    rows = R_CONV

    for q in range(n_seq + 1):
        pad_ref[q * stride:q * stride + CONV_HALO, :] = jnp.zeros((CONV_HALO, 512), F32)
    for q in range(n_seq):
        pad_ref[CONV_HALO + q * stride:CONV_HALO + q * stride + seq_len, :] = (
            cin_ref[q * seq_len:(q + 1) * seq_len, 256:768])

    pw = pw_ref[...].astype(BF16)
    bdw = bdw_ref[l:l + 1, :]
    lng = lng_ref[l:l + 1, :]
    lnb = lnb_ref[l:l + 1, :]

    def body(c, carry):
        q = c // chunks_per_seq
        r = c % chunks_per_seq
        base = pl.multiple_of(q * stride + r * rows, 8)
        row0 = pl.multiple_of(c * rows, 8)
        win = pad_ref[pl.ds(base, rows + 2 * CONV_HALO), :]
        win_s = win[:, 0:256]
        win_u = win[:, 256:512]

        s_acc = None
        for k in range(SCONV_K):
            off = CONV_HALO + k - SCONV_K // 2
            term = wsc_ref[pl.ds(k, 1), :] * win_s[off:off + rows]
            s_acc = term if s_acc is None else s_acc + term
        a_out = cin_ref[pl.ds(row0, rows), 0:256] * s_acc

        acc = None
        for res in range(8):
            part = None
            for a in range(-2, 2):
                k = 8 * a + res + CCM_K // 2
                if 0 <= k < CCM_K:
                    lo = CONV_HALO + 8 * a
                    term = wdw_ref[pl.ds(k, 1), :] * win_u[lo:lo + rows + 8]
                    part = term if part is None else part + term
            part = part[res:res + rows]
            acc = part if acc is None else acc + part
        u = acc + bdw
        mu = jnp.mean(u, axis=-1, keepdims=True)
        var = jnp.mean(jnp.square(u - mu), axis=-1, keepdims=True)
        y = (u - mu) * lax.rsqrt(var + EPS) * lng + lnb
        y = y * jax.nn.sigmoid(y)
        c_out = _dot(y.astype(BF16), pw)
        o_ref[pl.ds(row0, rows), 0:256] = a_out.astype(BF16)
        o_ref[pl.ds(row0, rows), 256:512] = c_out.astype(BF16)
        return carry

    lax.fori_loop(0, t_tile // rows, body, 0, unroll=2)


def _conv(cin, cparams, l):
    t = T_CONV
    vec = _const_spec((DEPTH, 256))
    pad_rows = max((t // s) * (s + CONV_HALO) + CONV_HALO for s in (SEQ, DEC_SEQ))
    return pl.pallas_call(
        functools.partial(_conv_kernel, l),
        grid=(N_TOK // t,),
        in_specs=[pl.BlockSpec((t, 768), lambda i: (i, 0)),
                  _layer_spec((SCONV_K, 256), l), _layer_spec((CCM_K, 256), l), vec, vec, vec,
                  _layer_spec((256, 256), l)],
        out_specs=pl.BlockSpec((t, 512), lambda i: (i, 0)),
        out_shape=jax.ShapeDtypeStruct((N_TOK, 512), BF16),
        scratch_shapes=[pltpu.VMEM((pad_rows, 512), F32)],
        compiler_params=pltpu.CompilerParams(
            dimension_semantics=("arbitrary",), vmem_limit_bytes=VMEM_LIMIT),
        name="conv",
    )(cin, *cparams)


def _cache_planes(cache_diff_k, cache_diff_v, cache_gqa_k, cache_gqa_v):
    lead = (DEC_BATCH, DEPTH, PAST_LEN)
    pad128 = lambda a: jnp.pad(a.reshape(lead + (128,)), ((0, 0), (0, 0), (0, 0), (0, 128)))
    keys = jnp.stack([cache_diff_k.reshape(lead + (256,)), pad128(cache_gqa_k)], axis=2)
    vals = jnp.stack([cache_diff_v.reshape(lead + (256,)), pad128(cache_gqa_v)], axis=2)
    return keys.astype(BF16), jnp.swapaxes(vals, 3, 4).astype(BF16)


def kernel(x_prompt, x_sample, cache_diff_k, cache_diff_v, cache_gqa_k, cache_gqa_v, c, c_ctx, w_ada, b_ada, norm_pre, norm_post, ffn1_gate, ffn1_up, ffn1_down, ffn2_gate, ffn2_up, ffn2_down, w_mix_in, w_mix_out, sconv_w, diff_lq1, diff_lk1, diff_lq2, diff_lk2, diff_subln, ccm_dw_w, ccm_dw_b, ccm_ln_g, ccm_ln_b, ccm_pw, gqa_qnorm, gqa_knorm):
    conds = jnp.concatenate(
        [c_ctx[None, :], c, jnp.zeros((N_COND - 1 - DEC_BATCH, D_MODEL), F32)], axis=0)
    mod = _ada(conds, w_ada, b_ada)
    tables = jnp.asarray(_rope_tables())
    qg = jnp.tile(gqa_qnorm, (1, 4))
    kg = jnp.tile(gqa_knorm, (1, 2))
    aparams = (diff_lq1, diff_lk1, diff_lq2, diff_lk2, diff_subln.T)
    cparams = (sconv_w, ccm_dw_w, ccm_dw_b, ccm_ln_g, ccm_ln_b, ccm_pw)
    cache_k, cache_vt = _cache_planes(cache_diff_k, cache_diff_v, cache_gqa_k, cache_gqa_v)

    xs = (x_prompt.reshape(N_CTX, D_MODEL), x_sample.reshape(N_LAT, D_MODEL))
    own = None
    for l in range(DEPTH):
        (x,) = _ffn(xs, mod, norm_pre, norm_post, ffn1_gate, ffn1_up, ffn1_down, l, 0)

        cin, qt, k_all, vt_all, *own = _mixproj(
            x, mod, norm_pre, w_mix_in, qg, kg, tables, own, l)

        att = _attn_ctx(qt, k_all, vt_all, aparams, l)
        att = _attn_lat(qt, k_all, vt_all, cache_k, cache_vt, att, aparams, l)

        ac = _conv(cin, cparams, l)

        xs = _ffn((x,), mod, norm_pre, norm_post, ffn2_gate, ffn2_up, ffn2_down, l, 2,
                  split_out=(l == DEPTH - 1), mix=(ac, att, w_mix_out))

    y_prompt, y_sample = xs
    odk, odv, ogk, ogv = own
    return (y_prompt.reshape(BATCH, SEQ, D_MODEL),
            y_sample.reshape(DEC_BATCH, DEC_SEQ, D_MODEL),
            odk.reshape(BATCH, DEPTH, SEQ, 4, 2, DK_DIFF),
            odv.reshape(BATCH, DEPTH, SEQ, 4, DV_DIFF),
            ogk.reshape(BATCH, DEPTH, SEQ, 2, HD_GQA),
            ogv.reshape(BATCH, DEPTH, SEQ, 2, HD_GQA))
```

```python
import functools
import math

import numpy as np

import jax
import jax.numpy as jnp
from jax import lax
from jax.experimental import pallas as pl
from jax.experimental.pallas import tpu as pltpu

F32 = jnp.float32
BF16 = jnp.bfloat16

D_MODEL = 1024
BATCH = 16
SEQ = 256
DEPTH = 2
DEC_BATCH = 2
DEC_SEQ = 2048
PAST_LEN = 256
GRID_W = 64
GRID_H = DEC_SEQ // GRID_W
ROPE_THETA = 10000.0
EPS = 1e-6
N_MOD = 9
D_FF = 2816
DK_DIFF = 32
DV_DIFF = 64
HD_GQA = 64
CCM_K = 31
SCONV_K = 3
MIX_IN = 2560

N_CTX = BATCH * SEQ
N_LAT = DEC_BATCH * DEC_SEQ
N_TOK = N_CTX + N_LAT
N_COND = 8
S_LAT = DEC_SEQ + PAST_LEN

LOG2E = math.log2(math.e)

VMEM_LIMIT = 56 * 1024 * 1024

TM_FFN = 512
FFN_SUB = 256
FFN_CHUNK = 256
FFN_STAGE_SLOTS = 3
TM_MIX = 1024
MIX_SUB = 256
ADA_VECS = 3
TQ_LAT = 256
CTX_SEQS_PER_STEP = 4
T_CONV = 2048
T_CONV_SUB = 512
R_CONV = 128
CONV_HALO = 16


def _cond_of_tile(i, tm):
    n_ctx_tiles = N_CTX // tm
    per_b = DEC_SEQ // tm
    return jnp.where(i < n_ctx_tiles, 0, 1 + (i - n_ctx_tiles) // per_b)


def _mod_row(mod_ref, k, cond):
    return mod_ref[k, pl.ds(cond, 1), :]


def _dot(a, b):
    return jnp.dot(a, b, preferred_element_type=F32)


def _rms(x, g):
    ms = jnp.mean(x * x, axis=-1, keepdims=True)
    return x * lax.rsqrt(ms + EPS) * g


def _group_mean(xx, width, group):
    r = lax.broadcasted_iota(jnp.int32, (width, width), 0) // group
    c = lax.broadcasted_iota(jnp.int32, (width, width), 1) // group
    gmat = jnp.where(r == c, 1.0 / group, 0.0).astype(BF16)
    hi = xx.astype(BF16)
    lo = (xx - hi.astype(F32)).astype(BF16)
    return _dot(hi, gmat) + _dot(lo, gmat)


def _const_spec(shape):
    return pl.BlockSpec(shape, lambda *_: (0,) * len(shape))


def _layer_spec(shape, l, **kw):
    return pl.BlockSpec((None,) + shape, lambda *_: (l,) + (0,) * len(shape), **kw)


_MOD_SHAPE = (N_MOD, N_COND, D_MODEL)


def _ada_kernel(c_ref, w_ref, b_ref, o_ref):
    cnd = c_ref[...]
    s = (cnd * jax.nn.sigmoid(cnd)).astype(BF16)
    for k in range(ADA_VECS):
        cols = slice(k * D_MODEL, (k + 1) * D_MODEL)
        o_ref[k] = _dot(s, w_ref[:, cols].astype(BF16)) + b_ref[k]


def _ada(conds, w_ada, b_ada):
    return pl.pallas_call(
        _ada_kernel,
        grid=(DEPTH, N_MOD // ADA_VECS),
        in_specs=[
            pl.BlockSpec((N_COND, D_MODEL), lambda l, n: (0, 0)),
            pl.BlockSpec((None, D_MODEL, ADA_VECS * D_MODEL), lambda l, n: (l, 0, n)),
            pl.BlockSpec((None, ADA_VECS, 1, D_MODEL), lambda l, n: (l, n, 0, 0)),
        ],
        out_specs=pl.BlockSpec((None, ADA_VECS, N_COND, D_MODEL), lambda l, n: (l, n, 0, 0)),
        out_shape=jax.ShapeDtypeStruct((DEPTH,) + _MOD_SHAPE, F32),
        compiler_params=pltpu.CompilerParams(
            dimension_semantics=("arbitrary", "arbitrary"), vmem_limit_bytes=VMEM_LIMIT),
        name="ada",
    )(conds, w_ada, b_ada.reshape(DEPTH, N_MOD, 1, D_MODEL))


class _WeightStager:
    def __init__(self, jobs, stage_ref, sem_ref):
        self.jobs, self.stage_ref, self.depth = jobs, stage_ref, stage_ref.shape[0]
        self.copies = [
            pltpu.make_async_copy(src, stage_ref.at[k % self.depth], sem_ref.at[k % self.depth])
            for k, (src, _) in enumerate(jobs)]

    def prime(self):
        for k in range(min(self.depth - 1, len(self.jobs))):
            self.copies[k].start()

    def finish(self, k):
        if k + self.depth - 1 < len(self.jobs):
            self.copies[k + self.depth - 1].start()
        self.copies[k].wait()
        self.jobs[k][1](self.stage_ref[k % self.depth].astype(BF16))


def _ffn_kernel(l, sub, tm, n_x, n_o, mix, *refs):
    refs = list(refs)
    x_refs = [refs.pop(0) for _ in range(n_x)]
    ac_ref, att_ref = (refs.pop(0), refs.pop(0)) if mix else (None, None)
    mod_ref, gpre_ref, gpost_ref, wg_hbm, wu_hbm, wd_hbm = [refs.pop(0) for _ in range(6)]
    wo_hbm = refs.pop(0) if mix else None
    o_refs = [refs.pop(0) for _ in range(n_o)]
    wg_ref, wu_ref, wd_ref = [refs.pop(0) for _ in range(3)]
    wo_ref = refs.pop(0) if mix else None
    i = pl.program_id(0)
    is_ctx = i < N_CTX // tm
    cond = _cond_of_tile(i, tm)
    c = FFN_CHUNK
    n_chunks = D_FF // c
    gate = _mod_row(mod_ref, 3 * sub + 2, cond)
    gpost = gpost_ref[pl.ds(sub, 1), :]

    def prep():
        if n_x == 1:
            x = x_refs[0][...]
        else:
            x = jnp.where(is_ctx, x_refs[0][...], x_refs[1][...])
        if mix:
            cat = jnp.concatenate([ac_ref[:, 0:256], att_ref[:, 0:256],
                                   ac_ref[:, 256:512], att_ref[:, 256:512]], axis=1)
            ym = _dot(cat, wo_ref[...])
            x = x + _mod_row(mod_ref, 5, cond) * _rms(ym, gpost_ref[pl.ds(1, 1), :])
        shift = _mod_row(mod_ref, 3 * sub, cond)
        scale = _mod_row(mod_ref, 3 * sub + 1, cond)
        h = (_rms(x, gpre_ref[pl.ds(sub, 1), :]) * (1.0 + scale) + shift).astype(BF16)
        return x, h

    def swiglu(hh, cols):
        g = _dot(hh, wg_ref[:, cols])
        u = _dot(hh, wu_ref[:, cols])
        a = ((g * jax.nn.sigmoid(g)) * u).astype(BF16)
        return _dot(a, wd_ref[cols, :])

    def emit(y):
        if n_o == 1:
            o_refs[0][...] = y
        else:
            @pl.when(is_ctx)
            def _():
                o_refs[0][...] = y

            @pl.when(jnp.logical_not(is_ctx))
            def _():
                o_refs[1][...] = y

    @pl.when(i == 0)
    def _():
        def col_store(dst, j):
            def store(v):
                dst[:, j * c:(j + 1) * c] = v
            return store

        def row_store(j):
            def store(v):
                wd_ref[j * c:(j + 1) * c, :] = v
            return store

        up_jobs = [(src.at[l, :, pl.ds(j * c, c)], col_store(dst, j))
                   for j in range(n_chunks) for src, dst in ((wg_hbm, wg_ref), (wu_hbm, wu_ref))]
        down_jobs = [(wd_hbm.at[l, pl.ds(j * c, c), :], row_store(j)) for j in range(n_chunks)]

        def run(stage_up, stage_down, sem_up, sem_down):
            up = _WeightStager(up_jobs, stage_up, sem_up)
            up.prime()
            if mix:
                def wo_store(j):
                    def store(v):
                        wo_ref[j * c:(j + 1) * c, :] = v
                    return store

                wo = _WeightStager(
                    [(wo_hbm.at[l, pl.ds(j * c, c), :], wo_store(j)) for j in range(D_MODEL // c)],
                    stage_down, sem_down)
                wo.prime()
                for j in range(D_MODEL // c):
                    wo.finish(j)
            down = _WeightStager(down_jobs, stage_down, sem_down)
            down.prime()
            x, h = prep()
            y = None
            for j in range(n_chunks):
                up.finish(2 * j)
                up.finish(2 * j + 1)
                down.finish(j)
                part = swiglu(h, slice(j * c, (j + 1) * c))
                y = part if y is None else y + part
            emit(x + 0.5 * (gate * _rms(y, gpost)))

        pl.run_scoped(run,
                      pltpu.VMEM((2 * FFN_STAGE_SLOTS, D_MODEL, c), F32),
                      pltpu.VMEM((FFN_STAGE_SLOTS, c, D_MODEL), F32),
                      pltpu.SemaphoreType.DMA((2 * FFN_STAGE_SLOTS,)),
                      pltpu.SemaphoreType.DMA((FFN_STAGE_SLOTS,)))

    @pl.when(i != 0)
    def _():
        x, h = prep()
        outs = []
        for s in range(tm // FFN_SUB):
            rows = slice(s * FFN_SUB, (s + 1) * FFN_SUB)
            y = swiglu(h[rows], slice(None))
            outs.append(x[rows] + 0.5 * (gate * _rms(y, gpost)))
        emit(jnp.concatenate(outs, axis=0))


def _ffn(xs, mod, norm_pre, norm_post, wg, wu, wd, l, sub, split_out=False, mix=None):
    tm = TM_FFN
    n_ctx_tiles = N_CTX // tm
    ctx_spec = pl.BlockSpec((tm, D_MODEL), lambda i: (jnp.minimum(i, n_ctx_tiles - 1), 0))
    lat_spec = pl.BlockSpec((tm, D_MODEL), lambda i: (jnp.maximum(i - n_ctx_tiles, 0), 0))
    full_spec = pl.BlockSpec((tm, D_MODEL), lambda i: (i, 0))
    x_specs = [full_spec] if len(xs) == 1 else [ctx_spec, lat_spec]
    if split_out:
        out_specs = [ctx_spec, lat_spec]
        out_shape = [jax.ShapeDtypeStruct((N_CTX, D_MODEL), F32),
                     jax.ShapeDtypeStruct((N_LAT, D_MODEL), F32)]
    else:
        out_specs = [full_spec]
        out_shape = [jax.ShapeDtypeStruct((N_TOK, D_MODEL), F32)]
    hbm = pl.BlockSpec(memory_space=pl.ANY)
    args = list(xs)
    in_specs = list(x_specs)
    scratch = [pltpu.VMEM((D_MODEL, D_FF), BF16), pltpu.VMEM((D_MODEL, D_FF), BF16),
               pltpu.VMEM((D_FF, D_MODEL), BF16)]
    if mix is not None:
        args += [mix[0], mix[1]]
        in_specs += [pl.BlockSpec((tm, 512), lambda i: (i, 0))] * 2
    args += [mod, norm_pre, norm_post, wg, wu, wd]
    in_specs += [_layer_spec(_MOD_SHAPE, l), _layer_spec((3, D_MODEL), l),
                 _layer_spec((3, D_MODEL), l), hbm, hbm, hbm]
    if mix is not None:
        args.append(mix[2])
        in_specs.append(hbm)
        scratch.append(pltpu.VMEM((D_MODEL, D_MODEL), BF16))
    return pl.pallas_call(
        functools.partial(_ffn_kernel, l, sub, tm, len(xs), len(out_specs), mix is not None),
        grid=(N_TOK // tm,),
        in_specs=in_specs,
        out_specs=out_specs,
        out_shape=out_shape,
        scratch_shapes=scratch,
        compiler_params=pltpu.CompilerParams(
            dimension_semantics=("arbitrary",), vmem_limit_bytes=VMEM_LIMIT),
        name=f"ffn{sub}",
    )(*args)


def _swap_blocks(x, blk):
    w = x.shape[-1]
    lane = lax.broadcasted_iota(jnp.int32, x.shape, x.ndim - 1)
    first = (lane % (2 * blk)) < blk
    return jnp.where(first, pltpu.roll(x, w - blk, x.ndim - 1), pltpu.roll(x, blk, x.ndim - 1))


def _rope_tables():
    out = []
    for rot_dim in (DK_DIFF, HD_GQA):
        n_freq = rot_dim // 4
        inv = ROPE_THETA ** (-np.arange(n_freq, dtype=np.float64) / n_freq)
        lane = np.arange(256) % rot_dim
        pos = lane % (rot_dim // 2)
        freq = inv[pos % n_freq]
        sign = np.where(pos < n_freq, -1.0, 1.0)
        coord = np.concatenate([np.arange(GRID_H), np.arange(GRID_W)]).astype(np.float64)
        ang = coord[:, None] * freq[None, :]
        out.append(np.cos(ang))
        out.append(np.sin(ang) * sign[None, :])
    return np.stack(out).astype(np.float32)


def _tile_rope(tab_ref, which, rot_dim, first_grid_row, n_grid_rows, identity):
    col_part = tab_ref[which, GRID_H:GRID_H + GRID_W, :]
    lane = lax.broadcasted_iota(jnp.int32, (1, 256), 1)
    on_row_axis = (lane % rot_dim) < rot_dim // 2
    tab = jnp.concatenate(
        [jnp.where(on_row_axis, tab_ref[which, pl.ds(first_grid_row + r, 1), :], col_part)
         for r in range(n_grid_rows)], axis=0)
    return jnp.where(identity, 1.0 - (which % 2), tab)


def _mixproj_kernel(l, has_prev, x_ref, mod_ref, gpre_ref, w_ref, qg_ref, kg_ref, tab_ref, *rest):
    n_alias = 4 if has_prev else 0
    (cin_ref, qt_ref, k_ref, vt_ref, odk_ref, odv_ref, ogk_ref, ogv_ref, wbf_ref) = rest[n_alias:]
    tm = TM_MIX
    i = pl.program_id(0)
    n_ctx_tiles = N_CTX // tm
    is_ctx = i < n_ctx_tiles
    cond = _cond_of_tile(i, tm)

    @pl.when(i == 0)
    def _():
        wbf_ref[...] = w_ref[...].astype(BF16)

    shift = _mod_row(mod_ref, 3, cond)
    scale = _mod_row(mod_ref, 4, cond)
    gpre = gpre_ref[pl.ds(1, 1), :]
    sub = MIX_SUB
    grid_rows = sub // GRID_W
    seqs = sub // SEQ
    tile_row = (jnp.maximum(i - n_ctx_tiles, 0) % (DEC_SEQ // tm)) * (tm // GRID_W)
    own = []

    for s in range(tm // sub):
        rows = slice(s * sub, (s + 1) * sub)
        h = (_rms(x_ref[rows, :], gpre) * (1.0 + scale) + shift).astype(BF16)

        a = _dot(h, wbf_ref[:, 0:768])
        cin_ref[rows, 0:256] = a[:, 0:256]
        cin_ref[rows, 256:512] = a[:, 256:512] * a[:, 512:768]
        c = _dot(h, wbf_ref[:, 1536:2048])
        cin_ref[rows, 512:768] = c[:, 0:256] * jax.nn.sigmoid(c[:, 256:512])

        first_row = tile_row + s * grid_rows
        cd = _tile_rope(tab_ref, 0, DK_DIFF, first_row, grid_rows, is_ctx)
        sd = _tile_rope(tab_ref, 1, DK_DIFF, first_row, grid_rows, is_ctx)
        cg = _tile_rope(tab_ref, 2, HD_GQA, first_row, grid_rows, is_ctx)
        sg = _tile_rope(tab_ref, 3, HD_GQA, first_row, grid_rows, is_ctx)

        b = _dot(h, wbf_ref[:, 768:1536])
        bq, bk, bv = b[:, 0:256], b[:, 256:512], b[:, 512:768]
        bq_r = (bq * cd + _swap_blocks(bq, DK_DIFF // 4) * sd) * (DK_DIFF ** -0.5 * LOG2E)
        bk_r = bk * cd + _swap_blocks(bk, DK_DIFF // 4) * sd
        qt_ref[0:256, rows] = bq_r.T.astype(BF16)

        d = _dot(h, wbf_ref[:, 2048:2560])
        dq, dk, dv = d[:, 0:256], d[:, 256:384], d[:, 384:512]
        dq_n = dq * lax.rsqrt(_group_mean(dq * dq, 256, HD_GQA) + EPS) * qg_ref[l:l + 1, :]
        dk_n = dk * lax.rsqrt(_group_mean(dk * dk, 128, HD_GQA) + EPS) * kg_ref[l:l + 1, :]
        dq_r = (dq_n * cg + _swap_blocks(dq_n, HD_GQA // 4) * sg) * (HD_GQA ** -0.5 * LOG2E)
        dk_r = dk_n * cg[:, 0:128] + _swap_blocks(dk_n, HD_GQA // 4) * sg[:, 0:128]
        qt_ref[256:512, rows] = dq_r.T.astype(BF16)

        k_ref[0, rows, :] = bk_r.astype(BF16)
        k_ref[1, rows, :] = jnp.concatenate([dk_r, jnp.zeros_like(dk_r)], axis=1).astype(BF16)
        vt_ref[0, :, rows] = bv.T.astype(BF16)
        vt_ref[1, :, rows] = jnp.concatenate([dv, jnp.zeros_like(dv)], axis=1).T.astype(BF16)
        own.append((bk, bv, dk_n, dv))

    @pl.when(is_ctx)
    def _():
        for s, (bk, bv, dk_n, dv) in enumerate(own):
            seq = slice(s * seqs, (s + 1) * seqs)
            odk_ref[seq] = bk.reshape(seqs, SEQ, 256)
            odv_ref[seq] = bv.reshape(seqs, SEQ, 256)
            ogk_ref[seq] = dk_n.reshape(seqs, SEQ, 128)
            ogv_ref[seq] = dv.reshape(seqs, SEQ, 128)


def _mixproj(x, mod, norm_pre, w_mix_in, qg, kg, tables, prev_own, l):
    tm = TM_MIX
    n_ctx_tiles = N_CTX // tm
    seqs = tm // SEQ

    ctx_row = lambda i: jnp.minimum(i, n_ctx_tiles - 1)
    row = lambda w: pl.BlockSpec((tm, w), lambda i: (i, 0))
    own_spec = lambda w: pl.BlockSpec((seqs, None, SEQ, w), lambda i: (ctx_row(i), l, 0, 0))
    own_shape = lambda w: jax.ShapeDtypeStruct((BATCH, DEPTH, SEQ, w), F32)
    any_spec = pl.BlockSpec(memory_space=pl.ANY)
    aliased = list(prev_own) if prev_own is not None else []
    n_in = 7
    aliases = {n_in + k: 4 + k for k in range(len(aliased))}
    return pl.pallas_call(
        functools.partial(_mixproj_kernel, l, prev_own is not None),
        grid=(N_TOK // tm,),
        in_specs=[
            row(D_MODEL),
            _layer_spec(_MOD_SHAPE, l),
            _layer_spec((3, D_MODEL), l),
            _layer_spec((D_MODEL, MIX_IN), l, pipeline_mode=pl.Buffered(1)),
            _const_spec((DEPTH, 256)),
            _const_spec((DEPTH, 128)),
            _const_spec(tables.shape),
        ] + [any_spec] * len(aliased),
        out_specs=[row(768),
                   pl.BlockSpec((512, tm), lambda i: (0, i)),
                   pl.BlockSpec((2, tm, 256), lambda i: (0, i, 0)),
                   pl.BlockSpec((2, 256, tm), lambda i: (0, 0, i)),
                   own_spec(256), own_spec(256), own_spec(128), own_spec(128)],
        out_shape=[
            jax.ShapeDtypeStruct((N_TOK, 768), F32),
            jax.ShapeDtypeStruct((512, N_TOK), BF16),
            jax.ShapeDtypeStruct((2, N_TOK, 256), BF16),
            jax.ShapeDtypeStruct((2, 256, N_TOK), BF16),
            own_shape(256), own_shape(256), own_shape(128), own_shape(128),
        ],
        scratch_shapes=[pltpu.VMEM((D_MODEL, MIX_IN), BF16)],
        input_output_aliases=aliases,
        compiler_params=pltpu.CompilerParams(
            dimension_semantics=("arbitrary",), vmem_limit_bytes=VMEM_LIMIT),
        name="mixproj",
    )(x, mod, norm_pre, w_mix_in, qg, kg, tables, *aliased)


N_UNITS = 12
ATTN_SLOTS = 4
ATTN_PIECE = 1024


def _attn_kernel(l, qt_ref, k_ref, vt_ref, kc_ref, vtc_ref, lq1_ref, lk1_ref, lq2_ref, lk2_ref,
                 subln_ref, *rest):
    n = ATTN_SLOTS
    o_ref, qs_ref = rest[-3 - 3 * n], rest[-2 - 3 * n]
    s_refs = rest[-1 - 3 * n:-1 - 2 * n]
    e_refs = rest[-1 - 2 * n:-1 - n]
    il_refs = rest[-1 - n:-1]
    oacc_ref = rest[-1]
    own = k_ref.shape[1]
    half = own // 2

    for u in range(N_UNITS):
        qs_ref[u] = _unit_queries(qt_ref, u)

    def scores(u):
        plane = 0 if u < 8 else 1
        s_ref = s_refs[u % n]
        for lo in range(0, own, ATTN_PIECE):
            s_ref[lo:lo + ATTN_PIECE, :] = _dot(k_ref[plane, lo:lo + ATTN_PIECE, :], qs_ref[u])
        s_ref[own:, :] = _dot(kc_ref[plane], qs_ref[u])

    def softmax(u):
        s = s_refs[u % n][...]
        e = jnp.exp2(s - jnp.max(s, axis=0, keepdims=True))
        il_refs[u % n][...] = 1.0 / jnp.sum(e, axis=0, keepdims=True)
        e_refs[u % n][...] = e.astype(BF16)

    def values(u):
        plane = 0 if u < 8 else 1
        vrow = DV_DIFF * (u // 2) if u < 8 else HD_GQA * ((u - 8) // 2)
        e_ref = e_refs[u % n]
        o = (_dot(vt_ref[plane, vrow:vrow + DV_DIFF, 0:half], e_ref[0:half, :])
             + _dot(vt_ref[plane, vrow:vrow + DV_DIFF, half:], e_ref[half:own, :])
             + _dot(vtc_ref[plane, vrow:vrow + DV_DIFF, :], e_ref[own:, :]))
        oacc_ref[u] = o * il_refs[u % n][...]

    lag = ATTN_SLOTS // 2
    for t in range(N_UNITS + 2 * lag):
        if t < N_UNITS:
            scores(t)
        if lag <= t < N_UNITS + lag:
            softmax(t - lag)
        if t >= 2 * lag:
            values(t - 2 * lag)

    _attn_finish(l, [oacc_ref[u] for u in range(N_UNITS)],
                 lq1_ref, lk1_ref, lq2_ref, lk2_ref, subln_ref, o_ref)


def _unit_queries(qt_ref, u, cols=slice(None)):
    if u < 8:
        feat = lax.broadcasted_iota(jnp.int32, (256, 1), 0)
        return jnp.where(feat // DK_DIFF == u, qt_ref[0:256, cols], jnp.zeros((), BF16))
    hd = u - 8
    g = hd // 2
    rows = qt_ref[256 + HD_GQA * hd:256 + HD_GQA * (hd + 1), cols]
    tq = rows.shape[1]
    pieces = ([jnp.zeros((HD_GQA * g, tq), BF16)] if g else []) + [rows]
    pieces.append(jnp.zeros((256 - HD_GQA * (g + 1), tq), BF16))
    return jnp.concatenate(pieces, axis=0)


def _attn_finish(l, outs, lq1_ref, lk1_ref, lq2_ref, lk2_ref, subln_ref, o_ref, rows=slice(None)):
    lam_init = 0.8 - 0.6 * math.exp(-0.3 * l)
    row = slice(l, l + 1)
    lam = (jnp.exp(jnp.sum(lq1_ref[row, :] * lk1_ref[row, :], axis=-1, keepdims=True))
           - jnp.exp(jnp.sum(lq2_ref[row, :] * lk2_ref[row, :], axis=-1, keepdims=True))
           + lam_init)
    heads = []
    for hd in range(4):
        o = outs[2 * hd] - lam * outs[2 * hd + 1]
        nrm = lax.rsqrt(jnp.mean(o * o, axis=0, keepdims=True) + EPS)
        heads.append(o * nrm * (subln_ref[:, l:l + 1] * (1.0 - lam_init)))
    heads += outs[8:12]
    o_ref[rows, :] = jnp.concatenate(heads, axis=0).T.astype(BF16)


def _attn_ctx_kernel(l, qt_ref, k_ref, vt_ref, lq1_ref, lk1_ref, lq2_ref, lk2_ref, subln_ref,
                     o_ref):
    for q in range(qt_ref.shape[1] // SEQ):
        tok = slice(q * SEQ, (q + 1) * SEQ)
        w_diff = jnp.concatenate([_unit_queries(qt_ref, u, tok) for u in range(8)], axis=1)
        w_gqa = jnp.concatenate([_unit_queries(qt_ref, u, tok) for u in range(8, N_UNITS)], axis=1)
        s = jnp.concatenate([_dot(k_ref[0, tok, :], w_diff), _dot(k_ref[1, tok, :], w_gqa)],
                            axis=1)
        e = jnp.exp2(s - jnp.max(s, axis=0, keepdims=True))
        il = 1.0 / jnp.sum(e, axis=0, keepdims=True)
        e = e.astype(BF16)
        outs = []
        for u in range(N_UNITS):
            plane = 0 if u < 8 else 1
            vrow = DV_DIFF * (u // 2) if u < 8 else HD_GQA * ((u - 8) // 2)
            cols = slice(u * SEQ, (u + 1) * SEQ)
            outs.append(_dot(vt_ref[plane, vrow:vrow + DV_DIFF, tok], e[:, cols]) * il[:, cols])
        _attn_finish(l, outs, lq1_ref, lk1_ref, lq2_ref, lk2_ref, subln_ref, o_ref, tok)


def _attn_param_specs():
    vec = _const_spec((DEPTH, DK_DIFF))
    return [vec, vec, vec, vec, _const_spec((DV_DIFF, DEPTH))]


def _attn_scratch(tq, s_len):
    n = ATTN_SLOTS
    return ([pltpu.VMEM((N_UNITS, 256, tq), BF16)]
            + [pltpu.VMEM((s_len, tq), F32)] * n
            + [pltpu.VMEM((s_len, tq), BF16)] * n
            + [pltpu.VMEM((1, tq), F32)] * n
            + [pltpu.VMEM((N_UNITS, DV_DIFF, tq), F32)])


def _attn_ctx(qt, k_ctx, vt_ctx, aparams, l):
    t = CTX_SEQS_PER_STEP * SEQ
    return pl.pallas_call(
        functools.partial(_attn_ctx_kernel, l),
        grid=(N_CTX // t,),
        in_specs=[pl.BlockSpec((512, t), lambda b: (0, b)),
                  pl.BlockSpec((2, t, 256), lambda b: (0, b, 0)),
                  pl.BlockSpec((2, 256, t), lambda b: (0, 0, b))] + _attn_param_specs(),
        out_specs=pl.BlockSpec((t, 512), lambda b: (b, 0)),
        out_shape=jax.ShapeDtypeStruct((N_TOK, 512), BF16),
        compiler_params=pltpu.CompilerParams(
            dimension_semantics=("arbitrary",), vmem_limit_bytes=VMEM_LIMIT),
        name="attn_ctx",
    )(qt, k_ctx, vt_ctx, *aparams)


def _attn_lat(qt, k_all, vt_all, cache_k, cache_vt, att, aparams, l):
    tq = TQ_LAT
    nq = DEC_SEQ // tq
    tile = lambda b, i: N_CTX // tq + b * nq + i
    own = lambda b: N_CTX // DEC_SEQ + b
    return pl.pallas_call(
        functools.partial(_attn_kernel, l),
        grid=(DEC_BATCH, nq),
        in_specs=[pl.BlockSpec((512, tq), lambda b, i: (0, tile(b, i))),
                  pl.BlockSpec((2, DEC_SEQ, 256), lambda b, i: (0, own(b), 0)),
                  pl.BlockSpec((2, 256, DEC_SEQ), lambda b, i: (0, 0, own(b))),
                  pl.BlockSpec((None, None, 2, PAST_LEN, 256), lambda b, i: (b, l, 0, 0, 0)),
                  pl.BlockSpec((None, None, 2, 256, PAST_LEN), lambda b, i: (b, l, 0, 0, 0))]
        + _attn_param_specs() + [pl.BlockSpec(memory_space=pl.ANY)],
        out_specs=pl.BlockSpec((tq, 512), lambda b, i: (tile(b, i), 0)),
        out_shape=jax.ShapeDtypeStruct((N_TOK, 512), BF16),
        scratch_shapes=_attn_scratch(tq, S_LAT),
        input_output_aliases={10: 0},
        compiler_params=pltpu.CompilerParams(
            dimension_semantics=("arbitrary", "arbitrary"), vmem_limit_bytes=VMEM_LIMIT),
        name="attn_lat",
    )(qt, k_all, vt_all, cache_k, cache_vt, *aparams, att)


def _conv_kernel(l, *refs):
    wo_ref, wbf_ref = refs[-4], refs[-1]
    is_ctx = pl.program_id(0) < N_CTX // T_CONV

    @pl.when((pl.program_id(0) == 0) & (pl.program_id(1) == 0))
    def _():
        wbf_ref[...] = wo_ref[...].astype(BF16)

    @pl.when(is_ctx)
    def _():
        _conv_tile(l, SEQ, *refs)

    @pl.when(jnp.logical_not(is_ctx))
    def _():
        _conv_tile(l, DEC_SEQ, *refs)


def _conv_tile(l, seq_len, cin_ref, att_ref, x_ref, mod_ref, gpost_ref, wsc_ref, wdw_ref, bdw_ref,
               lng_ref, lnb_ref, pw_ref, wo_ref, o_ref, pad_ref, wbf_ref):
    del wo_ref
    t_tile = cin_ref.shape[0]
    t_sub = x_ref.shape[0]
    n_seq = t_tile // seq_len
    stride = seq_len + CONV_HALO
    chunks_per_seq = seq_len // R_CONV
    rows = R_CONV
    j = pl.program_id(1)

    @pl.when(j == 0)
    def _():
        for q in range(n_seq + 1):
            pad_ref[q * stride:q * stride + CONV_HALO, :] = jnp.zeros((CONV_HALO, 512), F32)
        for q in range(n_seq):
            pad_ref[CONV_HALO + q * stride:CONV_HALO + q * stride + seq_len, :] = (
                cin_ref[q * seq_len:(q + 1) * seq_len, 256:768])

    pw = pw_ref[...].astype(BF16)
    bdw = bdw_ref[l:l + 1, :]
    lng = lng_ref[l:l + 1, :]
    lnb = lnb_ref[l:l + 1, :]
    gate = _mod_row(mod_ref, 5, _cond_of_tile(pl.program_id(0), T_CONV))
    gpost = gpost_ref[pl.ds(1, 1), :]

    def body(c_local, carry):
        c = j * (t_sub // rows) + c_local
        q = c // chunks_per_seq
        r = c % chunks_per_seq
        base = pl.multiple_of(q * stride + r * rows, 8)
        row0 = pl.multiple_of(c * rows, 8)
        sub0 = pl.multiple_of(c_local * rows, 8)
        win = pad_ref[pl.ds(base, rows + 2 * CONV_HALO), :]
        win_s = win[:, 0:256]
        win_u = win[:, 256:512]

        s_acc = None
        for k in range(SCONV_K):
            off = CONV_HALO + k - SCONV_K // 2
            term = wsc_ref[pl.ds(k, 1), :] * win_s[off:off + rows]
            s_acc = term if s_acc is None else s_acc + term
        a_out = cin_ref[pl.ds(row0, rows), 0:256] * s_acc

        acc = None
        for res in range(8):
            part = None
            for a in range(-2, 2):
                k = 8 * a + res + CCM_K // 2
                if 0 <= k < CCM_K:
                    lo = CONV_HALO + 8 * a
                    term = wdw_ref[pl.ds(k, 1), :] * win_u[lo:lo + rows + 8]
                    part = term if part is None else part + term
            part = part[res:res + rows]
            acc = part if acc is None else acc + part
        u = acc + bdw
        mu = jnp.mean(u, axis=-1, keepdims=True)
        var = jnp.mean(jnp.square(u - mu), axis=-1, keepdims=True)
        y = (u - mu) * lax.rsqrt(var + EPS) * lng + lnb
        y = y * jax.nn.sigmoid(y)
        c_out = _dot(y.astype(BF16), pw)

        att = att_ref[pl.ds(sub0, rows), :]
        cat = jnp.concatenate([a_out.astype(BF16), att[:, 0:256], c_out.astype(BF16),
                               att[:, 256:512]], axis=1)
        ym = _dot(cat, wbf_ref[...])
        o_ref[pl.ds(sub0, rows), :] = x_ref[pl.ds(sub0, rows), :] + gate * _rms(ym, gpost)
        return carry

    lax.fori_loop(0, t_sub // rows, body, 0, unroll=4)


def _conv_mix_out(cin, att, x, mod, norm_post, w_mix_out, cparams, l):
    t, ts = T_CONV, T_CONV_SUB
    per = t // ts
    vec = _const_spec((DEPTH, 256))
    sub = lambda w: pl.BlockSpec((ts, w), lambda i, j: (i * per + j, 0))
    pad_rows = max((t // s) * (s + CONV_HALO) + CONV_HALO for s in (SEQ, DEC_SEQ))
    return pl.pallas_call(
        functools.partial(_conv_kernel, l),
        grid=(N_TOK // t, per),
        in_specs=[pl.BlockSpec((t, 768), lambda i, j: (i, 0)), sub(512), sub(D_MODEL),
                  _layer_spec(_MOD_SHAPE, l), _layer_spec((3, D_MODEL), l),
                  _layer_spec((SCONV_K, 256), l), _layer_spec((CCM_K, 256), l), vec, vec, vec,
                  _layer_spec((256, 256), l),
                  _layer_spec((D_MODEL, D_MODEL), l, pipeline_mode=pl.Buffered(1))],
        out_specs=sub(D_MODEL),
        out_shape=jax.ShapeDtypeStruct((N_TOK, D_MODEL), F32),
        scratch_shapes=[pltpu.VMEM((pad_rows, 512), F32), pltpu.VMEM((D_MODEL, D_MODEL), BF16)],
        compiler_params=pltpu.CompilerParams(
            dimension_semantics=("arbitrary", "arbitrary"), vmem_limit_bytes=VMEM_LIMIT),
        name="conv",
    )(cin, att, x, mod, norm_post, *cparams, w_mix_out)


def _cache_planes(cache_diff_k, cache_diff_v, cache_gqa_k, cache_gqa_v):
    lead = (DEC_BATCH, DEPTH, PAST_LEN)
    pad128 = lambda a: jnp.pad(a.reshape(lead + (128,)), ((0, 0), (0, 0), (0, 0), (0, 128)))
    keys = jnp.stack([cache_diff_k.reshape(lead + (256,)), pad128(cache_gqa_k)], axis=2)
    vals = jnp.stack([cache_diff_v.reshape(lead + (256,)), pad128(cache_gqa_v)], axis=2)
    return keys.astype(BF16), jnp.swapaxes(vals, 3, 4).astype(BF16)


def kernel(x_prompt, x_sample, cache_diff_k, cache_diff_v, cache_gqa_k, cache_gqa_v, c, c_ctx, w_ada, b_ada, norm_pre, norm_post, ffn1_gate, ffn1_up, ffn1_down, ffn2_gate, ffn2_up, ffn2_down, w_mix_in, w_mix_out, sconv_w, diff_lq1, diff_lk1, diff_lq2, diff_lk2, diff_subln, ccm_dw_w, ccm_dw_b, ccm_ln_g, ccm_ln_b, ccm_pw, gqa_qnorm, gqa_knorm):
    conds = jnp.concatenate(
        [c_ctx[None, :], c, jnp.zeros((N_COND - 1 - DEC_BATCH, D_MODEL), F32)], axis=0)
    mod = _ada(conds, w_ada, b_ada)
    tables = jnp.asarray(_rope_tables())
    qg = jnp.tile(gqa_qnorm, (1, 4))
    kg = jnp.tile(gqa_knorm, (1, 2))
    aparams = (diff_lq1, diff_lk1, diff_lq2, diff_lk2, diff_subln.T)
    cparams = (sconv_w, ccm_dw_w, ccm_dw_b, ccm_ln_g, ccm_ln_b, ccm_pw)
    cache_k, cache_vt = _cache_planes(cache_diff_k, cache_diff_v, cache_gqa_k, cache_gqa_v)

    xs = (x_prompt.reshape(N_CTX, D_MODEL), x_sample.reshape(N_LAT, D_MODEL))
    own = None
    for l in range(DEPTH):
        (x,) = _ffn(xs, mod, norm_pre, norm_post, ffn1_gate, ffn1_up, ffn1_down, l, 0)

        cin, qt, k_all, vt_all, *own = _mixproj(
            x, mod, norm_pre, w_mix_in, qg, kg, tables, own, l)

        att = _attn_ctx(qt, k_all, vt_all, aparams, l)
        att = _attn_lat(qt, k_all, vt_all, cache_k, cache_vt, att, aparams, l)

        x = _conv_mix_out(cin, att, x, mod, norm_post, w_mix_out, cparams, l)

        xs = _ffn((x,), mod, norm_pre, norm_post, ffn2_gate, ffn2_up, ffn2_down, l, 2,
                  split_out=(l == DEPTH - 1))

    y_prompt, y_sample = xs
    odk, odv, ogk, ogv = own
    return (y_prompt.reshape(BATCH, SEQ, D_MODEL),
            y_sample.reshape(DEC_BATCH, DEC_SEQ, D_MODEL),
            odk.reshape(BATCH, DEPTH, SEQ, 4, 2, DK_DIFF),
            odv.reshape(BATCH, DEPTH, SEQ, 4, DV_DIFF),
            ogk.reshape(BATCH, DEPTH, SEQ, 2, HD_GQA),
            ogv.reshape(BATCH, DEPTH, SEQ, 2, HD_GQA))
```

```python
import functools
import math

import numpy as np

import jax
import jax.numpy as jnp
from jax import lax
from jax.experimental import pallas as pl
from jax.experimental.pallas import tpu as pltpu

F32 = jnp.float32
BF16 = jnp.bfloat16

D_MODEL = 1024
BATCH = 16
SEQ = 256
DEPTH = 2
DEC_BATCH = 2
DEC_SEQ = 2048
PAST_LEN = 256
GRID_W = 64
GRID_H = DEC_SEQ // GRID_W
ROPE_THETA = 10000.0
EPS = 1e-6
N_MOD = 9
D_FF = 2816
DK_DIFF = 32
DV_DIFF = 64
HD_GQA = 64
CCM_K = 31
SCONV_K = 3
MIX_IN = 2560

N_CTX = BATCH * SEQ
N_LAT = DEC_BATCH * DEC_SEQ
N_TOK = N_CTX + N_LAT
N_COND = 8
S_LAT = DEC_SEQ + PAST_LEN

LOG2E = math.log2(math.e)

VMEM_LIMIT = 56 * 1024 * 1024

TM_FFN = 512
FFN_SUB = 256
FFN_CHUNK = 256
FFN_STAGE_SLOTS = 3
TM_MIX = 1024
MIX_SUB = 256
ADA_VECS = 3
TQ_LAT = 256
CTX_SEQS_PER_STEP = 4
T_CONV = 2048
R_CONV = 128
CONV_HALO = 16


def _cond_of_tile(i, tm):
    n_ctx_tiles = N_CTX // tm
    per_b = DEC_SEQ // tm
    return jnp.where(i < n_ctx_tiles, 0, 1 + (i - n_ctx_tiles) // per_b)


def _mod_row(mod_ref, k, cond):
    return mod_ref[k, pl.ds(cond, 1), :]


def _dot(a, b):
    return jnp.dot(a, b, preferred_element_type=F32)


def _rms(x, g):
    ms = jnp.mean(x * x, axis=-1, keepdims=True)
    return x * lax.rsqrt(ms + EPS) * g


def _group_mean(xx, width, group):
    r = lax.broadcasted_iota(jnp.int32, (width, width), 0) // group
    c = lax.broadcasted_iota(jnp.int32, (width, width), 1) // group
    gmat = jnp.where(r == c, 1.0 / group, 0.0).astype(BF16)
    hi = xx.astype(BF16)
    lo = (xx - hi.astype(F32)).astype(BF16)
    return _dot(hi, gmat) + _dot(lo, gmat)


def _const_spec(shape):
    return pl.BlockSpec(shape, lambda *_: (0,) * len(shape))


def _layer_spec(shape, l, **kw):
    return pl.BlockSpec((None,) + shape, lambda *_: (l,) + (0,) * len(shape), **kw)


_MOD_SHAPE = (N_MOD, N_COND, D_MODEL)


def _ada_kernel(c_ref, w_ref, b_ref, o_ref):
    cnd = c_ref[...]
    s = (cnd * jax.nn.sigmoid(cnd)).astype(BF16)
    for k in range(ADA_VECS):
        cols = slice(k * D_MODEL, (k + 1) * D_MODEL)
        o_ref[k] = _dot(s, w_ref[:, cols].astype(BF16)) + b_ref[k]


def _ada(conds, w_ada, b_ada):
    return pl.pallas_call(
        _ada_kernel,
        grid=(DEPTH, N_MOD // ADA_VECS),
        in_specs=[
            pl.BlockSpec((N_COND, D_MODEL), lambda l, n: (0, 0)),
            pl.BlockSpec((None, D_MODEL, ADA_VECS * D_MODEL), lambda l, n: (l, 0, n)),
            pl.BlockSpec((None, ADA_VECS, 1, D_MODEL), lambda l, n: (l, n, 0, 0)),
        ],
        out_specs=pl.BlockSpec((None, ADA_VECS, N_COND, D_MODEL), lambda l, n: (l, n, 0, 0)),
        out_shape=jax.ShapeDtypeStruct((DEPTH,) + _MOD_SHAPE, F32),
        compiler_params=pltpu.CompilerParams(
            dimension_semantics=("arbitrary", "arbitrary"), vmem_limit_bytes=VMEM_LIMIT),
        name="ada",
    )(conds, w_ada, b_ada.reshape(DEPTH, N_MOD, 1, D_MODEL))


class _WeightStager:
    def __init__(self, jobs, stage_ref, sem_ref):
        self.jobs, self.stage_ref, self.depth = jobs, stage_ref, stage_ref.shape[0]
        self.copies = [
            pltpu.make_async_copy(src, stage_ref.at[k % self.depth], sem_ref.at[k % self.depth])
            for k, (src, _) in enumerate(jobs)]

    def prime(self):
        for k in range(min(self.depth - 1, len(self.jobs))):
            self.copies[k].start()

    def finish(self, k):
        if k + self.depth - 1 < len(self.jobs):
            self.copies[k + self.depth - 1].start()
        self.copies[k].wait()
        self.jobs[k][1](self.stage_ref[k % self.depth].astype(BF16))


def _ffn_kernel(l, sub, tm, n_x, n_o, mix, *refs):
    refs = list(refs)
    x_refs = [refs.pop(0) for _ in range(n_x)]
    ac_ref, att_ref = (refs.pop(0), refs.pop(0)) if mix else (None, None)
    mod_ref, gpre_ref, gpost_ref, wg_hbm, wu_hbm, wd_hbm = [refs.pop(0) for _ in range(6)]
    wo_hbm = refs.pop(0) if mix else None
    o_refs = [refs.pop(0) for _ in range(n_o)]
    wg_ref, wu_ref, wd_ref = [refs.pop(0) for _ in range(3)]
    wo_ref = refs.pop(0) if mix else None
    i = pl.program_id(0)
    is_ctx = i < N_CTX // tm
    cond = _cond_of_tile(i, tm)
    c = FFN_CHUNK
    n_chunks = D_FF // c
    gate = _mod_row(mod_ref, 3 * sub + 2, cond)
    gpost = gpost_ref[pl.ds(sub, 1), :]

    def prep():
        if n_x == 1:
            x = x_refs[0][...]
        else:
            x = jnp.where(is_ctx, x_refs[0][...], x_refs[1][...])
        if mix:
            cat = jnp.concatenate([ac_ref[:, 0:256], att_ref[:, 0:256],
                                   ac_ref[:, 256:512], att_ref[:, 256:512]], axis=1)
            ym = _dot(cat, wo_ref[...])
            x = x + _mod_row(mod_ref, 5, cond) * _rms(ym, gpost_ref[pl.ds(1, 1), :])
        shift = _mod_row(mod_ref, 3 * sub, cond)
        scale = _mod_row(mod_ref, 3 * sub + 1, cond)
        h = (_rms(x, gpre_ref[pl.ds(sub, 1), :]) * (1.0 + scale) + shift).astype(BF16)
        return x, h

    def swiglu(hh, cols):
        g = _dot(hh, wg_ref[:, cols])
        u = _dot(hh, wu_ref[:, cols])
        a = ((g * jax.nn.sigmoid(g)) * u).astype(BF16)
        return _dot(a, wd_ref[cols, :])

    def emit(y):
        if n_o == 1:
            o_refs[0][...] = y
        else:
            @pl.when(is_ctx)
            def _():
                o_refs[0][...] = y

            @pl.when(jnp.logical_not(is_ctx))
            def _():
                o_refs[1][...] = y

    @pl.when(i == 0)
    def _():
        def col_store(dst, j):
            def store(v):
                dst[:, j * c:(j + 1) * c] = v
            return store

        def row_store(j):
            def store(v):
                wd_ref[j * c:(j + 1) * c, :] = v
            return store

        up_jobs = [(src.at[l, :, pl.ds(j * c, c)], col_store(dst, j))
                   for j in range(n_chunks) for src, dst in ((wg_hbm, wg_ref), (wu_hbm, wu_ref))]
        down_jobs = [(wd_hbm.at[l, pl.ds(j * c, c), :], row_store(j)) for j in range(n_chunks)]

        def run(stage_up, stage_down, sem_up, sem_down):
            up = _WeightStager(up_jobs, stage_up, sem_up)
            up.prime()
            if mix:
                def wo_store(j):
                    def store(v):
                        wo_ref[j * c:(j + 1) * c, :] = v
                    return store

                wo = _WeightStager(
                    [(wo_hbm.at[l, pl.ds(j * c, c), :], wo_store(j)) for j in range(D_MODEL // c)],
                    stage_down, sem_down)
                wo.prime()
                for j in range(D_MODEL // c):
                    wo.finish(j)
            down = _WeightStager(down_jobs, stage_down, sem_down)
            down.prime()
            x, h = prep()
            y = None
            for j in range(n_chunks):
                up.finish(2 * j)
                up.finish(2 * j + 1)
                down.finish(j)
                part = swiglu(h, slice(j * c, (j + 1) * c))
                y = part if y is None else y + part
            emit(x + 0.5 * (gate * _rms(y, gpost)))

        pl.run_scoped(run,
                      pltpu.VMEM((2 * FFN_STAGE_SLOTS, D_MODEL, c), F32),
                      pltpu.VMEM((FFN_STAGE_SLOTS, c, D_MODEL), F32),
                      pltpu.SemaphoreType.DMA((2 * FFN_STAGE_SLOTS,)),
                      pltpu.SemaphoreType.DMA((FFN_STAGE_SLOTS,)))

    @pl.when(i != 0)
    def _():
        x, h = prep()
        outs = []
        for s in range(tm // FFN_SUB):
            rows = slice(s * FFN_SUB, (s + 1) * FFN_SUB)
            y = swiglu(h[rows], slice(None))
            outs.append(x[rows] + 0.5 * (gate * _rms(y, gpost)))
        emit(jnp.concatenate(outs, axis=0))


def _ffn(xs, mod, norm_pre, norm_post, wg, wu, wd, l, sub, split_out=False, mix=None):
    tm = TM_FFN
    n_ctx_tiles = N_CTX // tm
    ctx_spec = pl.BlockSpec((tm, D_MODEL), lambda i: (jnp.minimum(i, n_ctx_tiles - 1), 0))
    lat_spec = pl.BlockSpec((tm, D_MODEL), lambda i: (jnp.maximum(i - n_ctx_tiles, 0), 0))
    full_spec = pl.BlockSpec((tm, D_MODEL), lambda i: (i, 0))
    x_specs = [full_spec] if len(xs) == 1 else [ctx_spec, lat_spec]
    if split_out:
        out_specs = [ctx_spec, lat_spec]
        out_shape = [jax.ShapeDtypeStruct((N_CTX, D_MODEL), F32),
                     jax.ShapeDtypeStruct((N_LAT, D_MODEL), F32)]
    else:
        out_specs = [full_spec]
        out_shape = [jax.ShapeDtypeStruct((N_TOK, D_MODEL), F32)]
    hbm = pl.BlockSpec(memory_space=pl.ANY)
    args = list(xs)
    in_specs = list(x_specs)
    scratch = [pltpu.VMEM((D_MODEL, D_FF), BF16), pltpu.VMEM((D_MODEL, D_FF), BF16),
               pltpu.VMEM((D_FF, D_MODEL), BF16)]
    if mix is not None:
        args += [mix[0], mix[1]]
        in_specs += [pl.BlockSpec((tm, 512), lambda i: (i, 0))] * 2
    args += [mod, norm_pre, norm_post, wg, wu, wd]
    in_specs += [_layer_spec(_MOD_SHAPE, l), _layer_spec((3, D_MODEL), l),
                 _layer_spec((3, D_MODEL), l), hbm, hbm, hbm]
    if mix is not None:
        args.append(mix[2])
        in_specs.append(hbm)
        scratch.append(pltpu.VMEM((D_MODEL, D_MODEL), BF16))
    return pl.pallas_call(
        functools.partial(_ffn_kernel, l, sub, tm, len(xs), len(out_specs), mix is not None),
        grid=(N_TOK // tm,),
        in_specs=in_specs,
        out_specs=out_specs,
        out_shape=out_shape,
        scratch_shapes=scratch,
        compiler_params=pltpu.CompilerParams(
            dimension_semantics=("arbitrary",), vmem_limit_bytes=VMEM_LIMIT),
        name=f"ffn{sub}",
    )(*args)


def _swap_blocks(x, blk):
    w = x.shape[-1]
    lane = lax.broadcasted_iota(jnp.int32, x.shape, x.ndim - 1)
    first = (lane % (2 * blk)) < blk
    return jnp.where(first, pltpu.roll(x, w - blk, x.ndim - 1), pltpu.roll(x, blk, x.ndim - 1))


def _rope_tables():
    out = []
    for rot_dim in (DK_DIFF, HD_GQA):
        n_freq = rot_dim // 4
        inv = ROPE_THETA ** (-np.arange(n_freq, dtype=np.float64) / n_freq)
        lane = np.arange(256) % rot_dim
        pos = lane % (rot_dim // 2)
        freq = inv[pos % n_freq]
        sign = np.where(pos < n_freq, -1.0, 1.0)
        coord = np.concatenate([np.arange(GRID_H), np.arange(GRID_W)]).astype(np.float64)
        ang = coord[:, None] * freq[None, :]
        out.append(np.cos(ang))
        out.append(np.sin(ang) * sign[None, :])
    return np.stack(out).astype(np.float32)


def _tile_rope(tab_ref, which, rot_dim, first_grid_row, n_grid_rows, identity):
    col_part = tab_ref[which, GRID_H:GRID_H + GRID_W, :]
    lane = lax.broadcasted_iota(jnp.int32, (1, 256), 1)
    on_row_axis = (lane % rot_dim) < rot_dim // 2
    tab = jnp.concatenate(
        [jnp.where(on_row_axis, tab_ref[which, pl.ds(first_grid_row + r, 1), :], col_part)
         for r in range(n_grid_rows)], axis=0)
    return jnp.where(identity, 1.0 - (which % 2), tab)


def _mixproj_kernel(l, has_prev, x_ref, mod_ref, gpre_ref, w_ref, qg_ref, kg_ref, tab_ref, *rest):
    n_alias = 4 if has_prev else 0
    (cin_ref, qt_ref, k_ref, vt_ref, odk_ref, odv_ref, ogk_ref, ogv_ref, wbf_ref) = rest[n_alias:]
    tm = TM_MIX
    i = pl.program_id(0)
    n_ctx_tiles = N_CTX // tm
    is_ctx = i < n_ctx_tiles
    cond = _cond_of_tile(i, tm)

    @pl.when(i == 0)
    def _():
        wbf_ref[...] = w_ref[...].astype(BF16)

    shift = _mod_row(mod_ref, 3, cond)
    scale = _mod_row(mod_ref, 4, cond)
    gpre = gpre_ref[pl.ds(1, 1), :]
    sub = MIX_SUB
    grid_rows = sub // GRID_W
    seqs = sub // SEQ
    tile_row = (jnp.maximum(i - n_ctx_tiles, 0) % (DEC_SEQ // tm)) * (tm // GRID_W)
    own = []

    for s in range(tm // sub):
        rows = slice(s * sub, (s + 1) * sub)
        h = (_rms(x_ref[rows, :], gpre) * (1.0 + scale) + shift).astype(BF16)

        a = _dot(h, wbf_ref[:, 0:768])
        cin_ref[rows, 0:256] = a[:, 0:256]
        cin_ref[rows, 256:512] = a[:, 256:512] * a[:, 512:768]
        c = _dot(h, wbf_ref[:, 1536:2048])
        cin_ref[rows, 512:768] = c[:, 0:256] * jax.nn.sigmoid(c[:, 256:512])

        first_row = tile_row + s * grid_rows
        cd = _tile_rope(tab_ref, 0, DK_DIFF, first_row, grid_rows, is_ctx)
        sd = _tile_rope(tab_ref, 1, DK_DIFF, first_row, grid_rows, is_ctx)
        cg = _tile_rope(tab_ref, 2, HD_GQA, first_row, grid_rows, is_ctx)
        sg = _tile_rope(tab_ref, 3, HD_GQA, first_row, grid_rows, is_ctx)

        b = _dot(h, wbf_ref[:, 768:1536])
        bq, bk, bv = b[:, 0:256], b[:, 256:512], b[:, 512:768]
        bq_r = (bq * cd + _swap_blocks(bq, DK_DIFF // 4) * sd) * (DK_DIFF ** -0.5 * LOG2E)
        bk_r = bk * cd + _swap_blocks(bk, DK_DIFF // 4) * sd
        qt_ref[0:256, rows] = bq_r.T.astype(BF16)

        d = _dot(h, wbf_ref[:, 2048:2560])
        dq, dk, dv = d[:, 0:256], d[:, 256:384], d[:, 384:512]
        dq_n = dq * lax.rsqrt(_group_mean(dq * dq, 256, HD_GQA) + EPS) * qg_ref[l:l + 1, :]
        dk_n = dk * lax.rsqrt(_group_mean(dk * dk, 128, HD_GQA) + EPS) * kg_ref[l:l + 1, :]
        dq_r = (dq_n * cg + _swap_blocks(dq_n, HD_GQA // 4) * sg) * (HD_GQA ** -0.5 * LOG2E)
        dk_r = dk_n * cg[:, 0:128] + _swap_blocks(dk_n, HD_GQA // 4) * sg[:, 0:128]
        qt_ref[256:512, rows] = dq_r.T.astype(BF16)

        k_ref[0, rows, :] = bk_r.astype(BF16)
        k_ref[1, rows, :] = jnp.concatenate([dk_r, jnp.zeros_like(dk_r)], axis=1).astype(BF16)
        vt_ref[0, :, rows] = bv.T.astype(BF16)
        vt_ref[1, :, rows] = jnp.concatenate([dv, jnp.zeros_like(dv)], axis=1).T.astype(BF16)
        own.append((bk, bv, dk_n, dv))

    @pl.when(is_ctx)
    def _():
        for s, (bk, bv, dk_n, dv) in enumerate(own):
            seq = slice(s * seqs, (s + 1) * seqs)
            odk_ref[seq] = bk.reshape(seqs, SEQ, 256)
            odv_ref[seq] = bv.reshape(seqs, SEQ, 256)
            ogk_ref[seq] = dk_n.reshape(seqs, SEQ, 128)
            ogv_ref[seq] = dv.reshape(seqs, SEQ, 128)


def _mixproj(x, mod, norm_pre, w_mix_in, qg, kg, tables, prev_own, l):
    tm = TM_MIX
    n_ctx_tiles = N_CTX // tm
    seqs = tm // SEQ

    ctx_row = lambda i: jnp.minimum(i, n_ctx_tiles - 1)
    row = lambda w: pl.BlockSpec((tm, w), lambda i: (i, 0))
    own_spec = lambda w: pl.BlockSpec((seqs, None, SEQ, w), lambda i: (ctx_row(i), l, 0, 0))
    own_shape = lambda w: jax.ShapeDtypeStruct((BATCH, DEPTH, SEQ, w), F32)
    any_spec = pl.BlockSpec(memory_space=pl.ANY)
    aliased = list(prev_own) if prev_own is not None else []
    n_in = 7
    aliases = {n_in + k: 4 + k for k in range(len(aliased))}
    return pl.pallas_call(
        functools.partial(_mixproj_kernel, l, prev_own is not None),
        grid=(N_TOK // tm,),
        in_specs=[
            row(D_MODEL),
            _layer_spec(_MOD_SHAPE, l),
            _layer_spec((3, D_MODEL), l),
            _layer_spec((D_MODEL, MIX_IN), l, pipeline_mode=pl.Buffered(1)),
            _const_spec((DEPTH, 256)),
            _const_spec((DEPTH, 128)),
            _const_spec(tables.shape),
        ] + [any_spec] * len(aliased),
        out_specs=[row(768),
                   pl.BlockSpec((512, tm), lambda i: (0, i)),
                   pl.BlockSpec((2, tm, 256), lambda i: (0, i, 0)),
                   pl.BlockSpec((2, 256, tm), lambda i: (0, 0, i)),
                   own_spec(256), own_spec(256), own_spec(128), own_spec(128)],
        out_shape=[
            jax.ShapeDtypeStruct((N_TOK, 768), F32),
            jax.ShapeDtypeStruct((512, N_TOK), BF16),
            jax.ShapeDtypeStruct((2, N_TOK, 256), BF16),
            jax.ShapeDtypeStruct((2, 256, N_TOK), BF16),
            own_shape(256), own_shape(256), own_shape(128), own_shape(128),
        ],
        scratch_shapes=[pltpu.VMEM((D_MODEL, MIX_IN), BF16)],
        input_output_aliases=aliases,
        compiler_params=pltpu.CompilerParams(
            dimension_semantics=("arbitrary",), vmem_limit_bytes=VMEM_LIMIT),
        name="mixproj",
    )(x, mod, norm_pre, w_mix_in, qg, kg, tables, *aliased)


N_UNITS = 12
ATTN_SLOTS = 4
ATTN_PIECE = 1024


def _attn_kernel(l, qt_ref, k_ref, vt_ref, kc_ref, vtc_ref, lq1_ref, lk1_ref, lq2_ref, lk2_ref,
                 subln_ref, *rest):
    n = ATTN_SLOTS
    o_ref, qs_ref = rest[-3 - 3 * n], rest[-2 - 3 * n]
    s_refs = rest[-1 - 3 * n:-1 - 2 * n]
    e_refs = rest[-1 - 2 * n:-1 - n]
    il_refs = rest[-1 - n:-1]
    oacc_ref = rest[-1]
    own = k_ref.shape[1]
    half = own // 2

    for u in range(N_UNITS):
        qs_ref[u] = _unit_queries(qt_ref, u)

    def scores(u):
        plane = 0 if u < 8 else 1
        s_ref = s_refs[u % n]
        for lo in range(0, own, ATTN_PIECE):
            s_ref[lo:lo + ATTN_PIECE, :] = _dot(k_ref[plane, lo:lo + ATTN_PIECE, :], qs_ref[u])
        s_ref[own:, :] = _dot(kc_ref[plane], qs_ref[u])

    def softmax(u):
        s = s_refs[u % n][...]
        e = jnp.exp2(s - jnp.max(s, axis=0, keepdims=True))
        il_refs[u % n][...] = 1.0 / jnp.sum(e, axis=0, keepdims=True)
        e_refs[u % n][...] = e.astype(BF16)

    def values(u):
        plane = 0 if u < 8 else 1
        vrow = DV_DIFF * (u // 2) if u < 8 else HD_GQA * ((u - 8) // 2)
        e_ref = e_refs[u % n]
        o = (_dot(vt_ref[plane, vrow:vrow + DV_DIFF, 0:half], e_ref[0:half, :])
             + _dot(vt_ref[plane, vrow:vrow + DV_DIFF, half:], e_ref[half:own, :])
             + _dot(vtc_ref[plane, vrow:vrow + DV_DIFF, :], e_ref[own:, :]))
        oacc_ref[u] = o * il_refs[u % n][...]

    lag = ATTN_SLOTS // 2
    for t in range(N_UNITS + 2 * lag):
        if t < N_UNITS:
            scores(t)
        if lag <= t < N_UNITS + lag:
            softmax(t - lag)
        if t >= 2 * lag:
            values(t - 2 * lag)

    _attn_finish(l, [oacc_ref[u] for u in range(N_UNITS)],
                 lq1_ref, lk1_ref, lq2_ref, lk2_ref, subln_ref, o_ref)


def _unit_queries(qt_ref, u, cols=slice(None)):
    if u < 8:
        feat = lax.broadcasted_iota(jnp.int32, (256, 1), 0)
        return jnp.where(feat // DK_DIFF == u, qt_ref[0:256, cols], jnp.zeros((), BF16))
    hd = u - 8
    g = hd // 2
    rows = qt_ref[256 + HD_GQA * hd:256 + HD_GQA * (hd + 1), cols]
    tq = rows.shape[1]
    pieces = ([jnp.zeros((HD_GQA * g, tq), BF16)] if g else []) + [rows]
    pieces.append(jnp.zeros((256 - HD_GQA * (g + 1), tq), BF16))
    return jnp.concatenate(pieces, axis=0)


def _attn_finish(l, outs, lq1_ref, lk1_ref, lq2_ref, lk2_ref, subln_ref, o_ref, rows=slice(None)):
    lam_init = 0.8 - 0.6 * math.exp(-0.3 * l)
    row = slice(l, l + 1)
    lam = (jnp.exp(jnp.sum(lq1_ref[row, :] * lk1_ref[row, :], axis=-1, keepdims=True))
           - jnp.exp(jnp.sum(lq2_ref[row, :] * lk2_ref[row, :], axis=-1, keepdims=True))
           + lam_init)
    heads = []
    for hd in range(4):
        o = outs[2 * hd] - lam * outs[2 * hd + 1]
        nrm = lax.rsqrt(jnp.mean(o * o, axis=0, keepdims=True) + EPS)
        heads.append(o * nrm * (subln_ref[:, l:l + 1] * (1.0 - lam_init)))
    heads += outs[8:12]
    o_ref[rows, :] = jnp.concatenate(heads, axis=0).T.astype(BF16)


def _attn_ctx_kernel(l, qt_ref, k_ref, vt_ref, lq1_ref, lk1_ref, lq2_ref, lk2_ref, subln_ref,
                     o_ref):
    for q in range(qt_ref.shape[1] // SEQ):
        tok = slice(q * SEQ, (q + 1) * SEQ)
        w_diff = jnp.concatenate([_unit_queries(qt_ref, u, tok) for u in range(8)], axis=1)
        w_gqa = jnp.concatenate([_unit_queries(qt_ref, u, tok) for u in range(8, N_UNITS)], axis=1)
        s = jnp.concatenate([_dot(k_ref[0, tok, :], w_diff), _dot(k_ref[1, tok, :], w_gqa)],
                            axis=1)
        e = jnp.exp2(s - jnp.max(s, axis=0, keepdims=True))
        il = 1.0 / jnp.sum(e, axis=0, keepdims=True)
        e = e.astype(BF16)
        outs = []
        for u in range(N_UNITS):
            plane = 0 if u < 8 else 1
            vrow = DV_DIFF * (u // 2) if u < 8 else HD_GQA * ((u - 8) // 2)
            cols = slice(u * SEQ, (u + 1) * SEQ)
            outs.append(_dot(vt_ref[plane, vrow:vrow + DV_DIFF, tok], e[:, cols]) * il[:, cols])
        _attn_finish(l, outs, lq1_ref, lk1_ref, lq2_ref, lk2_ref, subln_ref, o_ref, tok)


def _attn_param_specs():
    vec = _const_spec((DEPTH, DK_DIFF))
    return [vec, vec, vec, vec, _const_spec((DV_DIFF, DEPTH))]


def _attn_scratch(tq, s_len):
    n = ATTN_SLOTS
    return ([pltpu.VMEM((N_UNITS, 256, tq), BF16)]
            + [pltpu.VMEM((s_len, tq), F32)] * n
            + [pltpu.VMEM((s_len, tq), BF16)] * n
            + [pltpu.VMEM((1, tq), F32)] * n
            + [pltpu.VMEM((N_UNITS, DV_DIFF, tq), F32)])


def _attn_ctx(qt, k_ctx, vt_ctx, aparams, l):
    t = CTX_SEQS_PER_STEP * SEQ
    return pl.pallas_call(
        functools.partial(_attn_ctx_kernel, l),
        grid=(N_CTX // t,),
        in_specs=[pl.BlockSpec((512, t), lambda b: (0, b)),
                  pl.BlockSpec((2, t, 256), lambda b: (0, b, 0)),
                  pl.BlockSpec((2, 256, t), lambda b: (0, 0, b))] + _attn_param_specs(),
        out_specs=pl.BlockSpec((t, 512), lambda b: (b, 0)),
        out_shape=jax.ShapeDtypeStruct((N_TOK, 512), BF16),
        compiler_params=pltpu.CompilerParams(
            dimension_semantics=("arbitrary",), vmem_limit_bytes=VMEM_LIMIT),
        name="attn_ctx",
    )(qt, k_ctx, vt_ctx, *aparams)


def _attn_lat(qt, k_all, vt_all, cache_k, cache_vt, att, aparams, l):
    tq = TQ_LAT
    nq = DEC_SEQ // tq
    tile = lambda b, i: N_CTX // tq + b * nq + i
    own = lambda b: N_CTX // DEC_SEQ + b
    return pl.pallas_call(
        functools.partial(_attn_kernel, l),
        grid=(DEC_BATCH, nq),
        in_specs=[pl.BlockSpec((512, tq), lambda b, i: (0, tile(b, i))),
                  pl.BlockSpec((2, DEC_SEQ, 256), lambda b, i: (0, own(b), 0)),
                  pl.BlockSpec((2, 256, DEC_SEQ), lambda b, i: (0, 0, own(b))),
                  pl.BlockSpec((None, None, 2, PAST_LEN, 256), lambda b, i: (b, l, 0, 0, 0)),
                  pl.BlockSpec((None, None, 2, 256, PAST_LEN), lambda b, i: (b, l, 0, 0, 0))]
        + _attn_param_specs() + [pl.BlockSpec(memory_space=pl.ANY)],
        out_specs=pl.BlockSpec((tq, 512), lambda b, i: (tile(b, i), 0)),
        out_shape=jax.ShapeDtypeStruct((N_TOK, 512), BF16),
        scratch_shapes=_attn_scratch(tq, S_LAT),
        input_output_aliases={10: 0},
        compiler_params=pltpu.CompilerParams(
            dimension_semantics=("arbitrary", "arbitrary"), vmem_limit_bytes=VMEM_LIMIT),
        name="attn_lat",
    )(qt, k_all, vt_all, cache_k, cache_vt, *aparams, att)


def _conv_kernel(l, *refs):
    is_ctx = pl.program_id(0) < N_CTX // T_CONV

    @pl.when(is_ctx)
    def _():
        _conv_tile(l, SEQ, *refs)

    @pl.when(jnp.logical_not(is_ctx))
    def _():
        _conv_tile(l, DEC_SEQ, *refs)


def _conv_tile(l, seq_len, cin_ref, wsc_ref, wdw_ref, bdw_ref, lng_ref, lnb_ref, pw_ref, o_ref,
               pad_ref):
    t_tile = cin_ref.shape[0]
    n_seq = t_tile // seq_len
    stride = seq_len + CONV_HALO
    chunks_per_seq = seq_len // R_CONV
    rows = R_CONV

    for q in range(n_seq + 1):
        pad_ref[q * stride:q * stride + CONV_HALO, :] = jnp.zeros((CONV_HALO, 512), F32)
    for q in range(n_seq):
        pad_ref[CONV_HALO + q * stride:CONV_HALO + q * stride + seq_len, :] = (
            cin_ref[q * seq_len:(q + 1) * seq_len, 256:768])

    pw = pw_ref[...].astype(BF16)
    bdw = bdw_ref[l:l + 1, :]
    lng = lng_ref[l:l + 1, :]
    lnb = lnb_ref[l:l + 1, :]

    def body(c, carry):
        q = c // chunks_per_seq
        r = c % chunks_per_seq
        base = pl.multiple_of(q * stride + r * rows, 8)
        row0 = pl.multiple_of(c * rows, 8)
        win = pad_ref[pl.ds(base, rows + 2 * CONV_HALO), :]
        win_s = win[:, 0:256]
        win_u = win[:, 256:512]

        s_acc = None
        for k in range(SCONV_K):
            off = CONV_HALO + k - SCONV_K // 2
            term = wsc_ref[pl.ds(k, 1), :] * win_s[off:off + rows]
            s_acc = term if s_acc is None else s_acc + term
        a_out = cin_ref[pl.ds(row0, rows), 0:256] * s_acc

        acc = None
        for res in range(8):
            part = None
            for a in range(-2, 2):
                k = 8 * a + res + CCM_K // 2
                if 0 <= k < CCM_K:
                    lo = CONV_HALO + 8 * a
                    term = wdw_ref[pl.ds(k, 1), :] * win_u[lo:lo + rows + 8]
                    part = term if part is None else part + term
            part = part[res:res + rows]
            acc = part if acc is None else acc + part
        u = acc + bdw
        mu = jnp.mean(u, axis=-1, keepdims=True)
        var = jnp.mean(jnp.square(u - mu), axis=-1, keepdims=True)
        y = (u - mu) * lax.rsqrt(var + EPS) * lng + lnb
        y = y * jax.nn.sigmoid(y)
        c_out = _dot(y.astype(BF16), pw)
        o_ref[pl.ds(row0, rows), 0:256] = a_out.astype(BF16)
        o_ref[pl.ds(row0, rows), 256:512] = c_out.astype(BF16)
        return carry

    lax.fori_loop(0, t_tile // rows, body, 0, unroll=2)


def _conv(cin, cparams, l):
    t = T_CONV
    vec = _const_spec((DEPTH, 256))
    pad_rows = max((t // s) * (s + CONV_HALO) + CONV_HALO for s in (SEQ, DEC_SEQ))
    return pl.pallas_call(
        functools.partial(_conv_kernel, l),
        grid=(N_TOK // t,),
        in_specs=[pl.BlockSpec((t, 768), lambda i: (i, 0)),
                  _layer_spec((SCONV_K, 256), l), _layer_spec((CCM_K, 256), l), vec, vec, vec,
                  _layer_spec((256, 256), l)],
        out_specs=pl.BlockSpec((t, 512), lambda i: (i, 0)),
        out_shape=jax.ShapeDtypeStruct((N_TOK, 512), BF16),
        scratch_shapes=[pltpu.VMEM((pad_rows, 512), F32)],
        compiler_params=pltpu.CompilerParams(
            dimension_semantics=("arbitrary",), vmem_limit_bytes=VMEM_LIMIT),
        name="conv",
    )(cin, *cparams)


def _cache_planes(cache_diff_k, cache_diff_v, cache_gqa_k, cache_gqa_v):
    lead = (DEC_BATCH, DEPTH, PAST_LEN)
    pad128 = lambda a: jnp.pad(a.reshape(lead + (128,)), ((0, 0), (0, 0), (0, 0), (0, 128)))
    keys = jnp.stack([cache_diff_k.reshape(lead + (256,)), pad128(cache_gqa_k)], axis=2)
    vals = jnp.stack([cache_diff_v.reshape(lead + (256,)), pad128(cache_gqa_v)], axis=2)
    return keys.astype(BF16), jnp.swapaxes(vals, 3, 4).astype(BF16)


def kernel(x_prompt, x_sample, cache_diff_k, cache_diff_v, cache_gqa_k, cache_gqa_v, c, c_ctx, w_ada, b_ada, norm_pre, norm_post, ffn1_gate, ffn1_up, ffn1_down, ffn2_gate, ffn2_up, ffn2_down, w_mix_in, w_mix_out, sconv_w, diff_lq1, diff_lk1, diff_lq2, diff_lk2, diff_subln, ccm_dw_w, ccm_dw_b, ccm_ln_g, ccm_ln_b, ccm_pw, gqa_qnorm, gqa_knorm):
    conds = jnp.concatenate(
        [c_ctx[None, :], c, jnp.zeros((N_COND - 1 - DEC_BATCH, D_MODEL), F32)], axis=0)
    mod = _ada(conds, w_ada, b_ada)
    tables = jnp.asarray(_rope_tables())
    qg = jnp.tile(gqa_qnorm, (1, 4))
    kg = jnp.tile(gqa_knorm, (1, 2))
    aparams = (diff_lq1, diff_lk1, diff_lq2, diff_lk2, diff_subln.T)
    cparams = (sconv_w, ccm_dw_w, ccm_dw_b, ccm_ln_g, ccm_ln_b, ccm_pw)
    cache_k, cache_vt = _cache_planes(cache_diff_k, cache_diff_v, cache_gqa_k, cache_gqa_v)

    xs = (x_prompt.reshape(N_CTX, D_MODEL), x_sample.reshape(N_LAT, D_MODEL))
    own = None
    for l in range(DEPTH):
        (x,) = _ffn(xs, mod, norm_pre, norm_post, ffn1_gate, ffn1_up, ffn1_down, l, 0)

        cin, qt, k_all, vt_all, *own = _mixproj(
            x, mod, norm_pre, w_mix_in, qg, kg, tables, own, l)

        att = _attn_ctx(qt, k_all, vt_all, aparams, l)
        att = _attn_lat(qt, k_all, vt_all, cache_k, cache_vt, att, aparams, l)

        ac = _conv(cin, cparams, l)

        xs = _ffn((x,), mod, norm_pre, norm_post, ffn2_gate, ffn2_up, ffn2_down, l, 2,
                  split_out=(l == DEPTH - 1), mix=(ac, att, w_mix_out))

    y_prompt, y_sample = xs
    odk, odv, ogk, ogv = own
    return (y_prompt.reshape(BATCH, SEQ, D_MODEL),
            y_sample.reshape(DEC_BATCH, DEC_SEQ, D_MODEL),
            odk.reshape(BATCH, DEPTH, SEQ, 4, 2, DK_DIFF),
            odv.reshape(BATCH, DEPTH, SEQ, 4, DV_DIFF),
            ogk.reshape(BATCH, DEPTH, SEQ, 2, HD_GQA),
            ogv.reshape(BATCH, DEPTH, SEQ, 2, HD_GQA))
```

```python
import functools
import math

import numpy as np

import jax
import jax.numpy as jnp
from jax import lax
from jax.experimental import pallas as pl
from jax.experimental.pallas import tpu as pltpu

F32 = jnp.float32
BF16 = jnp.bfloat16

D_MODEL = 1024
BATCH = 16
SEQ = 256
DEPTH = 2
DEC_BATCH = 2
DEC_SEQ = 2048
PAST_LEN = 256
GRID_W = 64
GRID_H = DEC_SEQ // GRID_W
ROPE_THETA = 10000.0
EPS = 1e-6
N_MOD = 9
D_FF = 2816
DK_DIFF = 32
DV_DIFF = 64
HD_GQA = 64
CCM_K = 31
SCONV_K = 3
MIX_IN = 2560

N_CTX = BATCH * SEQ
N_LAT = DEC_BATCH * DEC_SEQ
N_TOK = N_CTX + N_LAT
N_COND = 8
S_LAT = DEC_SEQ + PAST_LEN

LOG2E = math.log2(math.e)

VMEM_LIMIT = 56 * 1024 * 1024

TM_FFN = 512
FFN_SUB = 256
FFN_CHUNK = 256
FFN_STAGE_SLOTS = 3
TM_MIX = 1024
MIX_SUB = 256
ADA_VECS = 3
TQ_LAT = 256
CTX_SEQS_PER_STEP = 4
T_CONV = 2048
R_CONV = 128
CONV_HALO = 16


def _cond_of_tile(i, tm):
    n_ctx_tiles = N_CTX // tm
    per_b = DEC_SEQ // tm
    return jnp.where(i < n_ctx_tiles, 0, 1 + (i - n_ctx_tiles) // per_b)


def _mod_row(mod_ref, k, cond):
    return mod_ref[k, pl.ds(cond, 1), :]


def _dot(a, b):
    return jnp.dot(a, b, preferred_element_type=F32)


def _rms(x, g):
    ms = jnp.mean(x * x, axis=-1, keepdims=True)
    return x * lax.rsqrt(ms + EPS) * g


def _group_mean(xx, width, group):
    r = lax.broadcasted_iota(jnp.int32, (width, width), 0) // group
    c = lax.broadcasted_iota(jnp.int32, (width, width), 1) // group
    gmat = jnp.where(r == c, 1.0 / group, 0.0).astype(BF16)
    hi = xx.astype(BF16)
    lo = (xx - hi.astype(F32)).astype(BF16)
    return _dot(hi, gmat) + _dot(lo, gmat)


def _const_spec(shape):
    return pl.BlockSpec(shape, lambda *_: (0,) * len(shape))


def _layer_spec(shape, l, **kw):
    return pl.BlockSpec((None,) + shape, lambda *_: (l,) + (0,) * len(shape), **kw)


_MOD_SHAPE = (N_MOD, N_COND, D_MODEL)


def _ada_kernel(c_ref, w_ref, b_ref, o_ref):
    cnd = c_ref[...]
    s = (cnd * jax.nn.sigmoid(cnd)).astype(BF16)
    for k in range(ADA_VECS):
        cols = slice(k * D_MODEL, (k + 1) * D_MODEL)
        o_ref[k] = _dot(s, w_ref[:, cols].astype(BF16)) + b_ref[k]


def _ada(conds, w_ada, b_ada):
    return pl.pallas_call(
        _ada_kernel,
        grid=(DEPTH, N_MOD // ADA_VECS),
        in_specs=[
            pl.BlockSpec((N_COND, D_MODEL), lambda l, n: (0, 0)),
            pl.BlockSpec((None, D_MODEL, ADA_VECS * D_MODEL), lambda l, n: (l, 0, n)),
            pl.BlockSpec((None, ADA_VECS, 1, D_MODEL), lambda l, n: (l, n, 0, 0)),
        ],
        out_specs=pl.BlockSpec((None, ADA_VECS, N_COND, D_MODEL), lambda l, n: (l, n, 0, 0)),
        out_shape=jax.ShapeDtypeStruct((DEPTH,) + _MOD_SHAPE, F32),
        compiler_params=pltpu.CompilerParams(
            dimension_semantics=("arbitrary", "arbitrary"), vmem_limit_bytes=VMEM_LIMIT),
        name="ada",
    )(conds, w_ada, b_ada.reshape(DEPTH, N_MOD, 1, D_MODEL))


class _WeightStager:
    def __init__(self, jobs, stage_ref, sem_ref):
        self.jobs, self.stage_ref, self.depth = jobs, stage_ref, stage_ref.shape[0]
        self.copies = [
            pltpu.make_async_copy(src, stage_ref.at[k % self.depth], sem_ref.at[k % self.depth])
            for k, (src, _) in enumerate(jobs)]

    def prime(self):
        for k in range(min(self.depth - 1, len(self.jobs))):
            self.copies[k].start()

    def finish(self, k):
        self.copies[k].wait()
        self.jobs[k][1](self.stage_ref[k % self.depth].astype(BF16))
        if k + self.depth - 1 < len(self.jobs):
            self.copies[k + self.depth - 1].start()


def _ffn_kernel(l, sub, tm, n_x, n_o, mix, *refs):
    refs = list(refs)
    x_refs = [refs.pop(0) for _ in range(n_x)]
    ac_ref, att_ref = (refs.pop(0), refs.pop(0)) if mix else (None, None)
    mod_ref, gpre_ref, gpost_ref, wg_hbm, wu_hbm, wd_hbm = [refs.pop(0) for _ in range(6)]
    wo_hbm = refs.pop(0) if mix else None
    o_refs = [refs.pop(0) for _ in range(n_o)]
    wg_ref, wu_ref, wd_ref = [refs.pop(0) for _ in range(3)]
    wo_ref = refs.pop(0) if mix else None
    i = pl.program_id(0)
    is_ctx = i < N_CTX // tm
    cond = _cond_of_tile(i, tm)
    c = FFN_CHUNK
    n_chunks = D_FF // c
    gate = _mod_row(mod_ref, 3 * sub + 2, cond)
    gpost = gpost_ref[pl.ds(sub, 1), :]

    def prep():
        if n_x == 1:
            x = x_refs[0][...]
        else:
            x = jnp.where(is_ctx, x_refs[0][...], x_refs[1][...])
        if mix:
            cat = jnp.concatenate([ac_ref[:, 0:256], att_ref[:, 0:256],
                                   ac_ref[:, 256:512], att_ref[:, 256:512]], axis=1)
            ym = _dot(cat, wo_ref[...])
            x = x + _mod_row(mod_ref, 5, cond) * _rms(ym, gpost_ref[pl.ds(1, 1), :])
        shift = _mod_row(mod_ref, 3 * sub, cond)
        scale = _mod_row(mod_ref, 3 * sub + 1, cond)
        h = (_rms(x, gpre_ref[pl.ds(sub, 1), :]) * (1.0 + scale) + shift).astype(BF16)
        return x, h

    def swiglu(hh, cols):
        g = _dot(hh, wg_ref[:, cols])
        u = _dot(hh, wu_ref[:, cols])
        a = ((g * jax.nn.sigmoid(g)) * u).astype(BF16)
        return _dot(a, wd_ref[cols, :])

    def emit(y):
        if n_o == 1:
            o_refs[0][...] = y
        else:
            @pl.when(is_ctx)
            def _():
                o_refs[0][...] = y

            @pl.when(jnp.logical_not(is_ctx))
            def _():
                o_refs[1][...] = y

    @pl.when(i == 0)
    def _():
        def col_store(dst, j):
            def store(v):
                dst[:, j * c:(j + 1) * c] = v
            return store

        def row_store(j):
            def store(v):
                wd_ref[j * c:(j + 1) * c, :] = v
            return store

        up_jobs = [(src.at[l, :, pl.ds(j * c, c)], col_store(dst, j))
                   for j in range(n_chunks) for src, dst in ((wg_hbm, wg_ref), (wu_hbm, wu_ref))]
        down_jobs = [(wd_hbm.at[l, pl.ds(j * c, c), :], row_store(j)) for j in range(n_chunks)]

        def run(stage_up, stage_down, sem_up, sem_down):
            up = _WeightStager(up_jobs, stage_up, sem_up)
            up.prime()
            if mix:
                def wo_store(j):
                    def store(v):
                        wo_ref[j * c:(j + 1) * c, :] = v
                    return store

                wo = _WeightStager(
                    [(wo_hbm.at[l, pl.ds(j * c, c), :], wo_store(j)) for j in range(D_MODEL // c)],
                    stage_down, sem_down)
                wo.prime()
                for j in range(D_MODEL // c):
                    wo.finish(j)
            down = _WeightStager(down_jobs, stage_down, sem_down)
            down.prime()
            x, h = prep()
            y = None
            for j in range(n_chunks):
                up.finish(2 * j)
                up.finish(2 * j + 1)
                down.finish(j)
                part = swiglu(h, slice(j * c, (j + 1) * c))
                y = part if y is None else y + part
            emit(x + 0.5 * (gate * _rms(y, gpost)))

        pl.run_scoped(run,
                      pltpu.VMEM((2 * FFN_STAGE_SLOTS, D_MODEL, c), F32),
                      pltpu.VMEM((FFN_STAGE_SLOTS, c, D_MODEL), F32),
                      pltpu.SemaphoreType.DMA((2 * FFN_STAGE_SLOTS,)),
                      pltpu.SemaphoreType.DMA((FFN_STAGE_SLOTS,)))

    @pl.when(i != 0)
    def _():
        x, h = prep()
        outs = []
        for s in range(tm // FFN_SUB):
            rows = slice(s * FFN_SUB, (s + 1) * FFN_SUB)
            y = swiglu(h[rows], slice(None))
            outs.append(x[rows] + 0.5 * (gate * _rms(y, gpost)))
        emit(jnp.concatenate(outs, axis=0))


def _ffn(xs, mod, norm_pre, norm_post, wg, wu, wd, l, sub, split_out=False, mix=None):
    tm = TM_FFN
    n_ctx_tiles = N_CTX // tm
    ctx_spec = pl.BlockSpec((tm, D_MODEL), lambda i: (jnp.minimum(i, n_ctx_tiles - 1), 0))
    lat_spec = pl.BlockSpec((tm, D_MODEL), lambda i: (jnp.maximum(i - n_ctx_tiles, 0), 0))
    full_spec = pl.BlockSpec((tm, D_MODEL), lambda i: (i, 0))
    x_specs = [full_spec] if len(xs) == 1 else [ctx_spec, lat_spec]
    if split_out:
        out_specs = [ctx_spec, lat_spec]
        out_shape = [jax.ShapeDtypeStruct((N_CTX, D_MODEL), F32),
                     jax.ShapeDtypeStruct((N_LAT, D_MODEL), F32)]
    else:
        out_specs = [full_spec]
        out_shape = [jax.ShapeDtypeStruct((N_TOK, D_MODEL), F32)]
    hbm = pl.BlockSpec(memory_space=pl.ANY)
    args = list(xs)
    in_specs = list(x_specs)
    scratch = [pltpu.VMEM((D_MODEL, D_FF), BF16), pltpu.VMEM((D_MODEL, D_FF), BF16),
               pltpu.VMEM((D_FF, D_MODEL), BF16)]
    if mix is not None:
        args += [mix[0], mix[1]]
        in_specs += [pl.BlockSpec((tm, 512), lambda i: (i, 0))] * 2
    args += [mod, norm_pre, norm_post, wg, wu, wd]
    in_specs += [_layer_spec(_MOD_SHAPE, l), _layer_spec((3, D_MODEL), l),
                 _layer_spec((3, D_MODEL), l), hbm, hbm, hbm]
    if mix is not None:
        args.append(mix[2])
        in_specs.append(hbm)
        scratch.append(pltpu.VMEM((D_MODEL, D_MODEL), BF16))
    return pl.pallas_call(
        functools.partial(_ffn_kernel, l, sub, tm, len(xs), len(out_specs), mix is not None),
        grid=(N_TOK // tm,),
        in_specs=in_specs,
        out_specs=out_specs,
        out_shape=out_shape,
        scratch_shapes=scratch,
        compiler_params=pltpu.CompilerParams(
            dimension_semantics=("arbitrary",), vmem_limit_bytes=VMEM_LIMIT),
        name=f"ffn{sub}",
    )(*args)


def _swap_blocks(x, blk):
    w = x.shape[-1]
    lane = lax.broadcasted_iota(jnp.int32, x.shape, x.ndim - 1)
    first = (lane % (2 * blk)) < blk
    return jnp.where(first, pltpu.roll(x, w - blk, x.ndim - 1), pltpu.roll(x, blk, x.ndim - 1))


def _rope_tables():
    out = []
    for rot_dim in (DK_DIFF, HD_GQA):
        n_freq = rot_dim // 4
        inv = ROPE_THETA ** (-np.arange(n_freq, dtype=np.float64) / n_freq)
        lane = np.arange(256) % rot_dim
        pos = lane % (rot_dim // 2)
        freq = inv[pos % n_freq]
        sign = np.where(pos < n_freq, -1.0, 1.0)
        coord = np.concatenate([np.arange(GRID_H), np.arange(GRID_W)]).astype(np.float64)
        ang = coord[:, None] * freq[None, :]
        out.append(np.cos(ang))
        out.append(np.sin(ang) * sign[None, :])
    return np.stack(out).astype(np.float32)


def _tile_rope(tab_ref, which, rot_dim, first_grid_row, n_grid_rows, identity):
    col_part = tab_ref[which, GRID_H:GRID_H + GRID_W, :]
    lane = lax.broadcasted_iota(jnp.int32, (1, 256), 1)
    on_row_axis = (lane % rot_dim) < rot_dim // 2
    tab = jnp.concatenate(
        [jnp.where(on_row_axis, tab_ref[which, pl.ds(first_grid_row + r, 1), :], col_part)
         for r in range(n_grid_rows)], axis=0)
    return jnp.where(identity, 1.0 - (which % 2), tab)


def _mixproj_kernel(l, has_prev, x_ref, mod_ref, gpre_ref, w_ref, qg_ref, kg_ref, tab_ref, *rest):
    n_alias = 4 if has_prev else 0
    (cin_ref, qt_ref, k_ref, vt_ref, odk_ref, odv_ref, ogk_ref, ogv_ref, wbf_ref) = rest[n_alias:]
    tm = TM_MIX
    i = pl.program_id(0)
    n_ctx_tiles = N_CTX // tm
    is_ctx = i < n_ctx_tiles
    cond = _cond_of_tile(i, tm)

    @pl.when(i == 0)
    def _():
        wbf_ref[...] = w_ref[...].astype(BF16)

    shift = _mod_row(mod_ref, 3, cond)
    scale = _mod_row(mod_ref, 4, cond)
    gpre = gpre_ref[pl.ds(1, 1), :]
    sub = MIX_SUB
    grid_rows = sub // GRID_W
    seqs = sub // SEQ
    tile_row = (jnp.maximum(i - n_ctx_tiles, 0) % (DEC_SEQ // tm)) * (tm // GRID_W)
    own = []

    for s in range(tm // sub):
        rows = slice(s * sub, (s + 1) * sub)
        h = (_rms(x_ref[rows, :], gpre) * (1.0 + scale) + shift).astype(BF16)

        a = _dot(h, wbf_ref[:, 0:768])
        cin_ref[rows, 0:256] = a[:, 0:256]
        cin_ref[rows, 256:512] = a[:, 256:512] * a[:, 512:768]
        c = _dot(h, wbf_ref[:, 1536:2048])
        cin_ref[rows, 512:768] = c[:, 0:256] * jax.nn.sigmoid(c[:, 256:512])

        first_row = tile_row + s * grid_rows
        cd = _tile_rope(tab_ref, 0, DK_DIFF, first_row, grid_rows, is_ctx)
        sd = _tile_rope(tab_ref, 1, DK_DIFF, first_row, grid_rows, is_ctx)
        cg = _tile_rope(tab_ref, 2, HD_GQA, first_row, grid_rows, is_ctx)
        sg = _tile_rope(tab_ref, 3, HD_GQA, first_row, grid_rows, is_ctx)

        b = _dot(h, wbf_ref[:, 768:1536])
        bq, bk, bv = b[:, 0:256], b[:, 256:512], b[:, 512:768]
        bq_r = (bq * cd + _swap_blocks(bq, DK_DIFF // 4) * sd) * (DK_DIFF ** -0.5 * LOG2E)
        bk_r = bk * cd + _swap_blocks(bk, DK_DIFF // 4) * sd
        qt_ref[0:256, rows] = bq_r.T.astype(BF16)

        d = _dot(h, wbf_ref[:, 2048:2560])
        dq, dk, dv = d[:, 0:256], d[:, 256:384], d[:, 384:512]
        dq_n = dq * lax.rsqrt(_group_mean(dq * dq, 256, HD_GQA) + EPS) * qg_ref[l:l + 1, :]
        dk_n = dk * lax.rsqrt(_group_mean(dk * dk, 128, HD_GQA) + EPS) * kg_ref[l:l + 1, :]
        dq_r = (dq_n * cg + _swap_blocks(dq_n, HD_GQA // 4) * sg) * (HD_GQA ** -0.5 * LOG2E)
        dk_r = dk_n * cg[:, 0:128] + _swap_blocks(dk_n, HD_GQA // 4) * sg[:, 0:128]
        qt_ref[256:512, rows] = dq_r.T.astype(BF16)

        k_ref[0, rows, :] = bk_r.astype(BF16)
        k_ref[1, rows, :] = jnp.concatenate([dk_r, jnp.zeros_like(dk_r)], axis=1).astype(BF16)
        vt_ref[0, :, rows] = bv.T.astype(BF16)
        vt_ref[1, :, rows] = jnp.concatenate([dv, jnp.zeros_like(dv)], axis=1).T.astype(BF16)
        own.append((bk, bv, dk_n, dv))

    @pl.when(is_ctx)
    def _():
        for s, (bk, bv, dk_n, dv) in enumerate(own):
            seq = slice(s * seqs, (s + 1) * seqs)
            odk_ref[seq] = bk.reshape(seqs, SEQ, 256)
            odv_ref[seq] = bv.reshape(seqs, SEQ, 256)
            ogk_ref[seq] = dk_n.reshape(seqs, SEQ, 128)
            ogv_ref[seq] = dv.reshape(seqs, SEQ, 128)


def _mixproj(x, mod, norm_pre, w_mix_in, qg, kg, tables, prev_own, l):
    tm = TM_MIX
    n_ctx_tiles = N_CTX // tm
    seqs = tm // SEQ

    ctx_row = lambda i: jnp.minimum(i, n_ctx_tiles - 1)
    row = lambda w: pl.BlockSpec((tm, w), lambda i: (i, 0))
    own_spec = lambda w: pl.BlockSpec((seqs, None, SEQ, w), lambda i: (ctx_row(i), l, 0, 0))
    own_shape = lambda w: jax.ShapeDtypeStruct((BATCH, DEPTH, SEQ, w), F32)
    any_spec = pl.BlockSpec(memory_space=pl.ANY)
    aliased = list(prev_own) if prev_own is not None else []
    n_in = 7
    aliases = {n_in + k: 4 + k for k in range(len(aliased))}
    return pl.pallas_call(
        functools.partial(_mixproj_kernel, l, prev_own is not None),
        grid=(N_TOK // tm,),
        in_specs=[
            row(D_MODEL),
            _layer_spec(_MOD_SHAPE, l),
            _layer_spec((3, D_MODEL), l),
            _layer_spec((D_MODEL, MIX_IN), l, pipeline_mode=pl.Buffered(1)),
            _const_spec((DEPTH, 256)),
            _const_spec((DEPTH, 128)),
            _const_spec(tables.shape),
        ] + [any_spec] * len(aliased),
        out_specs=[row(768),
                   pl.BlockSpec((512, tm), lambda i: (0, i)),
                   pl.BlockSpec((2, tm, 256), lambda i: (0, i, 0)),
                   pl.BlockSpec((2, 256, tm), lambda i: (0, 0, i)),
                   own_spec(256), own_spec(256), own_spec(128), own_spec(128)],
        out_shape=[
            jax.ShapeDtypeStruct((N_TOK, 768), F32),
            jax.ShapeDtypeStruct((512, N_TOK), BF16),
            jax.ShapeDtypeStruct((2, N_TOK, 256), BF16),
            jax.ShapeDtypeStruct((2, 256, N_TOK), BF16),
            own_shape(256), own_shape(256), own_shape(128), own_shape(128),
        ],
        scratch_shapes=[pltpu.VMEM((D_MODEL, MIX_IN), BF16)],
        input_output_aliases=aliases,
        compiler_params=pltpu.CompilerParams(
            dimension_semantics=("arbitrary",), vmem_limit_bytes=VMEM_LIMIT),
        name="mixproj",
    )(x, mod, norm_pre, w_mix_in, qg, kg, tables, *aliased)


N_UNITS = 12
ATTN_SLOTS = 4
ATTN_PIECE = 1024


def _attn_kernel(l, qt_ref, k_ref, vt_ref, kc_ref, vtc_ref, lq1_ref, lk1_ref, lq2_ref, lk2_ref,
                 subln_ref, *rest):
    n = ATTN_SLOTS
    o_ref, qs_ref = rest[-3 - 3 * n], rest[-2 - 3 * n]
    s_refs = rest[-1 - 3 * n:-1 - 2 * n]
    e_refs = rest[-1 - 2 * n:-1 - n]
    il_refs = rest[-1 - n:-1]
    oacc_ref = rest[-1]
    own = k_ref.shape[1]
    half = own // 2

    for u in range(N_UNITS):
        qs_ref[u] = _unit_queries(qt_ref, u)

    def scores(u):
        plane = 0 if u < 8 else 1
        s_ref = s_refs[u % n]
        for lo in range(0, own, ATTN_PIECE):
            s_ref[lo:lo + ATTN_PIECE, :] = _dot(k_ref[plane, lo:lo + ATTN_PIECE, :], qs_ref[u])
        s_ref[own:, :] = _dot(kc_ref[plane], qs_ref[u])

    def softmax(u):
        s = s_refs[u % n][...]
        e = jnp.exp2(s - jnp.max(s, axis=0, keepdims=True))
        il_refs[u % n][...] = 1.0 / jnp.sum(e, axis=0, keepdims=True)
        e_refs[u % n][...] = e.astype(BF16)

    def values(u):
        plane = 0 if u < 8 else 1
        vrow = DV_DIFF * (u // 2) if u < 8 else HD_GQA * ((u - 8) // 2)
        e_ref = e_refs[u % n]
        o = (_dot(vt_ref[plane, vrow:vrow + DV_DIFF, 0:half], e_ref[0:half, :])
             + _dot(vt_ref[plane, vrow:vrow + DV_DIFF, half:], e_ref[half:own, :])
             + _dot(vtc_ref[plane, vrow:vrow + DV_DIFF, :], e_ref[own:, :]))
        oacc_ref[u] = o * il_refs[u % n][...]

    lag = ATTN_SLOTS // 2
    for t in range(N_UNITS + 2 * lag):
        if t < N_UNITS:
            scores(t)
        if lag <= t < N_UNITS + lag:
            softmax(t - lag)
        if t >= 2 * lag:
            values(t - 2 * lag)

    _attn_finish(l, [oacc_ref[u] for u in range(N_UNITS)],
                 lq1_ref, lk1_ref, lq2_ref, lk2_ref, subln_ref, o_ref)


def _unit_queries(qt_ref, u, cols=slice(None)):
    if u < 8:
        feat = lax.broadcasted_iota(jnp.int32, (256, 1), 0)
        return jnp.where(feat // DK_DIFF == u, qt_ref[0:256, cols], jnp.zeros((), BF16))
    hd = u - 8
    g = hd // 2
    rows = qt_ref[256 + HD_GQA * hd:256 + HD_GQA * (hd + 1), cols]
    tq = rows.shape[1]
    pieces = ([jnp.zeros((HD_GQA * g, tq), BF16)] if g else []) + [rows]
    pieces.append(jnp.zeros((256 - HD_GQA * (g + 1), tq), BF16))
    return jnp.concatenate(pieces, axis=0)


def _attn_finish(l, outs, lq1_ref, lk1_ref, lq2_ref, lk2_ref, subln_ref, o_ref, rows=slice(None)):
    lam_init = 0.8 - 0.6 * math.exp(-0.3 * l)
    row = slice(l, l + 1)
    lam = (jnp.exp(jnp.sum(lq1_ref[row, :] * lk1_ref[row, :], axis=-1, keepdims=True))
           - jnp.exp(jnp.sum(lq2_ref[row, :] * lk2_ref[row, :], axis=-1, keepdims=True))
           + lam_init)
    heads = []
    for hd in range(4):
        o = outs[2 * hd] - lam * outs[2 * hd + 1]
        nrm = lax.rsqrt(jnp.mean(o * o, axis=0, keepdims=True) + EPS)
        heads.append(o * nrm * (subln_ref[:, l:l + 1] * (1.0 - lam_init)))
    heads += outs[8:12]
    o_ref[rows, :] = jnp.concatenate(heads, axis=0).T.astype(BF16)


def _attn_ctx_kernel(l, qt_ref, k_ref, vt_ref, lq1_ref, lk1_ref, lq2_ref, lk2_ref, subln_ref,
                     o_ref):
    for q in range(qt_ref.shape[1] // SEQ):
        tok = slice(q * SEQ, (q + 1) * SEQ)
        w_diff = jnp.concatenate([_unit_queries(qt_ref, u, tok) for u in range(8)], axis=1)
        w_gqa = jnp.concatenate([_unit_queries(qt_ref, u, tok) for u in range(8, N_UNITS)], axis=1)
        s = jnp.concatenate([_dot(k_ref[0, tok, :], w_diff), _dot(k_ref[1, tok, :], w_gqa)],
                            axis=1)
        e = jnp.exp2(s - jnp.max(s, axis=0, keepdims=True))
        il = 1.0 / jnp.sum(e, axis=0, keepdims=True)
        e = e.astype(BF16)
        outs = []
        for u in range(N_UNITS):
            plane = 0 if u < 8 else 1
            vrow = DV_DIFF * (u // 2) if u < 8 else HD_GQA * ((u - 8) // 2)
            cols = slice(u * SEQ, (u + 1) * SEQ)
            outs.append(_dot(vt_ref[plane, vrow:vrow + DV_DIFF, tok], e[:, cols]) * il[:, cols])
        _attn_finish(l, outs, lq1_ref, lk1_ref, lq2_ref, lk2_ref, subln_ref, o_ref, tok)


def _attn_param_specs():
    vec = _const_spec((DEPTH, DK_DIFF))
    return [vec, vec, vec, vec, _const_spec((DV_DIFF, DEPTH))]


def _attn_scratch(tq, s_len):
    n = ATTN_SLOTS
    return ([pltpu.VMEM((N_UNITS, 256, tq), BF16)]
            + [pltpu.VMEM((s_len, tq), F32)] * n
            + [pltpu.VMEM((s_len, tq), BF16)] * n
            + [pltpu.VMEM((1, tq), F32)] * n
            + [pltpu.VMEM((N_UNITS, DV_DIFF, tq), F32)])


def _attn_ctx(qt, k_ctx, vt_ctx, aparams, l):
    t = CTX_SEQS_PER_STEP * SEQ
    return pl.pallas_call(
        functools.partial(_attn_ctx_kernel, l),
        grid=(N_CTX // t,),
        in_specs=[pl.BlockSpec((512, t), lambda b: (0, b)),
                  pl.BlockSpec((2, t, 256), lambda b: (0, b, 0)),
                  pl.BlockSpec((2, 256, t), lambda b: (0, 0, b))] + _attn_param_specs(),
        out_specs=pl.BlockSpec((t, 512), lambda b: (b, 0)),
        out_shape=jax.ShapeDtypeStruct((N_TOK, 512), BF16),
        compiler_params=pltpu.CompilerParams(
            dimension_semantics=("arbitrary",), vmem_limit_bytes=VMEM_LIMIT),
        name="attn_ctx",
    )(qt, k_ctx, vt_ctx, *aparams)


def _attn_lat(qt, k_all, vt_all, cache_k, cache_vt, att, aparams, l):
    tq = TQ_LAT
    nq = DEC_SEQ // tq
    tile = lambda b, i: N_CTX // tq + b * nq + i
    own = lambda b: N_CTX // DEC_SEQ + b
    return pl.pallas_call(
        functools.partial(_attn_kernel, l),
        grid=(DEC_BATCH, nq),
        in_specs=[pl.BlockSpec((512, tq), lambda b, i: (0, tile(b, i))),
                  pl.BlockSpec((2, DEC_SEQ, 256), lambda b, i: (0, own(b), 0)),
                  pl.BlockSpec((2, 256, DEC_SEQ), lambda b, i: (0, 0, own(b))),
                  pl.BlockSpec((None, None, 2, PAST_LEN, 256), lambda b, i: (b, l, 0, 0, 0)),
                  pl.BlockSpec((None, None, 2, 256, PAST_LEN), lambda b, i: (b, l, 0, 0, 0))]
        + _attn_param_specs() + [pl.BlockSpec(memory_space=pl.ANY)],
        out_specs=pl.BlockSpec((tq, 512), lambda b, i: (tile(b, i), 0)),
        out_shape=jax.ShapeDtypeStruct((N_TOK, 512), BF16),
        scratch_shapes=_attn_scratch(tq, S_LAT),
        input_output_aliases={10: 0},
        compiler_params=pltpu.CompilerParams(
            dimension_semantics=("arbitrary", "arbitrary"), vmem_limit_bytes=VMEM_LIMIT),
        name="attn_lat",
    )(qt, k_all, vt_all, cache_k, cache_vt, *aparams, att)


def _conv_kernel(l, *refs):
    is_ctx = pl.program_id(0) < N_CTX // T_CONV

    @pl.when(is_ctx)
    def _():
        _conv_tile(l, SEQ, *refs)

    @pl.when(jnp.logical_not(is_ctx))
    def _():
        _conv_tile(l, DEC_SEQ, *refs)


def _conv_tile(l, seq_len, cin_ref, wsc_ref, wdw_ref, bdw_ref, lng_ref, lnb_ref, pw_ref, o_ref,
               pad_ref):
    t_tile = cin_ref.shape[0]
    n_seq = t_tile // seq_len
    stride = seq_len + CONV_HALO
    chunks_per_seq = seq_len // R_CONV
    rows = R_CONV

    for q in range(n_seq + 1):
        pad_ref[q * stride:q * stride + CONV_HALO, :] = jnp.zeros((CONV_HALO, 512), F32)
    for q in range(n_seq):
        pad_ref[CONV_HALO + q * stride:CONV_HALO + q * stride + seq_len, :] = (
            cin_ref[q * seq_len:(q + 1) * seq_len, 256:768])

    pw = pw_ref[...].astype(BF16)
    bdw = bdw_ref[l:l + 1, :]
    lng = lng_ref[l:l + 1, :]
    lnb = lnb_ref[l:l + 1, :]

    def body(c, carry):
        q = c // chunks_per_seq
        r = c % chunks_per_seq
        base = pl.multiple_of(q * stride + r * rows, 8)
        row0 = pl.multiple_of(c * rows, 8)
        win = pad_ref[pl.ds(base, rows + 2 * CONV_HALO), :]
        win_s = win[:, 0:256]
        win_u = win[:, 256:512]

        s_acc = None
        for k in range(SCONV_K):
            off = CONV_HALO + k - SCONV_K // 2
            term = wsc_ref[pl.ds(k, 1), :] * win_s[off:off + rows]
            s_acc = term if s_acc is None else s_acc + term
        a_out = cin_ref[pl.ds(row0, rows), 0:256] * s_acc

        acc = None
        for res in range(8):
            part = None
            for a in range(-2, 2):
                k = 8 * a + res + CCM_K // 2
                if 0 <= k < CCM_K:
                    lo = CONV_HALO + 8 * a
                    term = wdw_ref[pl.ds(k, 1), :] * win_u[lo:lo + rows + 8]
                    part = term if part is None else part + term
            part = part[res:res + rows]
            acc = part if acc is None else acc + part
        u = acc + bdw
        mu = jnp.mean(u, axis=-1, keepdims=True)
        var = jnp.mean(jnp.square(u - mu), axis=-1, keepdims=True)
        y = (u - mu) * lax.rsqrt(var + EPS) * lng + lnb
        y = y * jax.nn.sigmoid(y)
        c_out = _dot(y.astype(BF16), pw)
        o_ref[pl.ds(row0, rows), 0:256] = a_out.astype(BF16)
        o_ref[pl.ds(row0, rows), 256:512] = c_out.astype(BF16)
        return carry

    lax.fori_loop(0, t_tile // rows, body, 0, unroll=2)


def _conv(cin, cparams, l):
    t = T_CONV
    vec = _const_spec((DEPTH, 256))
    pad_rows = max((t // s) * (s + CONV_HALO) + CONV_HALO for s in (SEQ, DEC_SEQ))
    return pl.pallas_call(
        functools.partial(_conv_kernel, l),
        grid=(N_TOK // t,),
        in_specs=[pl.BlockSpec((t, 768), lambda i: (i, 0)),
                  _layer_spec((SCONV_K, 256), l), _layer_spec((CCM_K, 256), l), vec, vec, vec,
                  _layer_spec((256, 256), l)],
        out_specs=pl.BlockSpec((t, 512), lambda i: (i, 0)),
        out_shape=jax.ShapeDtypeStruct((N_TOK, 512), BF16),
        scratch_shapes=[pltpu.VMEM((pad_rows, 512), F32)],
        compiler_params=pltpu.CompilerParams(
            dimension_semantics=("arbitrary",), vmem_limit_bytes=VMEM_LIMIT),
        name="conv",
    )(cin, *cparams)


def _cache_planes(cache_diff_k, cache_diff_v, cache_gqa_k, cache_gqa_v):
    lead = (DEC_BATCH, DEPTH, PAST_LEN)
    pad128 = lambda a: jnp.pad(a.reshape(lead + (128,)), ((0, 0), (0, 0), (0, 0), (0, 128)))
    keys = jnp.stack([cache_diff_k.reshape(lead + (256,)), pad128(cache_gqa_k)], axis=2)
    vals = jnp.stack([cache_diff_v.reshape(lead + (256,)), pad128(cache_gqa_v)], axis=2)
    return keys.astype(BF16), jnp.swapaxes(vals, 3, 4).astype(BF16)


def kernel(x_prompt, x_sample, cache_diff_k, cache_diff_v, cache_gqa_k, cache_gqa_v, c, c_ctx, w_ada, b_ada, norm_pre, norm_post, ffn1_gate, ffn1_up, ffn1_down, ffn2_gate, ffn2_up, ffn2_down, w_mix_in, w_mix_out, sconv_w, diff_lq1, diff_lk1, diff_lq2, diff_lk2, diff_subln, ccm_dw_w, ccm_dw_b, ccm_ln_g, ccm_ln_b, ccm_pw, gqa_qnorm, gqa_knorm):
    conds = jnp.concatenate(
        [c_ctx[None, :], c, jnp.zeros((N_COND - 1 - DEC_BATCH, D_MODEL), F32)], axis=0)
    mod = _ada(conds, w_ada, b_ada)
    tables = jnp.asarray(_rope_tables())
    qg = jnp.tile(gqa_qnorm, (1, 4))
    kg = jnp.tile(gqa_knorm, (1, 2))
    aparams = (diff_lq1, diff_lk1, diff_lq2, diff_lk2, diff_subln.T)
    cparams = (sconv_w, ccm_dw_w, ccm_dw_b, ccm_ln_g, ccm_ln_b, ccm_pw)
    cache_k, cache_vt = _cache_planes(cache_diff_k, cache_diff_v, cache_gqa_k, cache_gqa_v)

    xs = (x_prompt.reshape(N_CTX, D_MODEL), x_sample.reshape(N_LAT, D_MODEL))
    own = None
    for l in range(DEPTH):
        (x,) = _ffn(xs, mod, norm_pre, norm_post, ffn1_gate, ffn1_up, ffn1_down, l, 0)

        cin, qt, k_all, vt_all, *own = _mixproj(
            x, mod, norm_pre, w_mix_in, qg, kg, tables, own, l)

        att = _attn_ctx(qt, k_all, vt_all, aparams, l)
        att = _attn_lat(qt, k_all, vt_all, cache_k, cache_vt, att, aparams, l)

        ac = _conv(cin, cparams, l)

        xs = _ffn((x,), mod, norm_pre, norm_post, ffn2_gate, ffn2_up, ffn2_down, l, 2,
                  split_out=(l == DEPTH - 1), mix=(ac, att, w_mix_out))

    y_prompt, y_sample = xs
    odk, odv, ogk, ogv = own
    return (y_prompt.reshape(BATCH, SEQ, D_MODEL),
            y_sample.reshape(DEC_BATCH, DEC_SEQ, D_MODEL),
            odk.reshape(BATCH, DEPTH, SEQ, 4, 2, DK_DIFF),
            odv.reshape(BATCH, DEPTH, SEQ, 4, DV_DIFF),
            ogk.reshape(BATCH, DEPTH, SEQ, 2, HD_GQA),
            ogv.reshape(BATCH, DEPTH, SEQ, 2, HD_GQA))
```
